```python
import functools
import jax, jax.numpy as jnp
from jax import lax
import numpy as np

D_MODEL = 2048
BATCH = 1
SEQ = 8192
DEPTH = 4
DEC_BATCH = 32
DEC_SEQ = 16
PAST_LEN = 2048

CHUNK = 64
N_MIXERS = 3
EPS = 1e-6
NEG_INF = -1e30
RET_HEADS = 8
RET_DK = 256
RET_DV = 512
ROPE_BASE = 10000.0
ATT_HEADS = 16
ATT_DH = 128
BAND_PAST_CHUNKS = 8
BAND_PAST = BAND_PAST_CHUNKS * CHUNK
REL_CLIP = 128
GMLP_CHUNK = 128
GMLP_GROUPS = 8
GMLP_DV = 3 * D_MODEL
D_FF = 5504
N_EXPERTS = 8
TOP_K = 2
D_FF_EXPERT = 7168
D_PLE = 256
N_RET = (DEPTH + 2) // 3
N_ATT = (DEPTH + 1) // 3
N_GMLP = DEPTH // 3
N_DENSE = (DEPTH + 1) // 2
N_MOE = DEPTH // 2

kernel_name = 'hybrid_streaming_encoder_step'

F32 = jnp.float32


def _rmsnorm(x, g):
    xf = x.astype(F32)
    y = xf * lax.rsqrt(jnp.mean(xf * xf, axis=-1, keepdims=True) + EPS)
    return (y * g.astype(F32)).astype(x.dtype)


def _rotary(x, pos):
    d = x.shape[-1]
    freqs = ROPE_BASE ** (-jnp.arange(0, d, 2, dtype=F32) / d)
    ang = pos.astype(F32)[:, None] * freqs[None, :]
    cos = jnp.cos(ang)[None, :, None, :]
    sin = jnp.sin(ang)[None, :, None, :]
    xf = x.astype(F32)
    x1, x2 = xf[..., : d // 2], xf[..., d // 2:]
    return jnp.concatenate([x1 * cos - x2 * sin, x1 * sin + x2 * cos], axis=-1)


def _retention_block(state, qkv, log_g):
    q, k, v = qkv
    L = q.shape[1]
    idx = jnp.arange(L, dtype=F32)
    rel = idx[:, None] - idx[None, :]
    decay = jnp.where(rel[None] >= 0, jnp.exp(jnp.maximum(rel, 0.0)[None] * log_g[:, None, None]), 0.0)
    scores = jnp.einsum('bqhd,bkhd->bhqk', q, k) * decay[None]
    o = jnp.einsum('bhqk,bkhe->bqhe', scores, v)
    q_decay = jnp.exp((idx + 1.0)[:, None] * log_g[None, :])
    o = o + jnp.einsum('bqhd,bhde->bqhe', q, state) * q_decay[None, :, :, None]
    k_decay = jnp.exp((L - 1.0 - idx)[:, None] * log_g[None, :])
    state = (state * jnp.exp(L * log_g)[None, :, None, None]
             + jnp.einsum('bkhd,bkhe->bhde', k * k_decay[None, :, :, None], v))
    return state, o


def _retention(h, pos, state0, w_in, w_out):
    B, S, _ = h.shape
    hk, hv = RET_HEADS * RET_DK, RET_HEADS * RET_DV
    q, k, v, g = jnp.split(h @ w_in, [hk, 2 * hk, 2 * hk + hv], axis=-1)
    q = _rotary(q.reshape(B, S, RET_HEADS, RET_DK), pos)
    k = _rotary(k.reshape(B, S, RET_HEADS, RET_DK), pos) * RET_DK ** -0.5
    v = v.reshape(B, S, RET_HEADS, RET_DV).astype(F32)
    log_g = jnp.log1p(-jnp.exp2(-5.0 - jnp.arange(RET_HEADS, dtype=F32)))
    blk = min(S, CHUNK)
    n = S // blk

    def to_blocks(t):
        return t.reshape(B, n, blk, *t.shape[2:]).swapaxes(0, 1)

    state, o = lax.scan(functools.partial(_retention_block, log_g=log_g), state0.astype(F32),
                        (to_blocks(q), to_blocks(k), to_blocks(v)))
    o = o.swapaxes(0, 1).reshape(B, S, RET_HEADS, RET_DV)
    o = o * lax.rsqrt(jnp.mean(o * o, axis=-1, keepdims=True) + EPS)
    o = o.reshape(B, S, hv).astype(h.dtype) * jax.nn.silu(g)
    return o @ w_out, state.astype(h.dtype)


def _rel_bias(table, n_q, n_k, offset):
    rel = offset + jnp.arange(n_q)[:, None] - jnp.arange(n_k)[None, :]
    b = jnp.take(table, jnp.clip(rel, -REL_CLIP, REL_CLIP) + REL_CLIP, axis=0)
    return b.transpose(2, 0, 1).astype(F32)


def _band_attention_prompt(h, w_in, w_out, table):
    B, S, _ = h.shape
    n, P = S // CHUNK, BAND_PAST_CHUNKS
    q, k, v = jnp.split(h @ w_in, 3, axis=-1)
    qc = q.reshape(B, n, CHUNK, ATT_HEADS, ATT_DH)

    def band(t):
        tc = jnp.pad(t.reshape(B, n, CHUNK, ATT_HEADS, ATT_DH), ((0, 0), (P, 0), (0, 0), (0, 0), (0, 0)))
        return jnp.stack([tc[:, j:j + n] for j in range(P + 1)], axis=2).reshape(
            B, n, (P + 1) * CHUNK, ATT_HEADS, ATT_DH)

    kb, vb = band(k), band(v)
    valid = (jnp.arange(n)[:, None] - P + jnp.arange(P + 1)[None, :]) >= 0
    valid = jnp.repeat(valid, CHUNK, axis=1)
    bias = _rel_bias(table, CHUNK, (P + 1) * CHUNK, P * CHUNK)
    s = jnp.einsum('bcqhd,bckhd->bchqk', qc, kb).astype(F32) * ATT_DH ** -0.5 + bias[None, None]
    s = jnp.where(valid[None, :, None, None, :], s, NEG_INF)
    pr = jax.nn.softmax(s, axis=-1).astype(vb.dtype)
    o = jnp.einsum('bchqk,bckhd->bcqhd', pr, vb).reshape(B, S, ATT_HEADS * ATT_DH)
    w = min(BAND_PAST, S)
    k_rows = k.reshape(B, S, ATT_HEADS, ATT_DH)[:, S - w:]
    v_rows = v.reshape(B, S, ATT_HEADS, ATT_DH)[:, S - w:]
    return o @ w_out, k_rows, v_rows


def _band_attention_sample(h, k_past, v_past, w_in, w_out, table):
    B, L, _ = h.shape
    W = k_past.shape[1]
    q, k, v = jnp.split(h @ w_in, 3, axis=-1)
    q = q.reshape(B, L, ATT_HEADS, ATT_DH)
    k = k.reshape(B, L, ATT_HEADS, ATT_DH)
    v = v.reshape(B, L, ATT_HEADS, ATT_DH)
    keys = jnp.concatenate([k_past.astype(k.dtype), k], axis=1)
    vals = jnp.concatenate([v_past.astype(v.dtype), v], axis=1)
    bias = _rel_bias(table, L, W + L, W)
    s = jnp.einsum('bqhd,bkhd->bhqk', q, keys).astype(F32) * ATT_DH ** -0.5 + bias[None]
    pr = jax.nn.softmax(s, axis=-1).astype(vals.dtype)
    o = jnp.einsum('bhqk,bkhd->bqhd', pr, vals).reshape(B, L, ATT_HEADS * ATT_DH)
    return o @ w_out, k, v


def _gmlp(h, w_in, ln_g, w_s, b_s, w_out):
    B, S, _ = h.shape
    u, v = jnp.split(jax.nn.gelu(h @ w_in), 2, axis=-1)
    vf = v.astype(F32)
    mu = jnp.mean(vf, axis=-1, keepdims=True)
    var = jnp.mean((vf - mu) ** 2, axis=-1, keepdims=True)
    v = ((vf - mu) * lax.rsqrt(var + EPS) * ln_g.astype(F32)).astype(h.dtype)
    blk = min(S, GMLP_CHUNK)
    n = S // blk
    cg = GMLP_DV // GMLP_GROUPS
    mask = jnp.tril(jnp.ones((blk, blk), dtype=bool))
    ws = jnp.where(mask[None], w_s[:, :blk, :blk], 0.0).astype(v.dtype)
    vb = v.reshape(B, n, blk, GMLP_GROUPS, cg)
    mixed = jnp.einsum('gts,bnsgc->bntgc', ws, vb) + b_s[:, :blk].T[None, None, :, :, None]
    y = u * mixed.reshape(B, S, GMLP_DV).astype(u.dtype)
    return y @ w_out, v[:, S - blk:]


def _swiglu(h, w_gu, w_down):
    a, b = jnp.split(h @ w_gu, 2, axis=-1)
    return (jax.nn.silu(a) * b) @ w_down


def _moe(h, w_router, w_gu, w_down):
    logits = (h @ w_router).astype(F32)
    top_v, top_i = lax.top_k(logits, TOP_K)
    wts = jax.nn.softmax(top_v, axis=-1)
    gates = jnp.sum(jax.nn.one_hot(top_i, N_EXPERTS, dtype=F32) * wts[..., None], axis=-2)
    out = jnp.zeros_like(h)
    for e in range(N_EXPERTS):
        out = out + gates[..., e:e + 1].astype(h.dtype) * _swiglu(h, w_gu[e], w_down[e])
    return out


def _trunk(x, p, pos, ret_state, att_k, att_v, w, prompt):
    B = x.shape[0]
    new_ret, new_k, new_v, new_gv = [], [], [], []
    for i in range(DEPTH):
        h = _rmsnorm(x, w['g_mix'][i])
        kind, slot = i % N_MIXERS, i // N_MIXERS
        if kind == 0:
            s0 = jnp.zeros((B, RET_HEADS, RET_DK, RET_DV), F32) if prompt else ret_state[slot]
            out, st = _retention(h, pos, s0, w['ret_w_in'][slot], w['ret_w_out'][slot])
            new_ret.append(st)
        elif kind == 1:
            if prompt:
                out, kr, vr = _band_attention_prompt(h, w['att_w_in'][slot], w['att_w_out'][slot],
                                                     w['att_rel_bias'][slot])
            else:
                out, kr, vr = _band_attention_sample(h, att_k[slot], att_v[slot], w['att_w_in'][slot],
                                                     w['att_w_out'][slot], w['att_rel_bias'][slot])
            new_k.append(kr)
            new_v.append(vr)
        else:
            out, vr = _gmlp(h, w['gmlp_w_in'][slot], w['gmlp_ln_g'][slot], w['gmlp_w_s'][slot],
                            w['gmlp_b_s'][slot], w['gmlp_w_out'][slot])
            new_gv.append(vr)
        x = x + out
        h = _rmsnorm(x, w['g_ffn'][i])
        if i % 2 == 0:
            x = x + _swiglu(h, w['ffn_w_gu'][i // 2], w['ffn_w_down'][i // 2])
        else:
            x = x + _moe(h, w['moe_w_router'][i // 2], w['moe_w_gu'][i // 2], w['moe_w_down'][i // 2])
        gate = jax.nn.sigmoid(_rmsnorm(x, w['g_ple'][i]) @ w['ple_w_gate'][i])
        x = x + gate * (p[i] @ w['ple_w_in'][i])
    y = _rmsnorm(x, w['g_final'])
    return y, jnp.stack(new_ret), jnp.stack(new_k), jnp.stack(new_v), jnp.stack(new_gv)


def setup_inputs(seed: int = 0) -> dict:
    key = jax.random.key(seed)
    ks = iter(jax.random.split(key, 40))

    def nrm(shape, scale=1.0):
        return jax.random.normal(next(ks), shape, F32) * scale

    D = D_MODEL
    hk, hv = RET_HEADS * RET_DK, RET_HEADS * RET_DV
    ha = ATT_HEADS * ATT_DH
    w_cache = min(BAND_PAST, PAST_LEN)
    return {
        'x_prompt': nrm((BATCH, SEQ, D)),
        'x_sample': nrm((DEC_BATCH, DEC_SEQ, D)),
        'state_ret': nrm((N_RET, DEC_BATCH, RET_HEADS, RET_DK, RET_DV), 0.5),
        'cache_k': nrm((N_ATT, DEC_BATCH, w_cache, ATT_HEADS, ATT_DH)),
        'cache_v': nrm((N_ATT, DEC_BATCH, w_cache, ATT_HEADS, ATT_DH)),
        'p_prompt': nrm((DEPTH, BATCH, SEQ, D_PLE)),
        'p_sample': nrm((DEPTH, DEC_BATCH, DEC_SEQ, D_PLE)),
        'g_mix': 1.0 + nrm((DEPTH, D), 0.02),
        'g_ffn': 1.0 + nrm((DEPTH, D), 0.02),
        'g_ple': 1.0 + nrm((DEPTH, D), 0.02),
        'g_final': 1.0 + nrm((D,), 0.02),
        'ret_w_in': nrm((N_RET, D, 2 * hk + 2 * hv), D ** -0.5),
        'ret_w_out': nrm((N_RET, hv, D), hv ** -0.5),
        'att_w_in': nrm((N_ATT, D, 3 * ha), D ** -0.5),
        'att_w_out': nrm((N_ATT, ha, D), ha ** -0.5),
        'att_rel_bias': nrm((N_ATT, 2 * REL_CLIP + 1, ATT_HEADS), 0.1),
        'gmlp_w_in': nrm((N_GMLP, D, 2 * GMLP_DV), D ** -0.5),
        'gmlp_ln_g': 1.0 + nrm((N_GMLP, GMLP_DV), 0.02),
        'gmlp_w_s': nrm((N_GMLP, GMLP_GROUPS, GMLP_CHUNK, GMLP_CHUNK), GMLP_CHUNK ** -0.5),
        'gmlp_b_s': nrm((N_GMLP, GMLP_GROUPS, GMLP_CHUNK), 0.02),
        'gmlp_w_out': nrm((N_GMLP, GMLP_DV, D), GMLP_DV ** -0.5),
        'ffn_w_gu': nrm((N_DENSE, D, 2 * D_FF), D ** -0.5),
        'ffn_w_down': nrm((N_DENSE, D_FF, D), D_FF ** -0.5),
        'moe_w_router': nrm((N_MOE, D, N_EXPERTS), D ** -0.5),
        'moe_w_gu': nrm((N_MOE, N_EXPERTS, D, 2 * D_FF_EXPERT), D ** -0.5),
        'moe_w_down': nrm((N_MOE, N_EXPERTS, D_FF_EXPERT, D), D_FF_EXPERT ** -0.5),
        'ple_w_in': nrm((DEPTH, D_PLE, D), D_PLE ** -0.5),
        'ple_w_gate': nrm((DEPTH, D, D), D ** -0.5),
    }


def reference(x_prompt, x_sample, state_ret, cache_k, cache_v, p_prompt, p_sample, g_mix, g_ffn, g_ple,
              g_final, ret_w_in, ret_w_out, att_w_in, att_w_out, att_rel_bias, gmlp_w_in, gmlp_ln_g,
              gmlp_w_s, gmlp_b_s, gmlp_w_out, ffn_w_gu, ffn_w_down, moe_w_router, moe_w_gu, moe_w_down,
              ple_w_in, ple_w_gate):
    w = dict(g_mix=g_mix, g_ffn=g_ffn, g_ple=g_ple, g_final=g_final, ret_w_in=ret_w_in,
             ret_w_out=ret_w_out, att_w_in=att_w_in, att_w_out=att_w_out, att_rel_bias=att_rel_bias,
             gmlp_w_in=gmlp_w_in, gmlp_ln_g=gmlp_ln_g, gmlp_w_s=gmlp_w_s, gmlp_b_s=gmlp_b_s,
             gmlp_w_out=gmlp_w_out, ffn_w_gu=ffn_w_gu, ffn_w_down=ffn_w_down,
             moe_w_router=moe_w_router, moe_w_gu=moe_w_gu, moe_w_down=moe_w_down,
             ple_w_in=ple_w_in, ple_w_gate=ple_w_gate)
    pos_p = jnp.arange(x_prompt.shape[1], dtype=jnp.int32)
    pos_s = PAST_LEN + jnp.arange(x_sample.shape[1], dtype=jnp.int32)
    y_prompt, ret_state_p, k_rows_p, v_rows_p, gmlp_v_p = _trunk(
        x_prompt, p_prompt, pos_p, None, None, None, w, True)
    y_sample, ret_state_s, k_rows_s, v_rows_s, gmlp_v_s = _trunk(
        x_sample, p_sample, pos_s, state_ret, cache_k, cache_v, w, False)
    return (y_prompt, y_sample, ret_state_p, ret_state_s, k_rows_p, v_rows_p, k_rows_s, v_rows_s,
            gmlp_v_p, gmlp_v_s)
```

```python
import functools

import jax
import jax.numpy as jnp
from jax import lax
from jax.experimental import pallas as pl
from jax.experimental.pallas import tpu as pltpu

F32 = jnp.float32
BF16 = jnp.bfloat16

EPS = 1e-6
NEG_INF = -1e30
CHUNK = 64
ROPE_BASE = 10000.0
RET_HEADS, RET_DK, RET_DV = 8, 256, 512
ATT_HEADS, ATT_DH = 16, 128
BAND_PAST_CHUNKS = 8
REL_CLIP = 128
GMLP_CHUNK, GMLP_GROUPS = 128, 8
N_EXPERTS, TOP_K = 8, 2
PAST_LEN = 2048

LANES = 128
V7X_VMEM_LIMIT_BYTES = 56 * 1024 * 1024

RET_CHUNK = 256
ATT_QBLOCK = 2 * CHUNK
MOE_TM = 512


def _params(*sem):
    return pltpu.CompilerParams(dimension_semantics=sem, vmem_limit_bytes=V7X_VMEM_LIMIT_BYTES)


def _silu(x):
    return x * jax.nn.sigmoid(x)


def _rmsnorm_kernel(x_ref, g_ref, *o_refs):
    x = x_ref[...]
    y = x * lax.rsqrt(jnp.mean(x * x, axis=-1, keepdims=True) + EPS) * g_ref[...]
    for o_ref in o_refs:
        o_ref[...] = y.astype(o_ref.dtype)


def rmsnorm(x, g, dtypes, tm):
    T, D = x.shape
    outs = pl.pallas_call(
        _rmsnorm_kernel,
        grid=(T // tm,),
        in_specs=[pl.BlockSpec((tm, D), lambda i: (i, 0)), pl.BlockSpec((1, D), lambda i: (0, 0))],
        out_specs=[pl.BlockSpec((tm, D), lambda i: (i, 0)) for _ in dtypes],
        out_shape=[jax.ShapeDtypeStruct((T, D), dt) for dt in dtypes],
        compiler_params=_params("arbitrary"),
        name="rmsnorm",
    )(x, g.reshape(1, D))
    return outs


def _mm_kernel(a_ref, w_ref, *rest, has_res, cast_w):
    if has_res:
        r_ref, o_ref, *scratch = rest
    else:
        o_ref, *scratch = rest
    if cast_w:
        w_bf = scratch[0]

        @pl.when(pl.program_id(1) == 0)
        def _():
            w_bf[...] = w_ref[...].astype(BF16)

        w = w_bf[...]
    else:
        w = w_ref[...]
    acc = jnp.dot(a_ref[...], w, preferred_element_type=F32)
    if has_res:
        acc = r_ref[...] + acc
    o_ref[...] = acc.astype(o_ref.dtype)


def matmul(a, w, *, tm, tn, out_dtype=F32, res=None):
    M, K = a.shape
    N = w.shape[1]
    cast_w = w.dtype != BF16
    in_specs = [pl.BlockSpec((tm, K), lambda n, m: (m, 0)), pl.BlockSpec((K, tn), lambda n, m: (0, n))]
    args = [a, w]
    if res is not None:
        in_specs.append(pl.BlockSpec((tm, tn), lambda n, m: (m, n)))
        args.append(res)
    return pl.pallas_call(
        functools.partial(_mm_kernel, has_res=res is not None, cast_w=cast_w),
        grid=(N // tn, M // tm),
        in_specs=in_specs,
        out_specs=pl.BlockSpec((tm, tn), lambda n, m: (m, n)),
        out_shape=jax.ShapeDtypeStruct((M, N), out_dtype),
        scratch_shapes=[pltpu.VMEM((K, tn), BF16)] if cast_w else [],
        compiler_params=_params("arbitrary", "arbitrary"),
        name="matmul",
    )(*args)


def _swiglu_up_kernel(x_ref, wa_ref, wb_ref, o_ref):
    x = x_ref[...]
    a = jnp.dot(x, wa_ref[...], preferred_element_type=F32)
    b = jnp.dot(x, wb_ref[...], preferred_element_type=F32)
    o_ref[...] = (_silu(a) * b).astype(o_ref.dtype)


def swiglu_up(x, w_gu, *, tm, tn):
    T, D = x.shape
    F = w_gu.shape[1] // 2
    nb = F // tn
    return pl.pallas_call(
        _swiglu_up_kernel,
        grid=(nb, T // tm),
        in_specs=[pl.BlockSpec((tm, D), lambda n, m: (m, 0)),
                  pl.BlockSpec((D, tn), lambda n, m: (0, n)),
                  pl.BlockSpec((D, tn), lambda n, m: (0, n + nb))],
        out_specs=pl.BlockSpec((tm, tn), lambda n, m: (m, n)),
        out_shape=jax.ShapeDtypeStruct((T, F), BF16),
        compiler_params=_params("arbitrary", "arbitrary"),
        name="swiglu_up",
    )(x, w_gu, w_gu)


def dense_ffn(x, h, w_gu, w_down, *, tm_up, tm_down):
    tn = 512
    F = w_gu.shape[1] // 2
    FP = -(-F // tn) * tn
    w_gu_p = jnp.concatenate([_pad_cols(w_gu[:, :F], FP), _pad_cols(w_gu[:, F:], FP)], axis=1).astype(BF16)
    w_dn_p = jnp.pad(w_down, ((0, FP - F), (0, 0))).astype(BF16)
    hm = swiglu_up(h, w_gu_p, tm=tm_up, tn=tn)
    return matmul(hm, w_dn_p, tm=tm_down, tn=tn, res=x)


def _pad_cols(w, n):
    return jnp.pad(w, ((0, 0), (0, n - w.shape[1])))


def _ple_kernel(hn_ref, wg_ref, p_ref, wi_ref, x_ref, o_ref, wg_bf, wi_bf):
    @pl.when(pl.program_id(1) == 0)
    def _():
        wg_bf[...] = wg_ref[...].astype(BF16)
        wi_bf[...] = wi_ref[...].astype(BF16)

    gate = jax.nn.sigmoid(jnp.dot(hn_ref[...], wg_bf[...], preferred_element_type=F32))
    emb = jnp.dot(p_ref[...].astype(BF16), wi_bf[...], preferred_element_type=F32)
    o_ref[...] = x_ref[...] + gate * emb


def ple(x, hn, p, w_gate, w_in, *, tm, tn):
    T, D = x.shape
    DP = p.shape[1]
    return pl.pallas_call(
        _ple_kernel,
        grid=(D // tn, T // tm),
        in_specs=[pl.BlockSpec((tm, D), lambda n, m: (m, 0)),
                  pl.BlockSpec((D, tn), lambda n, m: (0, n)),
                  pl.BlockSpec((tm, DP), lambda n, m: (m, 0)),
                  pl.BlockSpec((DP, tn), lambda n, m: (0, n)),
                  pl.BlockSpec((tm, tn), lambda n, m: (m, n))],
        out_specs=pl.BlockSpec((tm, tn), lambda n, m: (m, n)),
        out_shape=jax.ShapeDtypeStruct((T, D), F32),
        scratch_shapes=[pltpu.VMEM((D, tn), BF16), pltpu.VMEM((DP, tn), BF16)],
        compiler_params=_params("arbitrary", "arbitrary"),
        name="ple",
    )(hn, w_gate, p, w_in, x)


def _retention_kernel(lg_ref, q_ref, k_ref, v_ref, g_ref, cos_ref, sin_ref, *rest, has_state0):
    if has_state0:
        s0_ref, y_ref, sout_ref, state = rest
    else:
        y_ref, sout_ref, state = rest
    h = pl.program_id(1)
    c = pl.program_id(2)
    C = q_ref.shape[0]
    half = RET_DK // 2
    lg = lg_ref[h]

    @pl.when(c == 0)
    def _():
        if has_state0:
            state[...] = s0_ref[...]
        else:
            state[...] = jnp.zeros_like(state)

    cos = cos_ref[...]
    sin = sin_ref[...]

    def rotary(x):
        x1, x2 = x[:, :half], x[:, half:]
        return jnp.concatenate([x1 * cos - x2 * sin, x1 * sin + x2 * cos], axis=-1)

    q = rotary(q_ref[...])
    k = rotary(k_ref[...]) * RET_DK ** -0.5
    vb = v_ref[...].astype(BF16)

    row = lax.broadcasted_iota(jnp.int32, (C, C), 0)
    col = lax.broadcasted_iota(jnp.int32, (C, C), 1)
    rel = (row - col).astype(F32)
    decay = jnp.where(rel >= 0, jnp.exp(jnp.maximum(rel, 0.0) * lg), 0.0)
    idx = lax.broadcasted_iota(jnp.int32, (C, 1), 0).astype(F32)
    q_decay = jnp.exp((idx + 1.0) * lg)
    k_decay = jnp.exp((C - 1.0 - idx) * lg)
    chunk_decay = jnp.exp(jnp.zeros((1, 1), F32) + C * lg)

    qb = q.astype(BF16)
    scores = lax.dot_general(qb, k.astype(BF16), (((1,), (1,)), ((), ())), preferred_element_type=F32) * decay
    o = jnp.dot(scores.astype(BF16), vb, preferred_element_type=F32)
    st = state[...]
    o = o + jnp.dot(qb, st.astype(BF16), preferred_element_type=F32) * q_decay
    kv = lax.dot_general((k * k_decay).astype(BF16), vb, (((0,), (0,)), ((), ())), preferred_element_type=F32)
    st = st * chunk_decay + kv
    state[...] = st

    @pl.when(c == pl.num_programs(2) - 1)
    def _():
        sout_ref[...] = st

    o = o * lax.rsqrt(jnp.mean(o * o, axis=-1, keepdims=True) + EPS)
    y_ref[...] = (o * _silu(g_ref[...])).astype(y_ref.dtype)


def retention(qkvg, cos, sin, log_g, state0, *, batch, seq, chunk, row0):
    H, dk, dv = RET_HEADS, RET_DK, RET_DV
    nc = seq // chunk
    rb0 = row0 // chunk
    kq = (H * dk) // dk
    kv = (2 * H * dk) // dv
    kg = kv + H

    def rows(b, c):
        return rb0 + b * nc + c

    in_specs = [
        pl.BlockSpec((chunk, dk), lambda b, h, c, lg: (rows(b, c), h)),
        pl.BlockSpec((chunk, dk), lambda b, h, c, lg: (rows(b, c), kq + h)),
        pl.BlockSpec((chunk, dv), lambda b, h, c, lg: (rows(b, c), kv + h)),
        pl.BlockSpec((chunk, dv), lambda b, h, c, lg: (rows(b, c), kg + h)),
        pl.BlockSpec((chunk, dk // 2), lambda b, h, c, lg: (rows(b, c), 0)),
        pl.BlockSpec((chunk, dk // 2), lambda b, h, c, lg: (rows(b, c), 0)),
    ]
    args = [qkvg, qkvg, qkvg, qkvg, cos, sin]
    if state0 is not None:
        in_specs.append(pl.BlockSpec((None, None, dk, dv), lambda b, h, c, lg: (b, h, 0, 0)))
        args.append(state0)
    grid_spec = pltpu.PrefetchScalarGridSpec(
        num_scalar_prefetch=1,
        grid=(batch, H, nc),
        in_specs=in_specs,
        out_specs=[pl.BlockSpec((chunk, dv), lambda b, h, c, lg: (b * nc + c, h)),
                   pl.BlockSpec((None, None, dk, dv), lambda b, h, c, lg: (b, h, 0, 0))],
        scratch_shapes=[pltpu.VMEM((dk, dv), F32)],
    )
    return pl.pallas_call(
        functools.partial(_retention_kernel, has_state0=state0 is not None),
        grid_spec=grid_spec,
        out_shape=[jax.ShapeDtypeStruct((batch * seq, H * dv), BF16),
                   jax.ShapeDtypeStruct((batch, H, dk, dv), F32)],
        compiler_params=_params("arbitrary", "arbitrary", "arbitrary"),
        name="retention",
    )(log_g, *args)


def _att_prompt_kernel(q_ref, k_ref, v_ref, bias_ref, o_ref, kpad, vpad):
    S = q_ref.shape[0]
    QB = ATT_QBLOCK
    past = BAND_PAST_CHUNKS * CHUNK
    win = past + QB
    kpad[:past, :] = jnp.zeros((past, ATT_DH), BF16)
    vpad[:past, :] = jnp.zeros((past, ATT_DH), BF16)
    kpad[past:, :] = k_ref[...].astype(BF16)
    vpad[past:, :] = v_ref[...].astype(BF16)
    bias = bias_ref[...]
    qi = lax.broadcasted_iota(jnp.int32, (QB, win), 0)
    kj = lax.broadcasted_iota(jnp.int32, (QB, win), 1)
    chunk_shift = CHUNK.bit_length() - 1
    qc = jnp.right_shift(qi, chunk_shift)
    kc = jnp.right_shift(kj, chunk_shift)
    band = (kc >= qc) & (kc <= qc + BAND_PAST_CHUNKS)

    def body(i, carry):
        q0 = pl.multiple_of(i * QB, QB)
        qb = q_ref[pl.ds(q0, QB), :].astype(BF16)
        kw = kpad[pl.ds(q0, win), :]
        vw = vpad[pl.ds(q0, win), :]
        s = lax.dot_general(qb, kw, (((1,), (1,)), ((), ())), preferred_element_type=F32)
        s = s * ATT_DH ** -0.5 + bias
        valid = band & (kj >= past - q0)
        s = jnp.where(valid, s, NEG_INF)
        m = jnp.max(s, axis=-1, keepdims=True)
        e = jnp.exp(s - m)
        pr = (e / jnp.sum(e, axis=-1, keepdims=True)).astype(BF16)
        o_ref[pl.ds(q0, QB), :] = jnp.dot(pr, vw, preferred_element_type=F32).astype(o_ref.dtype)
        return carry

    lax.fori_loop(0, S // QB, body, 0)


def attention_prompt(qkv, bias, *, seq):
    H, dh = ATT_HEADS, ATT_DH
    past = BAND_PAST_CHUNKS * CHUNK
    return pl.pallas_call(
        _att_prompt_kernel,
        grid=(H,),
        in_specs=[pl.BlockSpec((seq, dh), lambda h: (0, h)),
                  pl.BlockSpec((seq, dh), lambda h: (0, H + h)),
                  pl.BlockSpec((seq, dh), lambda h: (0, 2 * H + h)),
                  pl.BlockSpec((None, ATT_QBLOCK, past + ATT_QBLOCK), lambda h: (h, 0, 0))],
        out_specs=pl.BlockSpec((seq, dh), lambda h: (0, h)),
        out_shape=jax.ShapeDtypeStruct((seq, H * dh), BF16),
        scratch_shapes=[pltpu.VMEM((past + seq, dh), BF16), pltpu.VMEM((past + seq, dh), BF16)],
        compiler_params=_params("arbitrary"),
        name="attention_prompt",
    )(qkv, qkv, qkv, bias)


def _att_sample_kernel(q_ref, kn_ref, vn_ref, ck_ref, cv_ref, bp_ref, bn_ref, o_ref):
    dh = ATT_DH
    for h in range(ATT_HEADS):
        sl = slice(h * dh, (h + 1) * dh)
        qb = q_ref[:, sl].astype(BF16)
        kn = kn_ref[:, sl].astype(BF16)
        vn = vn_ref[:, sl].astype(BF16)
        kp = ck_ref[:, sl].astype(BF16)
        vp = cv_ref[:, sl].astype(BF16)
        sp = lax.dot_general(qb, kp, (((1,), (1,)), ((), ())), preferred_element_type=F32)
        sn = lax.dot_general(qb, kn, (((1,), (1,)), ((), ())), preferred_element_type=F32)
        sp = sp * dh ** -0.5 + bp_ref[h]
        sn = sn * dh ** -0.5 + bn_ref[h]
        m = jnp.maximum(jnp.max(sp, axis=-1, keepdims=True), jnp.max(sn, axis=-1, keepdims=True))
        ep = jnp.exp(sp - m)
        en = jnp.exp(sn - m)
        den = jnp.sum(ep, axis=-1, keepdims=True) + jnp.sum(en, axis=-1, keepdims=True)
        o = jnp.dot((ep / den).astype(BF16), vp, preferred_element_type=F32)
        o = o + jnp.dot((en / den).astype(BF16), vn, preferred_element_type=F32)
        o_ref[:, sl] = o.astype(o_ref.dtype)


def attention_sample(qkv, cache_k, cache_v, bias_past, bias_new, *, batch, seq, row0):
    H, dh = ATT_HEADS, ATT_DH
    W = cache_k.shape[1]
    rb0 = row0 // seq
    return pl.pallas_call(
        _att_sample_kernel,
        grid=(batch,),
        in_specs=[pl.BlockSpec((seq, H * dh), lambda b: (rb0 + b, 0)),
                  pl.BlockSpec((seq, H * dh), lambda b: (rb0 + b, 1)),
                  pl.BlockSpec((seq, H * dh), lambda b: (rb0 + b, 2)),
                  pl.BlockSpec((None, W, H * dh), lambda b: (b, 0, 0)),
                  pl.BlockSpec((None, W, H * dh), lambda b: (b, 0, 0)),
                  pl.BlockSpec((H, seq, W), lambda b: (0, 0, 0)),
                  pl.BlockSpec((H, seq, seq), lambda b: (0, 0, 0))],
        out_specs=pl.BlockSpec((seq, H * dh), lambda b: (b, 0)),
        out_shape=jax.ShapeDtypeStruct((batch * seq, H * dh), BF16),
        compiler_params=_params("arbitrary"),
        name="attention_sample",
    )(qkv, qkv, qkv, cache_k, cache_v, bias_past, bias_new)


def _rel_bias(table, n_q, n_k, offset):
    rel = offset + jnp.arange(n_q)[:, None] - jnp.arange(n_k)[None, :]
    b = jnp.take(table, jnp.clip(rel, -REL_CLIP, REL_CLIP) + REL_CLIP, axis=0)
    return b.transpose(2, 0, 1).astype(F32)


def _gmlp_kernel(uv_ref, lng_ref, ws_ref, bs_ref, y_ref, vn_ref):
    DV = y_ref.shape[1]
    cg = DV // GMLP_GROUPS
    v = jax.nn.gelu(uv_ref[:, DV:])
    mu = jnp.mean(v, axis=-1, keepdims=True)
    d = v - mu
    var = jnp.mean(d * d, axis=-1, keepdims=True)
    vn = d * lax.rsqrt(var + EPS) * lng_ref[...]
    vn_ref[...] = vn
    for g in range(GMLP_GROUPS):
        sl = slice(g * cg, (g + 1) * cg)
        mixed = jnp.dot(ws_ref[g], vn[:, sl].astype(BF16), preferred_element_type=F32) + bs_ref[g]
        y_ref[:, sl] = (jax.nn.gelu(uv_ref[:, sl]) * mixed).astype(y_ref.dtype)


def gmlp_gate(uv, ln_g, ws, bs, *, n_prompt_tiles, tile):
    T = uv.shape[0]
    DV = uv.shape[1] // 2
    G = GMLP_GROUPS
    nt = T // tile
    last = n_prompt_tiles - 1
    return pl.pallas_call(
        _gmlp_kernel,
        grid=(nt,),
        in_specs=[pl.BlockSpec((tile, 2 * DV), lambda i: (i, 0)),
                  pl.BlockSpec((1, DV), lambda i: (0, 0)),
                  pl.BlockSpec((None, G, tile, tile), lambda i: (i // n_prompt_tiles, 0, 0, 0)),
                  pl.BlockSpec((None, G, tile, 1), lambda i: (i // n_prompt_tiles, 0, 0, 0))],
        out_specs=[pl.BlockSpec((tile, DV), lambda i: (i, 0)),
                   pl.BlockSpec((tile, DV), lambda i: (jnp.maximum(i - last, 0), 0))],
        out_shape=[jax.ShapeDtypeStruct((T, DV), BF16),
                   jax.ShapeDtypeStruct(((nt - last) * tile, DV), F32)],
        compiler_params=_params("arbitrary"),
        name="gmlp_gate",
    )(uv, ln_g.reshape(1, DV), ws, bs)


def _router_kernel(h_ref, w_ref, idx_ref, gate_ref):
    logits = jnp.dot(h_ref[...].astype(BF16), w_ref[...].astype(BF16), preferred_element_type=F32)
    lane = lax.broadcasted_iota(jnp.int32, logits.shape, 1)
    lane_f = lane.astype(F32)
    l1 = jnp.where(lane < N_EXPERTS, logits, -jnp.inf)
    v1 = jnp.max(l1, axis=-1, keepdims=True)
    i1 = jnp.min(jnp.where(l1 == v1, lane_f, float(LANES)), axis=-1, keepdims=True)
    l2 = jnp.where(lane_f == i1, -jnp.inf, l1)
    v2 = jnp.max(l2, axis=-1, keepdims=True)
    i2 = jnp.min(jnp.where(l2 == v2, lane_f, float(LANES)), axis=-1, keepdims=True)
    e2 = jnp.exp(v2 - v1)
    den = 1.0 + e2
    idx_ref[...] = jnp.where(lane == 0, i1, jnp.where(lane == 1, i2, 0.0)).astype(jnp.int32)
    gate_ref[...] = jnp.where(lane == 0, 1.0 / den, jnp.where(lane == 1, e2 / den, 0.0))


def router(h, w_router_padded, *, tm):
    T, D = h.shape
    return pl.pallas_call(
        _router_kernel,
        grid=(T // tm,),
        in_specs=[pl.BlockSpec((tm, D), lambda i: (i, 0)), pl.BlockSpec((D, LANES), lambda i: (0, 0))],
        out_specs=[pl.BlockSpec((tm, LANES), lambda i: (i, 0)), pl.BlockSpec((tm, LANES), lambda i: (i, 0))],
        out_shape=[jax.ShapeDtypeStruct((T, LANES), jnp.int32), jax.ShapeDtypeStruct((T, LANES), F32)],
        compiler_params=_params("arbitrary"),
        name="router",
    )(h, w_router_padded)


def _row_copy(src_hbm, dst_ref, src_row, dst_row, sem):
    return pltpu.make_async_copy(src_hbm.at[pl.ds(src_row, 1), :], dst_ref.at[pl.ds(dst_row, 1), :], sem)


def _gather_kernel(tok_ref, h_hbm, o_ref, buf, sem):
    tm = o_ref.shape[0]
    base = pl.program_id(0) * tm

    def start(r, carry):
        _row_copy(h_hbm, buf, tok_ref[base + r], r, sem).start()
        return carry

    def wait(r, carry):
        _row_copy(h_hbm, buf, 0, r, sem).wait()
        return carry

    lax.fori_loop(0, tm, start, 0)
    lax.fori_loop(0, tm, wait, 0)
    o_ref[...] = buf[...].astype(o_ref.dtype)


def gather_rows(h, tok_sorted, *, tm):
    D = h.shape[1]
    P = tok_sorted.shape[0]
    grid_spec = pltpu.PrefetchScalarGridSpec(
        num_scalar_prefetch=1,
        grid=(P // tm,),
        in_specs=[pl.BlockSpec(memory_space=pl.ANY)],
        out_specs=pl.BlockSpec((tm, D), lambda i, tok: (i, 0)),
        scratch_shapes=[pltpu.VMEM((tm, D), F32), pltpu.SemaphoreType.DMA(())],
    )
    return pl.pallas_call(
        _gather_kernel,
        grid_spec=grid_spec,
        out_shape=jax.ShapeDtypeStruct((P, D), BF16),
        compiler_params=_params("arbitrary"),
        name="moe_gather",
    )(tok_sorted, h)


def _moe_up_kernel(te_ref, first_ref, nused_ref, x_ref, wa_ref, wb_ref, o_ref, wa_bf, wb_bf):
    m = pl.program_id(1)

    @pl.when(first_ref[m] == 1)
    def _():
        wa_bf[...] = wa_ref[...].astype(BF16)
        wb_bf[...] = wb_ref[...].astype(BF16)

    @pl.when(m < nused_ref[0])
    def _():
        x = x_ref[...]
        a = jnp.dot(x, wa_bf[...], preferred_element_type=F32)
        b = jnp.dot(x, wb_bf[...], preferred_element_type=F32)
        o_ref[...] = (_silu(a) * b).astype(o_ref.dtype)

    @pl.when(m >= nused_ref[0])
    def _():
        o_ref[...] = jnp.zeros_like(o_ref)


def moe_up(xs, w_gu, tile_expert, tile_first, n_used, *, tm, tn):
    P, D = xs.shape
    F = w_gu.shape[2] // 2
    nb = F // tn
    grid_spec = pltpu.PrefetchScalarGridSpec(
        num_scalar_prefetch=3,
        grid=(nb, P // tm),
        in_specs=[pl.BlockSpec((tm, D), lambda n, m, te, fi, nu: (m, 0)),
                  pl.BlockSpec((None, D, tn), lambda n, m, te, fi, nu: (te[m], 0, n)),
                  pl.BlockSpec((None, D, tn), lambda n, m, te, fi, nu: (te[m], 0, n + nb))],
        out_specs=pl.BlockSpec((tm, tn), lambda n, m, te, fi, nu: (m, n)),
        scratch_shapes=[pltpu.VMEM((D, tn), BF16), pltpu.VMEM((D, tn), BF16)],
    )
    return pl.pallas_call(
        _moe_up_kernel,
        grid_spec=grid_spec,
        out_shape=jax.ShapeDtypeStruct((P, F), BF16),
        compiler_params=_params("arbitrary", "arbitrary"),
        name="moe_up",
    )(tile_expert, tile_first, n_used, xs, w_gu, w_gu)


def _moe_down_kernel(te_ref, first_ref, nused_ref, x_ref, w_ref, o_ref, w_bf):
    m = pl.program_id(1)

    @pl.when(first_ref[m] == 1)
    def _():
        w_bf[...] = w_ref[...].astype(BF16)

    @pl.when(m < nused_ref[0])
    def _():
        o_ref[...] = jnp.dot(x_ref[...], w_bf[...], preferred_element_type=F32)

    @pl.when(m >= nused_ref[0])
    def _():
        o_ref[...] = jnp.zeros_like(o_ref)


def moe_down(hm, w_down, tile_expert, tile_first, n_used, *, tm, tn):
    P, F = hm.shape
    D = w_down.shape[2]
    grid_spec = pltpu.PrefetchScalarGridSpec(
        num_scalar_prefetch=3,
        grid=(D // tn, P // tm),
        in_specs=[pl.BlockSpec((tm, F), lambda n, m, te, fi, nu: (m, 0)),
                  pl.BlockSpec((None, F, tn), lambda n, m, te, fi, nu: (te[m], 0, n))],
        out_specs=pl.BlockSpec((tm, tn), lambda n, m, te, fi, nu: (m, n)),
        scratch_shapes=[pltpu.VMEM((F, tn), BF16)],
    )
    return pl.pallas_call(
        _moe_down_kernel,
        grid_spec=grid_spec,
        out_shape=jax.ShapeDtypeStruct((P, D), F32),
        compiler_params=_params("arbitrary", "arbitrary"),
        name="moe_down",
    )(tile_expert, tile_first, n_used, hm, w_down)


def _combine_kernel(p0_ref, p1_ref, x_ref, gate_ref, y_hbm, o_ref, buf0, buf1, sems):
    tm = o_ref.shape[0]
    base = pl.program_id(0) * tm

    def start(r, carry):
        _row_copy(y_hbm, buf0, p0_ref[base + r], r, sems.at[0]).start()
        _row_copy(y_hbm, buf1, p1_ref[base + r], r, sems.at[1]).start()
        return carry

    def wait(r, carry):
        _row_copy(y_hbm, buf0, 0, r, sems.at[0]).wait()
        _row_copy(y_hbm, buf1, 0, r, sems.at[1]).wait()
        return carry

    lax.fori_loop(0, tm, start, 0)
    lax.fori_loop(0, tm, wait, 0)
    gate = gate_ref[...]
    o_ref[...] = x_ref[...] + (gate[:, 0:1] * buf0[...] + gate[:, 1:2] * buf1[...])


def moe_combine(x, gates, ys, pos0, pos1, *, tm):
    T, D = x.shape
    grid_spec = pltpu.PrefetchScalarGridSpec(
        num_scalar_prefetch=2,
        grid=(T // tm,),
        in_specs=[pl.BlockSpec((tm, D), lambda i, a, b: (i, 0)),
                  pl.BlockSpec((tm, LANES), lambda i, a, b: (i, 0)),
                  pl.BlockSpec(memory_space=pl.ANY)],
        out_specs=pl.BlockSpec((tm, D), lambda i, a, b: (i, 0)),
        scratch_shapes=[pltpu.VMEM((tm, D), F32), pltpu.VMEM((tm, D), F32), pltpu.SemaphoreType.DMA((2,))],
    )
    return pl.pallas_call(
        _combine_kernel,
        grid_spec=grid_spec,
        out_shape=jax.ShapeDtypeStruct((T, D), F32),
        compiler_params=_params("arbitrary"),
        name="moe_combine",
    )(pos0, pos1, x, gates, ys)


def _route_plan(idx, tm):
    T = idx.shape[0]
    E = N_EXPERTS
    n_tiles = (TOP_K * T) // tm + E
    flat_e = idx.reshape(-1)
    onehot = (flat_e[:, None] == jnp.arange(E, dtype=jnp.int32)[None, :]).astype(jnp.int32)
    csum = jnp.cumsum(onehot, axis=0)
    rank = jnp.sum((csum - onehot) * onehot, axis=1)
    counts = csum[-1]
    tiles_e = (counts + tm - 1) // tm
    tile_end = jnp.cumsum(tiles_e)
    tile_start = tile_end - tiles_e
    pos = (tile_start[flat_e] * tm + rank).astype(jnp.int32)
    tok_sorted = jnp.zeros((n_tiles * tm,), jnp.int32).at[pos].set(jnp.arange(TOP_K * T, dtype=jnp.int32) // TOP_K)
    n_used = tile_end[-1]
    m_ids = jnp.arange(n_tiles, dtype=jnp.int32)
    te = jnp.sum((jnp.minimum(m_ids, n_used - 1)[:, None] >= tile_end[None, :]).astype(jnp.int32), axis=1)
    first = ((m_ids == 0) | (te != jnp.roll(te, 1))) & (m_ids < n_used)
    return (tok_sorted, te.astype(jnp.int32), first.astype(jnp.int32), n_used.reshape(1).astype(jnp.int32),
            pos[0::2], pos[1::2])


def moe(x, h, w_router, w_gu, w_down, *, tm_tok):
    D = h.shape[1]
    w_r = jnp.pad(w_router, ((0, 0), (0, LANES - N_EXPERTS)))
    idx, gates = router(h, w_r, tm=tm_tok)
    tok_sorted, te, first, n_used, pos0, pos1 = _route_plan(idx[:, :TOP_K], MOE_TM)
    xs = gather_rows(h, tok_sorted, tm=MOE_TM)
    hm = moe_up(xs, w_gu, te, first, n_used, tm=MOE_TM, tn=512)
    ys = moe_down(hm, w_down, te, first, n_used, tm=MOE_TM, tn=256)
    return moe_combine(x, gates, ys, pos0, pos1, tm=tm_tok)


def kernel(x_prompt, x_sample, state_ret, cache_k, cache_v, p_prompt, p_sample, g_mix, g_ffn, g_ple, g_final,
           ret_w_in, ret_w_out, att_w_in, att_w_out, att_rel_bias, gmlp_w_in, gmlp_ln_g, gmlp_w_s, gmlp_b_s,
           gmlp_w_out, ffn_w_gu, ffn_w_down, moe_w_router, moe_w_gu, moe_w_down, ple_w_in, ple_w_gate):
    B, S, D = x_prompt.shape
    BS, L, _ = x_sample.shape
    depth = g_mix.shape[0]
    NP, NS = B * S, BS * L
    T = NP + NS
    assert B == 1 and S % RET_CHUNK == 0 and S % ATT_QBLOCK == 0 and L <= CHUNK
    TM = T // 8
    TMS = T // 16

    x = jnp.concatenate([x_prompt.reshape(NP, D), x_sample.reshape(NS, D)], axis=0)
    p_all = jnp.concatenate([p_prompt.reshape(depth, NP, -1), p_sample.reshape(depth, NS, -1)], axis=1)

    pos = jnp.concatenate([jnp.arange(S, dtype=jnp.int32), jnp.tile(PAST_LEN + jnp.arange(L, dtype=jnp.int32), BS)])
    freqs = ROPE_BASE ** (-jnp.arange(0, RET_DK, 2, dtype=F32) / RET_DK)
    ang = pos.astype(F32)[:, None] * freqs[None, :]
    cos, sin = jnp.cos(ang), jnp.sin(ang)
    log_g = jnp.log1p(-jnp.exp2(-5.0 - jnp.arange(RET_HEADS, dtype=F32)))

    ret_p, ret_s, k_p, v_p, k_s, v_s, gv = [], [], [], [], [], [], []
    for i in range(depth):
        kind, slot = i % 3, i // 3
        (h,) = rmsnorm(x, g_mix[i], [BF16], TMS)
        if kind == 0:
            qkvg = matmul(h, ret_w_in[slot], tm=TM, tn=1024)
            y_p, st_p = retention(qkvg, cos, sin, log_g, None, batch=B, seq=S, chunk=RET_CHUNK, row0=0)
            y_s, st_s = retention(qkvg, cos, sin, log_g, state_ret[slot], batch=BS, seq=L, chunk=L, row0=NP)
            ret_p.append(st_p)
            ret_s.append(st_s)
            y = jnp.concatenate([y_p, y_s], axis=0)
            x = matmul(y, ret_w_out[slot].astype(BF16), tm=TMS, tn=1024, res=x)
        elif kind == 1:
            HD = ATT_HEADS * ATT_DH
            W = cache_k.shape[2]
            past = BAND_PAST_CHUNKS * CHUNK
            qkv = matmul(h, att_w_in[slot], tm=TM, tn=1024)
            bias_p = _rel_bias(att_rel_bias[slot], ATT_QBLOCK, past + ATT_QBLOCK, past)
            bias_s = _rel_bias(att_rel_bias[slot], L, W + L, W)
            o_p = attention_prompt(qkv, bias_p, seq=S)
            o_s = attention_sample(qkv, cache_k[slot].reshape(BS, W, HD), cache_v[slot].reshape(BS, W, HD),
                                   bias_s[:, :, :W], bias_s[:, :, W:], batch=BS, seq=L, row0=NP)
            w = min(past, S)
            k_p.append(qkv[NP - w:NP, HD:2 * HD].reshape(B, w, ATT_HEADS, ATT_DH))
            v_p.append(qkv[NP - w:NP, 2 * HD:].reshape(B, w, ATT_HEADS, ATT_DH))
            k_s.append(qkv[NP:, HD:2 * HD].reshape(BS, L, ATT_HEADS, ATT_DH))
            v_s.append(qkv[NP:, 2 * HD:].reshape(BS, L, ATT_HEADS, ATT_DH))
            o = jnp.concatenate([o_p, o_s], axis=0)
            x = matmul(o, att_w_out[slot], tm=TM, tn=1024, res=x)
        else:
            tile = GMLP_CHUNK
            G = GMLP_GROUPS
            uv = matmul(h, gmlp_w_in[slot], tm=TM, tn=1024)
            tril = jnp.tril(jnp.ones((tile, tile), dtype=bool))
            ws_p = jnp.where(tril[None], gmlp_w_s[slot][:, :tile, :tile], 0.0)
            r = jnp.arange(tile)
            same = (r[:, None] // L == r[None, :] // L) & (r[:, None] >= r[None, :])
            ws_s = jnp.where(same[None], gmlp_w_s[slot][:, r[:, None] % L, r[None, :] % L], 0.0)
            bs_p = gmlp_b_s[slot][:, :tile]
            bs_s = gmlp_b_s[slot][:, r % L]
            ws = jnp.stack([ws_p, ws_s]).astype(BF16)
            bs = jnp.stack([bs_p, bs_s])[..., None]
            yg, vn = gmlp_gate(uv, gmlp_ln_g[slot], ws, bs, n_prompt_tiles=NP // tile, tile=tile)
            gv.append((vn[:tile].reshape(B, tile, -1), vn[tile:].reshape(BS, L, -1)))
            x = matmul(yg, gmlp_w_out[slot].astype(BF16), tm=TMS, tn=512, res=x)

        if i % 2 == 0:
            (h,) = rmsnorm(x, g_ffn[i], [BF16], TMS)
            x = dense_ffn(x, h, ffn_w_gu[i // 2], ffn_w_down[i // 2], tm_up=TM, tm_down=TMS)
        else:
            (h,) = rmsnorm(x, g_ffn[i], [F32], TMS)
            x = moe(x, h, moe_w_router[i // 2], moe_w_gu[i // 2], moe_w_down[i // 2], tm_tok=TMS)

        (hn,) = rmsnorm(x, g_ple[i], [BF16], TMS)
        x = ple(x, hn, p_all[i], ple_w_gate[i], ple_w_in[i], tm=TMS, tn=1024)

    (y,) = rmsnorm(x, g_final, [F32], TMS)
    y_prompt = y[:NP].reshape(B, S, D)
    y_sample = y[NP:].reshape(BS, L, D)
    return (y_prompt, y_sample, jnp.stack(ret_p), jnp.stack(ret_s), jnp.stack(k_p), jnp.stack(v_p),
            jnp.stack(k_s), jnp.stack(v_s), jnp.stack([a for a, _ in gv]), jnp.stack([b for _, b in gv]))
```

```python
import functools

import jax
import jax.numpy as jnp
from jax import lax
from jax.experimental import pallas as pl
from jax.experimental.pallas import tpu as pltpu

F32 = jnp.float32
BF16 = jnp.bfloat16

EPS = 1e-6
NEG_INF = -1e30
CHUNK = 64
ROPE_BASE = 10000.0
RET_HEADS, RET_DK, RET_DV = 8, 256, 512
ATT_HEADS, ATT_DH = 16, 128
BAND_PAST_CHUNKS = 8
REL_CLIP = 128
GMLP_CHUNK, GMLP_GROUPS = 128, 8
N_EXPERTS, TOP_K = 8, 2
PAST_LEN = 2048

LANES = 128
V7X_VMEM_LIMIT_BYTES = 60 * 1000 * 1024

RET_CHUNK = 256
ATT_QBLOCK = 2 * CHUNK
MOE_TM = 512


def _params(*sem):
    return pltpu.CompilerParams(dimension_semantics=sem, vmem_limit_bytes=V7X_VMEM_LIMIT_BYTES)


def _silu(x):
    return x * jax.nn.sigmoid(x)


def _slab_chunk(c, n_tokens, nc):
    return pl.ds(c, n_tokens, stride=nc)


def _rmsnorm_kernel(x_ref, g_ref, *o_refs):
    x = x_ref[...]
    tm, D = x.shape
    y = x * lax.rsqrt(jnp.mean(x * x, axis=-1, keepdims=True) + EPS) * g_ref[...]
    for o_ref in o_refs:
        if o_ref.shape == x.shape:
            o_ref[...] = y.astype(o_ref.dtype)
        else:
            nc = D // LANES
            for c in range(nc):
                o_ref[_slab_chunk(c, tm, nc), :] = y[:, c * LANES:(c + 1) * LANES]


def rmsnorm(x, g, dtypes, tm, slab_f32=False):
    T, D = x.shape
    out_specs = [pl.BlockSpec((tm, D), lambda i: (i, 0)) for _ in dtypes]
    out_shape = [jax.ShapeDtypeStruct((T, D), dt) for dt in dtypes]
    if slab_f32:
        out_specs.append(pl.BlockSpec((tm * D // LANES, LANES), lambda i: (i, 0)))
        out_shape.append(jax.ShapeDtypeStruct((T * D // LANES, LANES), F32))
    return pl.pallas_call(
        _rmsnorm_kernel,
        grid=(T // tm,),
        in_specs=[pl.BlockSpec((tm, D), lambda i: (i, 0)), pl.BlockSpec((1, D), lambda i: (0, 0))],
        out_specs=out_specs,
        out_shape=out_shape,
        compiler_params=_params("arbitrary"),
        name="rmsnorm",
    )(x, g.reshape(1, D))


def _mm_kernel(a_ref, w_ref, *rest, has_res, cast_w):
    if has_res:
        r_ref, o_ref, *scratch = rest
    else:
        o_ref, *scratch = rest
    if cast_w:
        w_bf = scratch[0]

        @pl.when(pl.program_id(1) == 0)
        def _():
            w_bf[...] = w_ref[...].astype(BF16)

        w = w_bf[...]
    else:
        w = w_ref[...]
    acc = jnp.dot(a_ref[...], w, preferred_element_type=F32)
    if has_res:
        acc = r_ref[...] + acc
    o_ref[...] = acc.astype(o_ref.dtype)


def _weight_spec(w, layer, tn):
    K = w.shape[-2]
    if w.ndim == 2:
        return pl.BlockSpec((K, tn), lambda n, m: (0, n))
    return pl.BlockSpec((None, K, tn), lambda n, m: (layer, 0, n))


def matmul(a, w, *, tm, tn, layer=0, out_dtype=F32, res=None):
    M, K = a.shape
    N = w.shape[-1]
    cast_w = w.dtype != BF16
    in_specs = [pl.BlockSpec((tm, K), lambda n, m: (m, 0)), _weight_spec(w, layer, tn)]
    args = [a, w]
    if res is not None:
        in_specs.append(pl.BlockSpec((tm, tn), lambda n, m: (m, n)))
        args.append(res)
    return pl.pallas_call(
        functools.partial(_mm_kernel, has_res=res is not None, cast_w=cast_w),
        grid=(N // tn, M // tm),
        in_specs=in_specs,
        out_specs=pl.BlockSpec((tm, tn), lambda n, m: (m, n)),
        out_shape=jax.ShapeDtypeStruct((M, N), out_dtype),
        scratch_shapes=[pltpu.VMEM((K, tn), BF16)] if cast_w else [],
        compiler_params=_params("arbitrary", "arbitrary"),
        name="matmul",
    )(*args)


def _swiglu_up_kernel(x_ref, wa_ref, wb_ref, o_ref):
    x = x_ref[...]
    a = jnp.dot(x, wa_ref[...], preferred_element_type=F32)
    b = jnp.dot(x, wb_ref[...], preferred_element_type=F32)
    o_ref[...] = (_silu(a) * b).astype(o_ref.dtype)


def swiglu_up(x, w_gu, *, tm, tn):
    T, D = x.shape
    F = w_gu.shape[1] // 2
    nb = F // tn
    return pl.pallas_call(
        _swiglu_up_kernel,
        grid=(nb, T // tm),
        in_specs=[pl.BlockSpec((tm, D), lambda n, m: (m, 0)),
                  pl.BlockSpec((D, tn), lambda n, m: (0, n)),
                  pl.BlockSpec((D, tn), lambda n, m: (0, n + nb))],
        out_specs=pl.BlockSpec((tm, tn), lambda n, m: (m, n)),
        out_shape=jax.ShapeDtypeStruct((T, F), BF16),
        compiler_params=_params("arbitrary", "arbitrary"),
        name="swiglu_up",
    )(x, w_gu, w_gu)


def dense_ffn(x, h, w_gu, w_down, *, tm_up, tm_down):
    tn = 512
    F = w_gu.shape[1] // 2
    FP = -(-F // tn) * tn
    w_gu_p = jnp.concatenate([_pad_cols(w_gu[:, :F], FP), _pad_cols(w_gu[:, F:], FP)], axis=1).astype(BF16)
    w_dn_p = jnp.pad(w_down, ((0, FP - F), (0, 0))).astype(BF16)
    hm = swiglu_up(h, w_gu_p, tm=tm_up, tn=tn)
    return matmul(hm, w_dn_p, tm=tm_down, tn=tn, res=x)


def _pad_cols(w, n):
    return jnp.pad(w, ((0, 0), (0, n - w.shape[1])))


def _ple_kernel(hn_ref, wg_ref, p_ref, wi_ref, x_ref, o_ref, wg_bf, wi_bf):
    @pl.when(pl.program_id(1) == 0)
    def _():
        wg_bf[...] = wg_ref[...].astype(BF16)
        wi_bf[...] = wi_ref[...].astype(BF16)

    gate = jax.nn.sigmoid(jnp.dot(hn_ref[...], wg_bf[...], preferred_element_type=F32))
    emb = jnp.dot(p_ref[...].astype(BF16), wi_bf[...], preferred_element_type=F32)
    o_ref[...] = x_ref[...] + gate * emb


def ple(x, hn, p, w_gate, w_in, *, layer, tm, tn):
    T, D = x.shape
    DP = p.shape[2]
    return pl.pallas_call(
        _ple_kernel,
        grid=(D // tn, T // tm),
        in_specs=[pl.BlockSpec((tm, D), lambda n, m: (m, 0)),
                  _weight_spec(w_gate, layer, tn),
                  pl.BlockSpec((None, tm, DP), lambda n, m: (layer, m, 0)),
                  _weight_spec(w_in, layer, tn),
                  pl.BlockSpec((tm, tn), lambda n, m: (m, n))],
        out_specs=pl.BlockSpec((tm, tn), lambda n, m: (m, n)),
        out_shape=jax.ShapeDtypeStruct((T, D), F32),
        scratch_shapes=[pltpu.VMEM((D, tn), BF16), pltpu.VMEM((DP, tn), BF16)],
        compiler_params=_params("arbitrary", "arbitrary"),
        name="ple",
    )(hn, w_gate, p, w_in, x)


def _retention_kernel(lg_ref, q_ref, k_ref, v_ref, g_ref, cos_ref, sin_ref, *rest, has_state0):
    if has_state0:
        s0_ref, y_ref, sout_ref, state = rest
    else:
        y_ref, sout_ref, state = rest
    h = pl.program_id(1)
    c = pl.program_id(2)
    C = q_ref.shape[0]
    half = RET_DK // 2
    lg = lg_ref[h]

    @pl.when(c == 0)
    def _():
        if has_state0:
            state[...] = s0_ref[...]
        else:
            state[...] = jnp.zeros_like(state)

    cos = cos_ref[...]
    sin = sin_ref[...]

    def rotary(x):
        x1, x2 = x[:, :half], x[:, half:]
        return jnp.concatenate([x1 * cos - x2 * sin, x1 * sin + x2 * cos], axis=-1)

    q = rotary(q_ref[...])
    k = rotary(k_ref[...]) * RET_DK ** -0.5
    vb = v_ref[...].astype(BF16)

    row = lax.broadcasted_iota(jnp.int32, (C, C), 0)
    col = lax.broadcasted_iota(jnp.int32, (C, C), 1)
    rel = (row - col).astype(F32)
    decay = jnp.where(rel >= 0, jnp.exp(jnp.maximum(rel, 0.0) * lg), 0.0)
    idx = lax.broadcasted_iota(jnp.int32, (C, 1), 0).astype(F32)
    q_decay = jnp.exp((idx + 1.0) * lg)
    k_decay = jnp.exp((C - 1.0 - idx) * lg)
    chunk_decay = jnp.exp(jnp.zeros((1, 1), F32) + C * lg)

    qb = q.astype(BF16)
    scores = lax.dot_general(qb, k.astype(BF16), (((1,), (1,)), ((), ())), preferred_element_type=F32) * decay
    o = jnp.dot(scores.astype(BF16), vb, preferred_element_type=F32)
    st = state[...]
    o = o + jnp.dot(qb, st.astype(BF16), preferred_element_type=F32) * q_decay
    kv = lax.dot_general((k * k_decay).astype(BF16), vb, (((0,), (0,)), ((), ())), preferred_element_type=F32)
    st = st * chunk_decay + kv
    state[...] = st

    @pl.when(c == pl.num_programs(2) - 1)
    def _():
        sout_ref[...] = st

    o = o * lax.rsqrt(jnp.mean(o * o, axis=-1, keepdims=True) + EPS)
    y_ref[...] = (o * _silu(g_ref[...])).astype(y_ref.dtype)


def retention(qkvg, cos, sin, log_g, state0, *, batch, seq, chunk, row0, state_layer=0):
    H, dk, dv = RET_HEADS, RET_DK, RET_DV
    nc = seq // chunk
    rb0 = row0 // chunk
    kq = (H * dk) // dk
    kv = (2 * H * dk) // dv
    kg = kv + H

    def rows(b, c):
        return rb0 + b * nc + c

    in_specs = [
        pl.BlockSpec((chunk, dk), lambda b, h, c, lg: (rows(b, c), h)),
        pl.BlockSpec((chunk, dk), lambda b, h, c, lg: (rows(b, c), kq + h)),
        pl.BlockSpec((chunk, dv), lambda b, h, c, lg: (rows(b, c), kv + h)),
        pl.BlockSpec((chunk, dv), lambda b, h, c, lg: (rows(b, c), kg + h)),
        pl.BlockSpec((chunk, dk // 2), lambda b, h, c, lg: (rows(b, c), 0)),
        pl.BlockSpec((chunk, dk // 2), lambda b, h, c, lg: (rows(b, c), 0)),
    ]
    args = [qkvg, qkvg, qkvg, qkvg, cos, sin]
    if state0 is not None:
        in_specs.append(pl.BlockSpec((None, None, None, dk, dv), lambda b, h, c, lg: (state_layer, b, h, 0, 0)))
        args.append(state0)
    grid_spec = pltpu.PrefetchScalarGridSpec(
        num_scalar_prefetch=1,
        grid=(batch, H, nc),
        in_specs=in_specs,
        out_specs=[pl.BlockSpec((chunk, dv), lambda b, h, c, lg: (b * nc + c, h)),
                   pl.BlockSpec((None, None, dk, dv), lambda b, h, c, lg: (b, h, 0, 0))],
        scratch_shapes=[pltpu.VMEM((dk, dv), F32)],
    )
    return pl.pallas_call(
        functools.partial(_retention_kernel, has_state0=state0 is not None),
        grid_spec=grid_spec,
        out_shape=[jax.ShapeDtypeStruct((batch * seq, H * dv), BF16),
                   jax.ShapeDtypeStruct((batch, H, dk, dv), F32)],
        compiler_params=_params("arbitrary", "arbitrary", "arbitrary"),
        name="retention",
    )(log_g, *args)


def _att_prompt_kernel(q_ref, k_ref, v_ref, bias_ref, o_ref, kpad, vpad):
    S = q_ref.shape[0]
    QB = ATT_QBLOCK
    past = BAND_PAST_CHUNKS * CHUNK
    win = past + QB
    kpad[:past, :] = jnp.zeros((past, ATT_DH), BF16)
    vpad[:past, :] = jnp.zeros((past, ATT_DH), BF16)
    kpad[past:, :] = k_ref[...].astype(BF16)
    vpad[past:, :] = v_ref[...].astype(BF16)
    bias = bias_ref[...]
    qi = lax.broadcasted_iota(jnp.int32, (QB, win), 0)
    kj = lax.broadcasted_iota(jnp.int32, (QB, win), 1)
    chunk_shift = CHUNK.bit_length() - 1
    qc = jnp.right_shift(qi, chunk_shift)
    kc = jnp.right_shift(kj, chunk_shift)
    band = (kc >= qc) & (kc <= qc + BAND_PAST_CHUNKS)

    def body(i, carry):
        q0 = pl.multiple_of(i * QB, QB)
        qb = q_ref[pl.ds(q0, QB), :].astype(BF16)
        kw = kpad[pl.ds(q0, win), :]
        vw = vpad[pl.ds(q0, win), :]
        s = lax.dot_general(qb, kw, (((1,), (1,)), ((), ())), preferred_element_type=F32)
        s = s * ATT_DH ** -0.5 + bias
        valid = band & (kj >= past - q0)
        s = jnp.where(valid, s, NEG_INF)
        m = jnp.max(s, axis=-1, keepdims=True)
        e = jnp.exp(s - m)
        pr = (e / jnp.sum(e, axis=-1, keepdims=True)).astype(BF16)
        o_ref[pl.ds(q0, QB), :] = jnp.dot(pr, vw, preferred_element_type=F32).astype(o_ref.dtype)
        return carry

    lax.fori_loop(0, S // QB, body, 0)


def attention_prompt(qkv, bias, *, seq):
    H, dh = ATT_HEADS, ATT_DH
    past = BAND_PAST_CHUNKS * CHUNK
    return pl.pallas_call(
        _att_prompt_kernel,
        grid=(H,),
        in_specs=[pl.BlockSpec((seq, dh), lambda h: (0, h)),
                  pl.BlockSpec((seq, dh), lambda h: (0, H + h)),
                  pl.BlockSpec((seq, dh), lambda h: (0, 2 * H + h)),
                  pl.BlockSpec((None, ATT_QBLOCK, past + ATT_QBLOCK), lambda h: (h, 0, 0))],
        out_specs=pl.BlockSpec((seq, dh), lambda h: (0, h)),
        out_shape=jax.ShapeDtypeStruct((seq, H * dh), BF16),
        scratch_shapes=[pltpu.VMEM((past + seq, dh), BF16), pltpu.VMEM((past + seq, dh), BF16)],
        compiler_params=_params("arbitrary"),
        name="attention_prompt",
    )(qkv, qkv, qkv, bias)


def _att_sample_kernel(q_ref, kn_ref, vn_ref, ck_ref, cv_ref, bp_ref, bn_ref, o_ref):
    dh = ATT_DH
    for h in range(ATT_HEADS):
        sl = slice(h * dh, (h + 1) * dh)
        qb = q_ref[:, sl].astype(BF16)
        kn = kn_ref[:, sl].astype(BF16)
        vn = vn_ref[:, sl].astype(BF16)
        kp = ck_ref[:, sl].astype(BF16)
        vp = cv_ref[:, sl].astype(BF16)
        sp = lax.dot_general(qb, kp, (((1,), (1,)), ((), ())), preferred_element_type=F32)
        sn = lax.dot_general(qb, kn, (((1,), (1,)), ((), ())), preferred_element_type=F32)
        sp = sp * dh ** -0.5 + bp_ref[h]
        sn = sn * dh ** -0.5 + bn_ref[h]
        m = jnp.maximum(jnp.max(sp, axis=-1, keepdims=True), jnp.max(sn, axis=-1, keepdims=True))
        ep = jnp.exp(sp - m)
        en = jnp.exp(sn - m)
        den = jnp.sum(ep, axis=-1, keepdims=True) + jnp.sum(en, axis=-1, keepdims=True)
        o = jnp.dot((ep / den).astype(BF16), vp, preferred_element_type=F32)
        o = o + jnp.dot((en / den).astype(BF16), vn, preferred_element_type=F32)
        o_ref[:, sl] = o.astype(o_ref.dtype)


def attention_sample(qkv, cache_k, cache_v, bias_past, bias_new, *, batch, seq, row0):
    H, dh = ATT_HEADS, ATT_DH
    W = cache_k.shape[1]
    rb0 = row0 // seq
    return pl.pallas_call(
        _att_sample_kernel,
        grid=(batch,),
        in_specs=[pl.BlockSpec((seq, H * dh), lambda b: (rb0 + b, 0)),
                  pl.BlockSpec((seq, H * dh), lambda b: (rb0 + b, 1)),
                  pl.BlockSpec((seq, H * dh), lambda b: (rb0 + b, 2)),
                  pl.BlockSpec((None, W, H * dh), lambda b: (b, 0, 0)),
                  pl.BlockSpec((None, W, H * dh), lambda b: (b, 0, 0)),
                  pl.BlockSpec((H, seq, W), lambda b: (0, 0, 0)),
                  pl.BlockSpec((H, seq, seq), lambda b: (0, 0, 0))],
        out_specs=pl.BlockSpec((seq, H * dh), lambda b: (b, 0)),
        out_shape=jax.ShapeDtypeStruct((batch * seq, H * dh), BF16),
        compiler_params=_params("arbitrary"),
        name="attention_sample",
    )(qkv, qkv, qkv, cache_k, cache_v, bias_past, bias_new)


def _rel_bias(table, n_q, n_k, offset):
    period = n_q + n_k
    d = jnp.arange(period)
    d = jnp.where(d < n_k, d, d - period)
    u = jnp.take(table, jnp.clip(offset - d, -REL_CLIP, REL_CLIP) + REL_CLIP, axis=0).T.astype(F32)
    H = u.shape[0]
    skew = jnp.tile(u, (1, n_q))[:, :n_q * (period - 1)].reshape(H, n_q, period - 1)
    return skew[:, :, :n_k]


def _gmlp_kernel(uv_ref, lng_ref, ws_ref, bs_ref, y_ref, vn_ref):
    DV = y_ref.shape[1]
    cg = DV // GMLP_GROUPS
    v = jax.nn.gelu(uv_ref[:, DV:])
    mu = jnp.mean(v, axis=-1, keepdims=True)
    d = v - mu
    var = jnp.mean(d * d, axis=-1, keepdims=True)
    vn = d * lax.rsqrt(var + EPS) * lng_ref[...]
    vn_ref[...] = vn
    for g in range(GMLP_GROUPS):
        sl = slice(g * cg, (g + 1) * cg)
        mixed = jnp.dot(ws_ref[g], vn[:, sl].astype(BF16), preferred_element_type=F32) + bs_ref[g]
        y_ref[:, sl] = (jax.nn.gelu(uv_ref[:, sl]) * mixed).astype(y_ref.dtype)


def gmlp_gate(uv, ln_g, ws, bs, *, n_prompt_tiles, tile):
    T = uv.shape[0]
    DV = uv.shape[1] // 2
    G = GMLP_GROUPS
    nt = T // tile
    last = n_prompt_tiles - 1
    return pl.pallas_call(
        _gmlp_kernel,
        grid=(nt,),
        in_specs=[pl.BlockSpec((tile, 2 * DV), lambda i: (i, 0)),
                  pl.BlockSpec((1, DV), lambda i: (0, 0)),
                  pl.BlockSpec((None, G, tile, tile), lambda i: (i // n_prompt_tiles, 0, 0, 0)),
                  pl.BlockSpec((None, G, tile, 1), lambda i: (i // n_prompt_tiles, 0, 0, 0))],
        out_specs=[pl.BlockSpec((tile, DV), lambda i: (i, 0)),
                   pl.BlockSpec((tile, DV), lambda i: (jnp.maximum(i - last, 0), 0))],
        out_shape=[jax.ShapeDtypeStruct((T, DV), BF16),
                   jax.ShapeDtypeStruct(((nt - last) * tile, DV), F32)],
        compiler_params=_params("arbitrary"),
        name="gmlp_gate",
    )(uv, ln_g.reshape(1, DV), ws, bs)


def _router_kernel(h_ref, w_ref, idx_ref, gate_ref):
    logits = jnp.dot(h_ref[...], w_ref[...].astype(BF16), preferred_element_type=F32)
    lane = lax.broadcasted_iota(jnp.int32, logits.shape, 1)
    lane_f = lane.astype(F32)
    l1 = jnp.where(lane < N_EXPERTS, logits, -jnp.inf)
    v1 = jnp.max(l1, axis=-1, keepdims=True)
    i1 = jnp.min(jnp.where(l1 == v1, lane_f, float(LANES)), axis=-1, keepdims=True)
    l2 = jnp.where(lane_f == i1, -jnp.inf, l1)
    v2 = jnp.max(l2, axis=-1, keepdims=True)
    i2 = jnp.min(jnp.where(l2 == v2, lane_f, float(LANES)), axis=-1, keepdims=True)
    e2 = jnp.exp(v2 - v1)
    den = 1.0 + e2
    idx_ref[...] = jnp.where(lane == 0, i1, jnp.where(lane == 1, i2, 0.0)).astype(jnp.int32)
    gate_ref[...] = jnp.where(lane == 0, 1.0 / den, jnp.where(lane == 1, e2 / den, 0.0))


def router(h, w_router_padded, *, tm):
    T, D = h.shape
    return pl.pallas_call(
        _router_kernel,
        grid=(T // tm,),
        in_specs=[pl.BlockSpec((tm, D), lambda i: (i, 0)), pl.BlockSpec((D, LANES), lambda i: (0, 0))],
        out_specs=[pl.BlockSpec((tm, LANES), lambda i: (i, 0)), pl.BlockSpec((tm, LANES), lambda i: (i, 0))],
        out_shape=[jax.ShapeDtypeStruct((T, LANES), jnp.int32), jax.ShapeDtypeStruct((T, LANES), F32)],
        compiler_params=_params("arbitrary"),
        name="router",
    )(h, w_router_padded)


def _token_copy(src_ref, dst_ref, src_tok, dst_tok, nc, sem):
    src = src_ref.at[pl.ds(pl.multiple_of(src_tok * nc, nc), nc), :]
    dst = dst_ref.at[pl.ds(pl.multiple_of(dst_tok * nc, nc), nc), :]
    return pltpu.make_async_copy(src, dst, sem)


def _gather_kernel(tok_ref, h_hbm, o_ref, buf, sem):
    tm, D = o_ref.shape
    nc = D // LANES
    base = pl.program_id(0) * tm

    def start(r, carry):
        _token_copy(h_hbm, buf, tok_ref[base + r], r, nc, sem).start()
        return carry

    def wait(r, carry):
        _token_copy(h_hbm, buf, 0, r, nc, sem).wait()
        return carry

    lax.fori_loop(0, tm, start, 0)
    lax.fori_loop(0, tm, wait, 0)
    for c in range(nc):
        o_ref[:, c * LANES:(c + 1) * LANES] = buf[_slab_chunk(c, tm, nc), :].astype(o_ref.dtype)


def gather_rows(h_slab, tok_sorted, D, *, tm):
    nc = D // LANES
    P = tok_sorted.shape[0]
    grid_spec = pltpu.PrefetchScalarGridSpec(
        num_scalar_prefetch=1,
        grid=(P // tm,),
        in_specs=[pl.BlockSpec(memory_space=pl.ANY)],
        out_specs=pl.BlockSpec((tm, D), lambda i, tok: (i, 0)),
        scratch_shapes=[pltpu.VMEM((tm * nc, LANES), F32), pltpu.SemaphoreType.DMA(())],
    )
    return pl.pallas_call(
        _gather_kernel,
        grid_spec=grid_spec,
        out_shape=jax.ShapeDtypeStruct((P, D), BF16),
        compiler_params=_params("arbitrary"),
        name="moe_gather",
    )(tok_sorted, h_slab)


def _moe_up_kernel(te_ref, first_ref, nused_ref, x_ref, wa_ref, wb_ref, o_ref, wa_bf, wb_bf):
    m = pl.program_id(1)

    @pl.when(first_ref[m] == 1)
    def _():
        wa_bf[...] = wa_ref[...].astype(BF16)
        wb_bf[...] = wb_ref[...].astype(BF16)

    @pl.when(m < nused_ref[0])
    def _():
        x = x_ref[...]
        a = jnp.dot(x, wa_bf[...], preferred_element_type=F32)
        b = jnp.dot(x, wb_bf[...], preferred_element_type=F32)
        o_ref[...] = (_silu(a) * b).astype(o_ref.dtype)

    @pl.when(m >= nused_ref[0])
    def _():
        o_ref[...] = jnp.zeros_like(o_ref)


def moe_up(xs, w_gu, tile_expert, tile_first, n_used, *, tm, tn):
    P, D = xs.shape
    F = w_gu.shape[2] // 2
    nb = F // tn
    grid_spec = pltpu.PrefetchScalarGridSpec(
        num_scalar_prefetch=3,
        grid=(nb, P // tm),
        in_specs=[pl.BlockSpec((tm, D), lambda n, m, te, fi, nu: (m, 0)),
                  pl.BlockSpec((None, D, tn), lambda n, m, te, fi, nu: (te[m], 0, n)),
                  pl.BlockSpec((None, D, tn), lambda n, m, te, fi, nu: (te[m], 0, n + nb))],
        out_specs=pl.BlockSpec((tm, tn), lambda n, m, te, fi, nu: (m, n)),
        scratch_shapes=[pltpu.VMEM((D, tn), BF16), pltpu.VMEM((D, tn), BF16)],
    )
    return pl.pallas_call(
        _moe_up_kernel,
        grid_spec=grid_spec,
        out_shape=jax.ShapeDtypeStruct((P, F), BF16),
        compiler_params=_params("arbitrary", "arbitrary"),
        name="moe_up",
    )(tile_expert, tile_first, n_used, xs, w_gu, w_gu)


def _moe_down_kernel(te_ref, first_ref, nused_ref, x_ref, w_ref, o_ref, w_bf):
    m = pl.program_id(1)

    @pl.when(first_ref[m] == 1)
    def _():
        w_bf[...] = w_ref[...].astype(BF16)

    @pl.when(m < nused_ref[0])
    def _():
        o_ref[...] = jnp.dot(x_ref[...], w_bf[...], preferred_element_type=F32)

    @pl.when(m >= nused_ref[0])
    def _():
        o_ref[...] = jnp.zeros_like(o_ref)


def moe_down(hm, w_down, tile_expert, tile_first, n_used, *, tm, tn):
    P, F = hm.shape
    D = w_down.shape[2]
    grid_spec = pltpu.PrefetchScalarGridSpec(
        num_scalar_prefetch=3,
        grid=(D // tn, P // tm),
        in_specs=[pl.BlockSpec((tm, F), lambda n, m, te, fi, nu: (m, 0)),
                  pl.BlockSpec((None, F, tn), lambda n, m, te, fi, nu: (te[m], 0, n))],
        out_specs=pl.BlockSpec((tm, tn), lambda n, m, te, fi, nu: (m, n)),
        scratch_shapes=[pltpu.VMEM((F, tn), BF16)],
    )
    return pl.pallas_call(
        _moe_down_kernel,
        grid_spec=grid_spec,
        out_shape=jax.ShapeDtypeStruct((P, D), F32),
        compiler_params=_params("arbitrary", "arbitrary"),
        name="moe_down",
    )(tile_expert, tile_first, n_used, hm, w_down)


def _scatter_kernel(dst_ref, nvalid_ref, y_ref, c_hbm, buf, sem):
    m = pl.program_id(0)
    tm, D = y_ref.shape
    nc = D // LANES
    base = m * tm
    n = nvalid_ref[m]

    @pl.when(n > 0)
    def _():
        for c in range(nc):
            buf[_slab_chunk(c, tm, nc), :] = y_ref[:, c * LANES:(c + 1) * LANES]

        def start(r, carry):
            _token_copy(buf, c_hbm, r, dst_ref[base + r], nc, sem).start()
            return carry

        def wait(r, carry):
            _token_copy(buf, c_hbm, r, 0, nc, sem).wait()
            return carry

        lax.fori_loop(0, n, start, 0)
        lax.fori_loop(0, n, wait, 0)


def scatter_rows(ys, dst_sorted, tile_valid, n_out, *, tm):
    P, D = ys.shape
    nc = D // LANES
    grid_spec = pltpu.PrefetchScalarGridSpec(
        num_scalar_prefetch=2,
        grid=(P // tm,),
        in_specs=[pl.BlockSpec((tm, D), lambda i, dst, nv: (i, 0))],
        out_specs=pl.BlockSpec(memory_space=pl.ANY),
        scratch_shapes=[pltpu.VMEM((tm * nc, LANES), F32), pltpu.SemaphoreType.DMA(())],
    )
    return pl.pallas_call(
        _scatter_kernel,
        grid_spec=grid_spec,
        out_shape=jax.ShapeDtypeStruct((n_out * nc, LANES), F32),
        compiler_params=_params("arbitrary"),
        name="moe_scatter",
    )(dst_sorted, tile_valid, ys)


def _combine_norm_kernel(x_ref, gate_ref, c0_ref, c1_ref, g_ref, xo_ref, hn_ref):
    gate = gate_ref[...]
    g0, g1 = gate[:, 0:1], gate[:, 1:2]
    tm, D = x_ref.shape
    nc = D // LANES
    ss = jnp.zeros((tm, 1), F32)
    for c in range(nc):
        sl = slice(c * LANES, (c + 1) * LANES)
        chunk = _slab_chunk(c, tm, nc)
        xn = x_ref[:, sl] + (g0 * c0_ref[chunk, :] + g1 * c1_ref[chunk, :])
        xo_ref[:, sl] = xn
        ss = ss + jnp.sum(xn * xn, axis=-1, keepdims=True)
    xn = xo_ref[...]
    hn_ref[...] = (xn * lax.rsqrt(ss / D + EPS) * g_ref[...]).astype(hn_ref.dtype)


def moe_combine_norm(x, gates, contrib, g, *, tm):
    T, D = x.shape
    nc = D // LANES
    nt = T // tm
    return pl.pallas_call(
        _combine_norm_kernel,
        grid=(nt,),
        in_specs=[pl.BlockSpec((tm, D), lambda i: (i, 0)),
                  pl.BlockSpec((tm, LANES), lambda i: (i, 0)),
                  pl.BlockSpec((tm * nc, LANES), lambda i: (i, 0)),
                  pl.BlockSpec((tm * nc, LANES), lambda i: (i + nt, 0)),
                  pl.BlockSpec((1, D), lambda i: (0, 0))],
        out_specs=[pl.BlockSpec((tm, D), lambda i: (i, 0)), pl.BlockSpec((tm, D), lambda i: (i, 0))],
        out_shape=[jax.ShapeDtypeStruct((T, D), F32), jax.ShapeDtypeStruct((T, D), BF16)],
        compiler_params=_params("arbitrary"),
        name="moe_combine_norm",
    )(x, gates, contrib, contrib, g.reshape(1, D))


def _route_plan(idx, tm):
    T = idx.shape[0]
    E = N_EXPERTS
    n_tiles = (TOP_K * T) // tm + E
    experts = jnp.arange(E, dtype=jnp.int32)
    flat_e = idx.reshape(-1)
    onehot = (flat_e[:, None] == experts[None, :]).astype(jnp.int32)
    csum = jnp.cumsum(onehot, axis=0)
    rank = jnp.sum((csum - onehot) * onehot, axis=1)
    counts = csum[-1]
    tiles_e = (counts + tm - 1) // tm
    tile_end = jnp.cumsum(tiles_e)
    tile_start = tile_end - tiles_e
    pos = (jnp.sum(onehot * tile_start[None, :], axis=1) * tm + rank).astype(jnp.int32)
    assign = jnp.zeros((n_tiles * tm,), jnp.int32).at[pos].set(jnp.arange(TOP_K * T, dtype=jnp.int32))
    tok_sorted = assign // TOP_K
    dst_sorted = (assign % TOP_K) * T + tok_sorted
    n_used = tile_end[-1]
    m_ids = jnp.arange(n_tiles, dtype=jnp.int32)
    te = jnp.sum((jnp.minimum(m_ids, n_used - 1)[:, None] >= tile_end[None, :]).astype(jnp.int32), axis=1)
    first = ((m_ids == 0) | (te != jnp.roll(te, 1))) & (m_ids < n_used)
    te_onehot = (te[:, None] == experts[None, :]).astype(jnp.int32)
    rows_before = (m_ids - jnp.sum(te_onehot * tile_start[None, :], axis=1)) * tm
    valid = jnp.clip(jnp.sum(te_onehot * counts[None, :], axis=1) - rows_before, 0, tm)
    valid = jnp.where(m_ids < n_used, valid, 0)
    return (tok_sorted, dst_sorted, te.astype(jnp.int32), first.astype(jnp.int32), valid.astype(jnp.int32),
            n_used.reshape(1).astype(jnp.int32))


def moe(x, h_bf, h_slab, w_router, w_gu, w_down, g_next, *, layer, tm_tok):
    T = x.shape[0]
    E = N_EXPERTS
    w_r = jnp.pad(w_router[layer], ((0, 0), (0, LANES - E)))
    idx, gates = router(h_bf, w_r, tm=tm_tok)
    tok_sorted, dst_sorted, te, first, valid, n_used = _route_plan(idx[:, :TOP_K], MOE_TM)
    te = te + layer * E
    xs = gather_rows(h_slab, tok_sorted, x.shape[1], tm=MOE_TM)
    hm = moe_up(xs, w_gu.reshape((-1,) + w_gu.shape[2:]), te, first, n_used, tm=MOE_TM, tn=1024)
    ys = moe_down(hm, w_down.reshape((-1,) + w_down.shape[2:]), te, first, n_used, tm=MOE_TM, tn=512)
    contrib = scatter_rows(ys, dst_sorted, valid, TOP_K * T, tm=MOE_TM)
    return moe_combine_norm(x, gates, contrib, g_next, tm=tm_tok)


def kernel(x_prompt, x_sample, state_ret, cache_k, cache_v, p_prompt, p_sample, g_mix, g_ffn, g_ple, g_final,
           ret_w_in, ret_w_out, att_w_in, att_w_out, att_rel_bias, gmlp_w_in, gmlp_ln_g, gmlp_w_s, gmlp_b_s,
           gmlp_w_out, ffn_w_gu, ffn_w_down, moe_w_router, moe_w_gu, moe_w_down, ple_w_in, ple_w_gate):
    B, S, D = x_prompt.shape
    BS, L, _ = x_sample.shape
    depth = g_mix.shape[0]
    NP, NS = B * S, BS * L
    T = NP + NS
    assert B == 1 and S % RET_CHUNK == 0 and S % ATT_QBLOCK == 0 and L <= CHUNK
    TM = T // 8
    TMS = T // 16

    x = jnp.concatenate([x_prompt.reshape(NP, D), x_sample.reshape(NS, D)], axis=0)
    p_all = jnp.concatenate([p_prompt.reshape(depth, NP, -1), p_sample.reshape(depth, NS, -1)], axis=1)

    pos = jnp.concatenate([jnp.arange(S, dtype=jnp.int32), jnp.tile(PAST_LEN + jnp.arange(L, dtype=jnp.int32), BS)])
    freqs = ROPE_BASE ** (-jnp.arange(0, RET_DK, 2, dtype=F32) / RET_DK)
    ang = pos.astype(F32)[:, None] * freqs[None, :]
    cos, sin = jnp.cos(ang), jnp.sin(ang)
    log_g = jnp.log1p(-jnp.exp2(-5.0 - jnp.arange(RET_HEADS, dtype=F32)))

    ret_p, ret_s, k_p, v_p, k_s, v_s, gv = [], [], [], [], [], [], []
    for i in range(depth):
        kind, slot = i % 3, i // 3
        (h,) = rmsnorm(x, g_mix[i], [BF16], TMS)
        if kind == 0:
            qkvg = matmul(h, ret_w_in, layer=slot, tm=TM, tn=1024)
            y_p, st_p = retention(qkvg, cos, sin, log_g, None, batch=B, seq=S, chunk=RET_CHUNK, row0=0)
            y_s, st_s = retention(qkvg, cos, sin, log_g, state_ret, batch=BS, seq=L, chunk=L, row0=NP,
                                  state_layer=slot)
            ret_p.append(st_p)
            ret_s.append(st_s)
            y = jnp.concatenate([y_p, y_s], axis=0)
            x = matmul(y, ret_w_out, layer=slot, tm=TMS, tn=512, res=x)
        elif kind == 1:
            HD = ATT_HEADS * ATT_DH
            W = cache_k.shape[2]
            past = BAND_PAST_CHUNKS * CHUNK
            qkv = matmul(h, att_w_in, layer=slot, tm=TM, tn=1024)
            bias_p = _rel_bias(att_rel_bias[slot], ATT_QBLOCK, past + ATT_QBLOCK, past)
            bias_s = _rel_bias(att_rel_bias[slot], L, W + L, W)
            o_p = attention_prompt(qkv, bias_p, seq=S)
            o_s = attention_sample(qkv, cache_k[slot].reshape(BS, W, HD), cache_v[slot].reshape(BS, W, HD),
                                   bias_s[:, :, :W], bias_s[:, :, W:], batch=BS, seq=L, row0=NP)
            w = min(past, S)
            k_p.append(qkv[NP - w:NP, HD:2 * HD].reshape(B, w, ATT_HEADS, ATT_DH))
            v_p.append(qkv[NP - w:NP, 2 * HD:].reshape(B, w, ATT_HEADS, ATT_DH))
            k_s.append(qkv[NP:, HD:2 * HD].reshape(BS, L, ATT_HEADS, ATT_DH))
            v_s.append(qkv[NP:, 2 * HD:].reshape(BS, L, ATT_HEADS, ATT_DH))
            o = jnp.concatenate([o_p, o_s], axis=0)
            x = matmul(o, att_w_out, layer=slot, tm=TM, tn=1024, res=x)
        else:
            tile = GMLP_CHUNK
            uv = matmul(h, gmlp_w_in, layer=slot, tm=TM, tn=1024)
            r = jnp.arange(tile)
            ws_p = jnp.where((r[:, None] >= r[None, :])[None], gmlp_w_s[slot][:, :tile, :tile], 0.0)
            same = (r[:, None] // L == r[None, :] // L) & (r[:, None] >= r[None, :])
            ws_s = jnp.where(same[None], jnp.tile(gmlp_w_s[slot][:, :L, :L], (1, tile // L, tile // L)), 0.0)
            bs_p = gmlp_b_s[slot][:, :tile]
            bs_s = jnp.tile(gmlp_b_s[slot][:, :L], (1, tile // L))
            ws = jnp.stack([ws_p, ws_s]).astype(BF16)
            bs = jnp.stack([bs_p, bs_s])[..., None]
            yg, vn = gmlp_gate(uv, gmlp_ln_g[slot], ws, bs, n_prompt_tiles=NP // tile, tile=tile)
            gv.append((vn[:tile].reshape(B, tile, -1), vn[tile:].reshape(BS, L, -1)))
            x = matmul(yg, gmlp_w_out, layer=slot, tm=TMS, tn=512, res=x)

        if i % 2 == 0:
            (h,) = rmsnorm(x, g_ffn[i], [BF16], TMS)
            x = dense_ffn(x, h, ffn_w_gu[i // 2], ffn_w_down[i // 2], tm_up=TM, tm_down=TMS)
            (hn,) = rmsnorm(x, g_ple[i], [BF16], TMS)
        else:
            h, h_slab = rmsnorm(x, g_ffn[i], [BF16], TMS, slab_f32=True)
            x, hn = moe(x, h, h_slab, moe_w_router, moe_w_gu, moe_w_down, g_ple[i], layer=i // 2, tm_tok=TMS)

        x = ple(x, hn, p_all, ple_w_gate, ple_w_in, layer=i, tm=TMS, tn=1024)

    (y,) = rmsnorm(x, g_final, [F32], TMS)
    y_prompt = y[:NP].reshape(B, S, D)
    y_sample = y[NP:].reshape(BS, L, D)
    return (y_prompt, y_sample, jnp.stack(ret_p), jnp.stack(ret_s), jnp.stack(k_p), jnp.stack(v_p),
            jnp.stack(k_s), jnp.stack(v_s), jnp.stack([a for a, _ in gv]), jnp.stack([b for _, b in gv]))
```

```python
import functools

import jax
import jax.numpy as jnp
from jax import lax
from jax.experimental import pallas as pl
from jax.experimental.pallas import tpu as pltpu

F32 = jnp.float32
BF16 = jnp.bfloat16

EPS = 1e-6
NEG_INF = -1e30
CHUNK = 64
ROPE_BASE = 10000.0
RET_HEADS, RET_DK, RET_DV = 8, 256, 512
ATT_HEADS, ATT_DH = 16, 128
BAND_PAST_CHUNKS = 8
REL_CLIP = 128
GMLP_CHUNK, GMLP_GROUPS = 128, 8
N_EXPERTS, TOP_K = 8, 2
PAST_LEN = 2048

LANES = 128
DMA_PRIORITIES = 2
V7X_VMEM_LIMIT_BYTES = 60 * 1000 * 1024

RET_CHUNK = 256
ATT_QBLOCK = 2 * CHUNK
MOE_TM = 512


def _params(*sem):
    return pltpu.CompilerParams(dimension_semantics=sem, vmem_limit_bytes=V7X_VMEM_LIMIT_BYTES)


def _silu(x):
    return x * jax.nn.sigmoid(x)


def _slab_chunk(c, n_tokens, nc):
    return pl.ds(c, n_tokens, stride=nc)


def _rmsnorm_kernel(x_ref, g_ref, *o_refs):
    x = x_ref[...]
    tm, D = x.shape
    y = x * lax.rsqrt(jnp.mean(x * x, axis=-1, keepdims=True) + EPS) * g_ref[...]
    for o_ref in o_refs:
        if o_ref.shape == x.shape:
            o_ref[...] = y.astype(o_ref.dtype)
        else:
            nc = D // LANES
            for c in range(nc):
                o_ref[_slab_chunk(c, tm, nc), :] = y[:, c * LANES:(c + 1) * LANES]


def rmsnorm(x, g, dtypes, tm, slab_f32=False):
    T, D = x.shape
    out_specs = [pl.BlockSpec((tm, D), lambda i: (i, 0)) for _ in dtypes]
    out_shape = [jax.ShapeDtypeStruct((T, D), dt) for dt in dtypes]
    if slab_f32:
        out_specs.append(pl.BlockSpec((tm * D // LANES, LANES), lambda i: (i, 0)))
        out_shape.append(jax.ShapeDtypeStruct((T * D // LANES, LANES), F32))
    return pl.pallas_call(
        _rmsnorm_kernel,
        grid=(T // tm,),
        in_specs=[pl.BlockSpec((tm, D), lambda i: (i, 0)), pl.BlockSpec((1, D), lambda i: (0, 0))],
        out_specs=out_specs,
        out_shape=out_shape,
        compiler_params=_params("arbitrary"),
        name="rmsnorm",
    )(x, g.reshape(1, D))


def _mm_kernel(a_ref, w_ref, *rest, has_res, cast_w):
    if has_res:
        r_ref, o_ref, *scratch = rest
    else:
        o_ref, *scratch = rest
    if cast_w:
        w_bf = scratch[0]

        @pl.when(pl.program_id(1) == 0)
        def _():
            w_bf[...] = w_ref[...].astype(BF16)

        w = w_bf[...]
    else:
        w = w_ref[...]
    acc = jnp.dot(a_ref[...], w, preferred_element_type=F32)
    if has_res:
        acc = r_ref[...] + acc
    o_ref[...] = acc.astype(o_ref.dtype)


def _weight_spec(w, layer, tn):
    K = w.shape[-2]
    if w.ndim == 2:
        return pl.BlockSpec((K, tn), lambda n, m: (0, n))
    return pl.BlockSpec((None, K, tn), lambda n, m: (layer, 0, n))


def matmul(a, w, *, tm, tn, layer=0, out_dtype=F32, res=None):
    M, K = a.shape
    N = w.shape[-1]
    cast_w = w.dtype != BF16
    in_specs = [pl.BlockSpec((tm, K), lambda n, m: (m, 0)), _weight_spec(w, layer, tn)]
    args = [a, w]
    if res is not None:
        in_specs.append(pl.BlockSpec((tm, tn), lambda n, m: (m, n)))
        args.append(res)
    return pl.pallas_call(
        functools.partial(_mm_kernel, has_res=res is not None, cast_w=cast_w),
        grid=(N // tn, M // tm),
        in_specs=in_specs,
        out_specs=pl.BlockSpec((tm, tn), lambda n, m: (m, n)),
        out_shape=jax.ShapeDtypeStruct((M, N), out_dtype),
        scratch_shapes=[pltpu.VMEM((K, tn), BF16)] if cast_w else [],
        compiler_params=_params("arbitrary", "arbitrary"),
        name="matmul",
    )(*args)


def _swiglu_up_kernel(x_ref, wa_ref, wb_ref, o_ref):
    x = x_ref[...]
    a = jnp.dot(x, wa_ref[...], preferred_element_type=F32)
    b = jnp.dot(x, wb_ref[...], preferred_element_type=F32)
    o_ref[...] = (_silu(a) * b).astype(o_ref.dtype)


def swiglu_up(x, w_gu, *, tm, tn):
    T, D = x.shape
    F = w_gu.shape[1] // 2
    nb = F // tn
    return pl.pallas_call(
        _swiglu_up_kernel,
        grid=(nb, T // tm),
        in_specs=[pl.BlockSpec((tm, D), lambda n, m: (m, 0)),
                  pl.BlockSpec((D, tn), lambda n, m: (0, n)),
                  pl.BlockSpec((D, tn), lambda n, m: (0, n + nb))],
        out_specs=pl.BlockSpec((tm, tn), lambda n, m: (m, n)),
        out_shape=jax.ShapeDtypeStruct((T, F), BF16),
        compiler_params=_params("arbitrary", "arbitrary"),
        name="swiglu_up",
    )(x, w_gu, w_gu)


def dense_ffn(x, h, w_gu, w_down, *, tm_up, tm_down):
    tn = 512
    F = w_gu.shape[1] // 2
    FP = -(-F // tn) * tn
    w_gu_p = jnp.concatenate([_pad_cols(w_gu[:, :F], FP), _pad_cols(w_gu[:, F:], FP)], axis=1).astype(BF16)
    w_dn_p = jnp.pad(w_down, ((0, FP - F), (0, 0))).astype(BF16)
    hm = swiglu_up(h, w_gu_p, tm=tm_up, tn=tn)
    return matmul(hm, w_dn_p, tm=tm_down, tn=tn, res=x)


def _pad_cols(w, n):
    return jnp.pad(w, ((0, 0), (0, n - w.shape[1])))


def _ple_kernel(hn_ref, wg_ref, p_ref, wi_ref, x_ref, o_ref, wg_bf, wi_bf):
    @pl.when(pl.program_id(1) == 0)
    def _():
        wg_bf[...] = wg_ref[...].astype(BF16)
        wi_bf[...] = wi_ref[...].astype(BF16)

    gate = jax.nn.sigmoid(jnp.dot(hn_ref[...], wg_bf[...], preferred_element_type=F32))
    emb = jnp.dot(p_ref[...].astype(BF16), wi_bf[...], preferred_element_type=F32)
    o_ref[...] = x_ref[...] + gate * emb


def ple(x, hn, p, w_gate, w_in, *, layer, tm, tn):
    T, D = x.shape
    DP = p.shape[2]
    return pl.pallas_call(
        _ple_kernel,
        grid=(D // tn, T // tm),
        in_specs=[pl.BlockSpec((tm, D), lambda n, m: (m, 0)),
                  _weight_spec(w_gate, layer, tn),
                  pl.BlockSpec((None, tm, DP), lambda n, m: (layer, m, 0)),
                  _weight_spec(w_in, layer, tn),
                  pl.BlockSpec((tm, tn), lambda n, m: (m, n))],
        out_specs=pl.BlockSpec((tm, tn), lambda n, m: (m, n)),
        out_shape=jax.ShapeDtypeStruct((T, D), F32),
        scratch_shapes=[pltpu.VMEM((D, tn), BF16), pltpu.VMEM((DP, tn), BF16)],
        compiler_params=_params("arbitrary", "arbitrary"),
        name="ple",
    )(hn, w_gate, p, w_in, x)


def _retention_kernel(lg_ref, q_ref, k_ref, v_ref, g_ref, cos_ref, sin_ref, *rest, has_state0):
    if has_state0:
        s0_ref, y_ref, sout_ref, state = rest
    else:
        y_ref, sout_ref, state = rest
    hb = pl.program_id(1)
    c = pl.program_id(2)
    C = q_ref.shape[0]
    hp = state.shape[0]
    dk, dv = RET_DK, RET_DV
    half = dk // 2

    @pl.when(c == 0)
    def _():
        if has_state0:
            state[...] = s0_ref[...]
        else:
            state[...] = jnp.zeros_like(state)

    cos = cos_ref[...]
    sin = sin_ref[...]

    def rotary(x):
        x1, x2 = x[:, :half], x[:, half:]
        return jnp.concatenate([x1 * cos - x2 * sin, x1 * sin + x2 * cos], axis=-1)

    row = lax.broadcasted_iota(jnp.int32, (C, C), 0)
    col = lax.broadcasted_iota(jnp.int32, (C, C), 1)
    rel = (row - col).astype(F32)
    rel_pos = jnp.maximum(rel, 0.0)
    idx = lax.broadcasted_iota(jnp.int32, (C, 1), 0).astype(F32)

    for j in range(hp):
        lg = lg_ref[hb * hp + j]
        q = rotary(q_ref[:, j * dk:(j + 1) * dk])
        k = rotary(k_ref[:, j * dk:(j + 1) * dk]) * dk ** -0.5
        vb = v_ref[:, j * dv:(j + 1) * dv].astype(BF16)
        decay = jnp.where(rel >= 0, jnp.exp(rel_pos * lg), 0.0)
        q_decay = jnp.exp((idx + 1.0) * lg)
        k_decay = jnp.exp((C - 1.0 - idx) * lg)
        chunk_decay = jnp.exp(jnp.zeros((1, 1), F32) + C * lg)

        qb = q.astype(BF16)
        scores = lax.dot_general(qb, k.astype(BF16), (((1,), (1,)), ((), ())), preferred_element_type=F32) * decay
        o = jnp.dot(scores.astype(BF16), vb, preferred_element_type=F32)
        st = state[j]
        o = o + jnp.dot(qb, st.astype(BF16), preferred_element_type=F32) * q_decay
        kv = lax.dot_general((k * k_decay).astype(BF16), vb, (((0,), (0,)), ((), ())), preferred_element_type=F32)
        st = st * chunk_decay + kv
        state[j] = st
        sout_ref[j] = st

        o = o * lax.rsqrt(jnp.mean(o * o, axis=-1, keepdims=True) + EPS)
        y_ref[:, j * dv:(j + 1) * dv] = (o * _silu(g_ref[:, j * dv:(j + 1) * dv])).astype(y_ref.dtype)


def retention(qkvg, cos, sin, log_g, state0, *, batch, seq, chunk, row0, heads_per_step, state_layer=0):
    H, dk, dv = RET_HEADS, RET_DK, RET_DV
    hp = heads_per_step
    nhb = H // hp
    nc = seq // chunk
    rb0 = row0 // chunk
    kq = nhb
    kv = (2 * H * dk) // (hp * dv)
    kg = kv + nhb

    def rows(b, c):
        return rb0 + b * nc + c

    in_specs = [
        pl.BlockSpec((chunk, hp * dk), lambda b, h, c, lg: (rows(b, c), h)),
        pl.BlockSpec((chunk, hp * dk), lambda b, h, c, lg: (rows(b, c), kq + h)),
        pl.BlockSpec((chunk, hp * dv), lambda b, h, c, lg: (rows(b, c), kv + h)),
        pl.BlockSpec((chunk, hp * dv), lambda b, h, c, lg: (rows(b, c), kg + h)),
        pl.BlockSpec((chunk, dk // 2), lambda b, h, c, lg: (rows(b, c), 0)),
        pl.BlockSpec((chunk, dk // 2), lambda b, h, c, lg: (rows(b, c), 0)),
    ]
    args = [qkvg, qkvg, qkvg, qkvg, cos, sin]
    if state0 is not None:
        in_specs.append(pl.BlockSpec((None, None, hp, dk, dv), lambda b, h, c, lg: (state_layer, b, h, 0, 0)))
        args.append(state0)
    grid_spec = pltpu.PrefetchScalarGridSpec(
        num_scalar_prefetch=1,
        grid=(batch, nhb, nc),
        in_specs=in_specs,
        out_specs=[pl.BlockSpec((chunk, hp * dv), lambda b, h, c, lg: (b * nc + c, h)),
                   pl.BlockSpec((None, hp, dk, dv), lambda b, h, c, lg: (b, h, 0, 0))],
        scratch_shapes=[pltpu.VMEM((hp, dk, dv), F32)],
    )
    return pl.pallas_call(
        functools.partial(_retention_kernel, has_state0=state0 is not None),
        grid_spec=grid_spec,
        out_shape=[jax.ShapeDtypeStruct((batch * seq, H * dv), BF16),
                   jax.ShapeDtypeStruct((batch, H, dk, dv), F32)],
        compiler_params=_params("arbitrary", "arbitrary", "arbitrary"),
        name="retention",
    )(log_g, *args)


def _att_prompt_kernel(q_ref, k_ref, v_ref, bias_ref, o_ref, kpad, vpad):
    S = q_ref.shape[0]
    QB = ATT_QBLOCK
    past = BAND_PAST_CHUNKS * CHUNK
    win = past + QB
    kpad[:past, :] = jnp.zeros((past, ATT_DH), BF16)
    vpad[:past, :] = jnp.zeros((past, ATT_DH), BF16)
    kpad[past:, :] = k_ref[...].astype(BF16)
    vpad[past:, :] = v_ref[...].astype(BF16)
    bias = bias_ref[...]
    qi = lax.broadcasted_iota(jnp.int32, (QB, win), 0)
    kj = lax.broadcasted_iota(jnp.int32, (QB, win), 1)
    chunk_shift = CHUNK.bit_length() - 1
    qc = jnp.right_shift(qi, chunk_shift)
    kc = jnp.right_shift(kj, chunk_shift)
    band = (kc >= qc) & (kc <= qc + BAND_PAST_CHUNKS)

    def body(i, carry):
        q0 = pl.multiple_of(i * QB, QB)
        qb = q_ref[pl.ds(q0, QB), :].astype(BF16)
        kw = kpad[pl.ds(q0, win), :]
        vw = vpad[pl.ds(q0, win), :]
        s = lax.dot_general(qb, kw, (((1,), (1,)), ((), ())), preferred_element_type=F32)
        s = s * ATT_DH ** -0.5 + bias
        valid = band & (kj >= past - q0)
        s = jnp.where(valid, s, NEG_INF)
        m = jnp.max(s, axis=-1, keepdims=True)
        e = jnp.exp(s - m)
        pr = (e / jnp.sum(e, axis=-1, keepdims=True)).astype(BF16)
        o_ref[pl.ds(q0, QB), :] = jnp.dot(pr, vw, preferred_element_type=F32).astype(o_ref.dtype)
        return carry

    lax.fori_loop(0, S // QB, body, 0, unroll=4)


def attention_prompt(qkv, bias, *, seq):
    H, dh = ATT_HEADS, ATT_DH
    past = BAND_PAST_CHUNKS * CHUNK
    return pl.pallas_call(
        _att_prompt_kernel,
        grid=(H,),
        in_specs=[pl.BlockSpec((seq, dh), lambda h: (0, h)),
                  pl.BlockSpec((seq, dh), lambda h: (0, H + h)),
                  pl.BlockSpec((seq, dh), lambda h: (0, 2 * H + h)),
                  pl.BlockSpec((None, ATT_QBLOCK, past + ATT_QBLOCK), lambda h: (h, 0, 0))],
        out_specs=pl.BlockSpec((seq, dh), lambda h: (0, h)),
        out_shape=jax.ShapeDtypeStruct((seq, H * dh), BF16),
        scratch_shapes=[pltpu.VMEM((past + seq, dh), BF16), pltpu.VMEM((past + seq, dh), BF16)],
        compiler_params=_params("arbitrary"),
        name="attention_prompt",
    )(qkv, qkv, qkv, bias)


def _att_sample_kernel(q_ref, kn_ref, vn_ref, ck_ref, cv_ref, bp_ref, bn_ref, o_ref):
    dh = ATT_DH
    H = ATT_HEADS
    W = ck_ref.shape[0] // H
    for h in range(H):
        sl = slice(h * dh, (h + 1) * dh)
        qb = q_ref[:, sl].astype(BF16)
        kn = kn_ref[:, sl].astype(BF16)
        vn = vn_ref[:, sl].astype(BF16)
        kp = ck_ref[pl.ds(h, W, stride=H), :].astype(BF16)
        vp = cv_ref[pl.ds(h, W, stride=H), :].astype(BF16)
        sp = lax.dot_general(qb, kp, (((1,), (1,)), ((), ())), preferred_element_type=F32)
        sn = lax.dot_general(qb, kn, (((1,), (1,)), ((), ())), preferred_element_type=F32)
        sp = sp * dh ** -0.5 + bp_ref[h]
        sn = sn * dh ** -0.5 + bn_ref[h]
        m = jnp.maximum(jnp.max(sp, axis=-1, keepdims=True), jnp.max(sn, axis=-1, keepdims=True))
        ep = jnp.exp(sp - m)
        en = jnp.exp(sn - m)
        den = jnp.sum(ep, axis=-1, keepdims=True) + jnp.sum(en, axis=-1, keepdims=True)
        o = jnp.dot((ep / den).astype(BF16), vp, preferred_element_type=F32)
        o = o + jnp.dot((en / den).astype(BF16), vn, preferred_element_type=F32)
        o_ref[:, sl] = o.astype(o_ref.dtype)


def attention_sample(qkv, cache_k, cache_v, bias_past, bias_new, *, batch, seq, row0, layer):
    H, dh = ATT_HEADS, ATT_DH
    W = cache_k.shape[2]
    rb0 = row0 // seq
    cache_k = cache_k.reshape(-1, dh)
    cache_v = cache_v.reshape(-1, dh)
    return pl.pallas_call(
        _att_sample_kernel,
        grid=(batch,),
        in_specs=[pl.BlockSpec((seq, H * dh), lambda b: (rb0 + b, 0)),
                  pl.BlockSpec((seq, H * dh), lambda b: (rb0 + b, 1)),
                  pl.BlockSpec((seq, H * dh), lambda b: (rb0 + b, 2)),
                  pl.BlockSpec((W * H, dh), lambda b: (layer * batch + b, 0)),
                  pl.BlockSpec((W * H, dh), lambda b: (layer * batch + b, 0)),
                  pl.BlockSpec((H, seq, W), lambda b: (0, 0, 0)),
                  pl.BlockSpec((H, seq, seq), lambda b: (0, 0, 0))],
        out_specs=pl.BlockSpec((seq, H * dh), lambda b: (b, 0)),
        out_shape=jax.ShapeDtypeStruct((batch * seq, H * dh), BF16),
        compiler_params=_params("arbitrary"),
        name="attention_sample",
    )(qkv, qkv, qkv, cache_k, cache_v, bias_past, bias_new)


def _rel_bias(table, n_q, n_k, offset):
    period = n_q + n_k
    d = jnp.arange(period)
    d = jnp.where(d < n_k, d, d - period)
    u = jnp.take(table, jnp.clip(offset - d, -REL_CLIP, REL_CLIP) + REL_CLIP, axis=0).T.astype(F32)
    H = u.shape[0]
    skew = jnp.tile(u, (1, n_q))[:, :n_q * (period - 1)].reshape(H, n_q, period - 1)
    return skew[:, :, :n_k]


def _gmlp_kernel(uv_ref, lng_ref, ws_ref, bs_ref, y_ref, vn_ref):
    DV = y_ref.shape[1]
    cg = DV // GMLP_GROUPS
    v = jax.nn.gelu(uv_ref[:, DV:])
    mu = jnp.mean(v, axis=-1, keepdims=True)
    d = v - mu
    var = jnp.mean(d * d, axis=-1, keepdims=True)
    vn = d * lax.rsqrt(var + EPS) * lng_ref[...]
    vn_ref[...] = vn
    for g in range(GMLP_GROUPS):
        sl = slice(g * cg, (g + 1) * cg)
        mixed = jnp.dot(ws_ref[g], vn[:, sl].astype(BF16), preferred_element_type=F32) + bs_ref[g]
        y_ref[:, sl] = (jax.nn.gelu(uv_ref[:, sl]) * mixed).astype(y_ref.dtype)


def gmlp_gate(uv, ln_g, ws, bs, *, n_prompt_tiles, tile):
    T = uv.shape[0]
    DV = uv.shape[1] // 2
    G = GMLP_GROUPS
    nt = T // tile
    last = n_prompt_tiles - 1
    return pl.pallas_call(
        _gmlp_kernel,
        grid=(nt,),
        in_specs=[pl.BlockSpec((tile, 2 * DV), lambda i: (i, 0)),
                  pl.BlockSpec((1, DV), lambda i: (0, 0)),
                  pl.BlockSpec((None, G, tile, tile), lambda i: (i // n_prompt_tiles, 0, 0, 0)),
                  pl.BlockSpec((None, G, tile, 1), lambda i: (i // n_prompt_tiles, 0, 0, 0))],
        out_specs=[pl.BlockSpec((tile, DV), lambda i: (i, 0)),
                   pl.BlockSpec((tile, DV), lambda i: (jnp.maximum(i - last, 0), 0))],
        out_shape=[jax.ShapeDtypeStruct((T, DV), BF16),
                   jax.ShapeDtypeStruct(((nt - last) * tile, DV), F32)],
        compiler_params=_params("arbitrary"),
        name="gmlp_gate",
    )(uv, ln_g.reshape(1, DV), ws, bs)


def _router_kernel(h_ref, w_ref, idx_ref, gate_ref):
    logits = jnp.dot(h_ref[...], w_ref[...].astype(BF16), preferred_element_type=F32)
    lane = lax.broadcasted_iota(jnp.int32, logits.shape, 1)
    lane_f = lane.astype(F32)
    l1 = jnp.where(lane < N_EXPERTS, logits, -jnp.inf)
    v1 = jnp.max(l1, axis=-1, keepdims=True)
    i1 = jnp.min(jnp.where(l1 == v1, lane_f, float(LANES)), axis=-1, keepdims=True)
    l2 = jnp.where(lane_f == i1, -jnp.inf, l1)
    v2 = jnp.max(l2, axis=-1, keepdims=True)
    i2 = jnp.min(jnp.where(l2 == v2, lane_f, float(LANES)), axis=-1, keepdims=True)
    e2 = jnp.exp(v2 - v1)
    den = 1.0 + e2
    idx_ref[...] = jnp.where(lane == 0, i1, jnp.where(lane == 1, i2, 0.0)).astype(jnp.int32)
    gate_ref[...] = jnp.where(lane == 0, 1.0 / den, jnp.where(lane == 1, e2 / den, 0.0))


def router(h, w_router_padded, *, tm):
    T, D = h.shape
    return pl.pallas_call(
        _router_kernel,
        grid=(T // tm,),
        in_specs=[pl.BlockSpec((tm, D), lambda i: (i, 0)), pl.BlockSpec((D, LANES), lambda i: (0, 0))],
        out_specs=[pl.BlockSpec((tm, LANES), lambda i: (i, 0)), pl.BlockSpec((tm, LANES), lambda i: (i, 0))],
        out_shape=[jax.ShapeDtypeStruct((T, LANES), jnp.int32), jax.ShapeDtypeStruct((T, LANES), F32)],
        compiler_params=_params("arbitrary"),
        name="router",
    )(h, w_router_padded)


def _token_copy(src_ref, dst_ref, src_tok, dst_tok, nc, sem):
    src = src_ref.at[pl.ds(pl.multiple_of(src_tok * nc, nc), nc), :]
    dst = dst_ref.at[pl.ds(pl.multiple_of(dst_tok * nc, nc), nc), :]
    return pltpu.make_async_copy(src, dst, sem)


def _gather_kernel(tok_ref, h_hbm, o_ref, buf, sem):
    tm, D = o_ref.shape
    nc = D // LANES
    base = pl.program_id(0) * tm

    def start(i, carry):
        for p in range(DMA_PRIORITIES):
            r = i * DMA_PRIORITIES + p
            _token_copy(h_hbm, buf, tok_ref[base + r], r, nc, sem).start(priority=p)
        return carry

    def wait(r, carry):
        _token_copy(h_hbm, buf, 0, r, nc, sem).wait()
        return carry

    lax.fori_loop(0, tm // DMA_PRIORITIES, start, 0)
    lax.fori_loop(0, tm, wait, 0)
    for c in range(nc):
        o_ref[:, c * LANES:(c + 1) * LANES] = buf[_slab_chunk(c, tm, nc), :].astype(o_ref.dtype)


def gather_rows(h_slab, tok_sorted, D, *, tm):
    nc = D // LANES
    P = tok_sorted.shape[0]
    grid_spec = pltpu.PrefetchScalarGridSpec(
        num_scalar_prefetch=1,
        grid=(P // tm,),
        in_specs=[pl.BlockSpec(memory_space=pl.ANY)],
        out_specs=pl.BlockSpec((tm, D), lambda i, tok: (i, 0)),
        scratch_shapes=[pltpu.VMEM((tm * nc, LANES), F32), pltpu.SemaphoreType.DMA(())],
    )
    return pl.pallas_call(
        _gather_kernel,
        grid_spec=grid_spec,
        out_shape=jax.ShapeDtypeStruct((P, D), BF16),
        compiler_params=_params("arbitrary"),
        name="moe_gather",
    )(tok_sorted, h_slab)


def _on_valid_rows(valid, o_ref, compute):
    tm = o_ref.shape[0]
    half = tm // 2

    @pl.when(valid > half)
    def _():
        compute(slice(0, tm))

    @pl.when((valid > 0) & (valid <= half))
    def _():
        compute(slice(0, half))
        o_ref[half:, :] = jnp.zeros((tm - half, o_ref.shape[1]), o_ref.dtype)

    @pl.when(valid == 0)
    def _():
        o_ref[...] = jnp.zeros_like(o_ref)


def _moe_up_kernel(te_ref, first_ref, valid_ref, x_ref, wa_ref, wb_ref, o_ref, wa_bf, wb_bf):
    m = pl.program_id(1)

    @pl.when(first_ref[m] == 1)
    def _():
        wa_bf[...] = wa_ref[...].astype(BF16)
        wb_bf[...] = wb_ref[...].astype(BF16)

    def compute(rows):
        x = x_ref[rows, :]
        a = jnp.dot(x, wa_bf[...], preferred_element_type=F32)
        b = jnp.dot(x, wb_bf[...], preferred_element_type=F32)
        o_ref[rows, :] = (_silu(a) * b).astype(o_ref.dtype)

    _on_valid_rows(valid_ref[m], o_ref, compute)


def moe_up(xs, w_gu, tile_expert, tile_first, tile_valid, *, tm, tn):
    P, D = xs.shape
    F = w_gu.shape[2] // 2
    nb = F // tn
    grid_spec = pltpu.PrefetchScalarGridSpec(
        num_scalar_prefetch=3,
        grid=(nb, P // tm),
        in_specs=[pl.BlockSpec((tm, D), lambda n, m, te, fi, nu: (m, 0)),
                  pl.BlockSpec((None, D, tn), lambda n, m, te, fi, nu: (te[m], 0, n)),
                  pl.BlockSpec((None, D, tn), lambda n, m, te, fi, nu: (te[m], 0, n + nb))],
        out_specs=pl.BlockSpec((tm, tn), lambda n, m, te, fi, nu: (m, n)),
        scratch_shapes=[pltpu.VMEM((D, tn), BF16), pltpu.VMEM((D, tn), BF16)],
    )
    return pl.pallas_call(
        _moe_up_kernel,
        grid_spec=grid_spec,
        out_shape=jax.ShapeDtypeStruct((P, F), BF16),
        compiler_params=_params("arbitrary", "arbitrary"),
        name="moe_up",
    )(tile_expert, tile_first, tile_valid, xs, w_gu, w_gu)


def _moe_down_kernel(te_ref, first_ref, valid_ref, x_ref, w_ref, o_ref, w_bf):
    m = pl.program_id(1)

    @pl.when(first_ref[m] == 1)
    def _():
        w_bf[...] = w_ref[...].astype(BF16)

    def compute(rows):
        o_ref[rows, :] = jnp.dot(x_ref[rows, :], w_bf[...], preferred_element_type=F32)

    _on_valid_rows(valid_ref[m], o_ref, compute)


def moe_down(hm, w_down, tile_expert, tile_first, tile_valid, *, tm, tn):
    P, F = hm.shape
    D = w_down.shape[2]
    grid_spec = pltpu.PrefetchScalarGridSpec(
        num_scalar_prefetch=3,
        grid=(D // tn, P // tm),
        in_specs=[pl.BlockSpec((tm, F), lambda n, m, te, fi, nu: (m, 0)),
                  pl.BlockSpec((None, F, tn), lambda n, m, te, fi, nu: (te[m], 0, n))],
        out_specs=pl.BlockSpec((tm, tn), lambda n, m, te, fi, nu: (m, n)),
        scratch_shapes=[pltpu.VMEM((F, tn), BF16)],
    )
    return pl.pallas_call(
        _moe_down_kernel,
        grid_spec=grid_spec,
        out_shape=jax.ShapeDtypeStruct((P, D), F32),
        compiler_params=_params("arbitrary", "arbitrary"),
        name="moe_down",
    )(tile_expert, tile_first, tile_valid, hm, w_down)


def _scatter_kernel(dst_ref, nvalid_ref, y_ref, c_hbm, buf, sem):
    m = pl.program_id(0)
    tm, D = y_ref.shape
    nc = D // LANES
    base = m * tm
    n = nvalid_ref[m]

    @pl.when(n > 0)
    def _():
        for c in range(nc):
            buf[_slab_chunk(c, tm, nc), :] = y_ref[:, c * LANES:(c + 1) * LANES]

        def start_one(r, p):
            _token_copy(buf, c_hbm, r, dst_ref[base + r], nc, sem).start(priority=p)

        def start(i, carry):
            for p in range(DMA_PRIORITIES):
                start_one(i * DMA_PRIORITIES + p, p)
            return carry

        def start_tail(r, carry):
            start_one(r, 0)
            return carry

        def wait(r, carry):
            _token_copy(buf, c_hbm, r, 0, nc, sem).wait()
            return carry

        n_full = lax.div(n, DMA_PRIORITIES)
        lax.fori_loop(0, n_full, start, 0)
        lax.fori_loop(n_full * DMA_PRIORITIES, n, start_tail, 0)
        lax.fori_loop(0, n, wait, 0)


def scatter_rows(ys, dst_sorted, tile_valid, n_out, *, tm):
    P, D = ys.shape
    nc = D // LANES
    grid_spec = pltpu.PrefetchScalarGridSpec(
        num_scalar_prefetch=2,
        grid=(P // tm,),
        in_specs=[pl.BlockSpec((tm, D), lambda i, dst, nv: (i, 0))],
        out_specs=pl.BlockSpec(memory_space=pl.ANY),
        scratch_shapes=[pltpu.VMEM((tm * nc, LANES), F32), pltpu.SemaphoreType.DMA(())],
    )
    return pl.pallas_call(
        _scatter_kernel,
        grid_spec=grid_spec,
        out_shape=jax.ShapeDtypeStruct((n_out * nc, LANES), F32),
        compiler_params=_params("arbitrary"),
        name="moe_scatter",
    )(dst_sorted, tile_valid, ys)


def _combine_norm_kernel(x_ref, gate_ref, c0_ref, c1_ref, g_ref, xo_ref, hn_ref):
    gate = gate_ref[...]
    g0, g1 = gate[:, 0:1], gate[:, 1:2]
    tm, D = x_ref.shape
    nc = D // LANES
    ss = jnp.zeros((tm, 1), F32)
    for c in range(nc):
        sl = slice(c * LANES, (c + 1) * LANES)
        chunk = _slab_chunk(c, tm, nc)
        xn = x_ref[:, sl] + (g0 * c0_ref[chunk, :] + g1 * c1_ref[chunk, :])
        xo_ref[:, sl] = xn
        ss = ss + jnp.sum(xn * xn, axis=-1, keepdims=True)
    xn = xo_ref[...]
    hn_ref[...] = (xn * lax.rsqrt(ss / D + EPS) * g_ref[...]).astype(hn_ref.dtype)


def moe_combine_norm(x, gates, contrib, g, *, tm):
    T, D = x.shape
    nc = D // LANES
    nt = T // tm
    return pl.pallas_call(
        _combine_norm_kernel,
        grid=(nt,),
        in_specs=[pl.BlockSpec((tm, D), lambda i: (i, 0)),
                  pl.BlockSpec((tm, LANES), lambda i: (i, 0)),
                  pl.BlockSpec((tm * nc, LANES), lambda i: (i, 0)),
                  pl.BlockSpec((tm * nc, LANES), lambda i: (i + nt, 0)),
                  pl.BlockSpec((1, D), lambda i: (0, 0))],
        out_specs=[pl.BlockSpec((tm, D), lambda i: (i, 0)), pl.BlockSpec((tm, D), lambda i: (i, 0))],
        out_shape=[jax.ShapeDtypeStruct((T, D), F32), jax.ShapeDtypeStruct((T, D), BF16)],
        compiler_params=_params("arbitrary"),
        name="moe_combine_norm",
    )(x, gates, contrib, contrib, g.reshape(1, D))


def _route_plan(idx, tm):
    T = idx.shape[0]
    E = N_EXPERTS
    n_tiles = (TOP_K * T) // tm + E
    experts = jnp.arange(E, dtype=jnp.int32)
    flat_e = idx.reshape(-1)
    onehot = (flat_e[:, None] == experts[None, :]).astype(jnp.int32)
    csum = jnp.cumsum(onehot, axis=0)
    rank = jnp.sum((csum - onehot) * onehot, axis=1)
    counts = csum[-1]
    tiles_e = (counts + tm - 1) // tm
    tile_end = jnp.cumsum(tiles_e)
    tile_start = tile_end - tiles_e
    pos = (jnp.sum(onehot * tile_start[None, :], axis=1) * tm + rank).astype(jnp.int32)
    assign = jnp.zeros((n_tiles * tm,), jnp.int32).at[pos].set(jnp.arange(TOP_K * T, dtype=jnp.int32))
    tok_sorted = assign // TOP_K
    dst_sorted = (assign % TOP_K) * T + tok_sorted
    n_used = tile_end[-1]
    m_ids = jnp.arange(n_tiles, dtype=jnp.int32)
    te = jnp.sum((jnp.minimum(m_ids, n_used - 1)[:, None] >= tile_end[None, :]).astype(jnp.int32), axis=1)
    first = ((m_ids == 0) | (te != jnp.roll(te, 1))) & (m_ids < n_used)
    te_onehot = (te[:, None] == experts[None, :]).astype(jnp.int32)
    rows_before = (m_ids - jnp.sum(te_onehot * tile_start[None, :], axis=1)) * tm
    valid = jnp.clip(jnp.sum(te_onehot * counts[None, :], axis=1) - rows_before, 0, tm)
    valid = jnp.where(m_ids < n_used, valid, 0)
    return tok_sorted, dst_sorted, te.astype(jnp.int32), first.astype(jnp.int32), valid.astype(jnp.int32)


def moe(x, h_bf, h_slab, w_router, w_gu, w_down, g_next, *, layer, tm_tok):
    T = x.shape[0]
    E = N_EXPERTS
    w_r = jnp.pad(w_router[layer], ((0, 0), (0, LANES - E)))
    idx, gates = router(h_bf, w_r, tm=tm_tok)
    tok_sorted, dst_sorted, te, first, valid = _route_plan(idx[:, :TOP_K], MOE_TM)
    te = te + layer * E
    xs = gather_rows(h_slab, tok_sorted, x.shape[1], tm=MOE_TM)
    hm = moe_up(xs, w_gu.reshape((-1,) + w_gu.shape[2:]), te, first, valid, tm=MOE_TM, tn=1024)
    ys = moe_down(hm, w_down.reshape((-1,) + w_down.shape[2:]), te, first, valid, tm=MOE_TM, tn=512)
    contrib = scatter_rows(ys, dst_sorted, valid, TOP_K * T, tm=MOE_TM)
    return moe_combine_norm(x, gates, contrib, g_next, tm=tm_tok)


def kernel(x_prompt, x_sample, state_ret, cache_k, cache_v, p_prompt, p_sample, g_mix, g_ffn, g_ple, g_final,
           ret_w_in, ret_w_out, att_w_in, att_w_out, att_rel_bias, gmlp_w_in, gmlp_ln_g, gmlp_w_s, gmlp_b_s,
           gmlp_w_out, ffn_w_gu, ffn_w_down, moe_w_router, moe_w_gu, moe_w_down, ple_w_in, ple_w_gate):
    B, S, D = x_prompt.shape
    BS, L, _ = x_sample.shape
    depth = g_mix.shape[0]
    NP, NS = B * S, BS * L
    T = NP + NS
    assert B == 1 and S % RET_CHUNK == 0 and S % ATT_QBLOCK == 0 and L <= CHUNK
    TM = T // 8
    TMS = T // 16

    x = jnp.concatenate([x_prompt.reshape(NP, D), x_sample.reshape(NS, D)], axis=0)
    p_all = jnp.concatenate([p_prompt.reshape(depth, NP, -1), p_sample.reshape(depth, NS, -1)], axis=1)

    pos = jnp.concatenate([jnp.arange(S, dtype=jnp.int32), jnp.tile(PAST_LEN + jnp.arange(L, dtype=jnp.int32), BS)])
    freqs = ROPE_BASE ** (-jnp.arange(0, RET_DK, 2, dtype=F32) / RET_DK)
    ang = pos.astype(F32)[:, None] * freqs[None, :]
    cos, sin = jnp.cos(ang), jnp.sin(ang)
    log_g = jnp.log1p(-jnp.exp2(-5.0 - jnp.arange(RET_HEADS, dtype=F32)))

    ret_p, ret_s, k_p, v_p, k_s, v_s, gv = [], [], [], [], [], [], []
    for i in range(depth):
        kind, slot = i % 3, i // 3
        (h,) = rmsnorm(x, g_mix[i], [BF16], TMS)
        if kind == 0:
            qkvg = matmul(h, ret_w_in, layer=slot, tm=TM, tn=1024)
            y_p, st_p = retention(qkvg, cos, sin, log_g, None, batch=B, seq=S, chunk=RET_CHUNK, row0=0,
                                  heads_per_step=RET_HEADS // 2)
            y_s, st_s = retention(qkvg, cos, sin, log_g, state_ret, batch=BS, seq=L, chunk=L, row0=NP,
                                  heads_per_step=RET_HEADS, state_layer=slot)
            ret_p.append(st_p)
            ret_s.append(st_s)
            y = jnp.concatenate([y_p, y_s], axis=0)
            x = matmul(y, ret_w_out, layer=slot, tm=TMS, tn=512, res=x)
        elif kind == 1:
            HD = ATT_HEADS * ATT_DH
            W = cache_k.shape[2]
            past = BAND_PAST_CHUNKS * CHUNK
            qkv = matmul(h, att_w_in, layer=slot, tm=TM, tn=1024)
            bias_p = _rel_bias(att_rel_bias[slot], ATT_QBLOCK, past + ATT_QBLOCK, past)
            bias_s = _rel_bias(att_rel_bias[slot], L, W + L, W)
            o_p = attention_prompt(qkv, bias_p, seq=S)
            o_s = attention_sample(qkv, cache_k, cache_v, bias_s[:, :, :W], bias_s[:, :, W:],
                                   batch=BS, seq=L, row0=NP, layer=slot)
            w = min(past, S)
            k_p.append(qkv[NP - w:NP, HD:2 * HD].reshape(B, w, ATT_HEADS, ATT_DH))
            v_p.append(qkv[NP - w:NP, 2 * HD:].reshape(B, w, ATT_HEADS, ATT_DH))
            k_s.append(qkv[NP:, HD:2 * HD].reshape(BS, L, ATT_HEADS, ATT_DH))
            v_s.append(qkv[NP:, 2 * HD:].reshape(BS, L, ATT_HEADS, ATT_DH))
            o = jnp.concatenate([o_p, o_s], axis=0)
            x = matmul(o, att_w_out, layer=slot, tm=TM, tn=1024, res=x)
        else:
            tile = GMLP_CHUNK
            uv = matmul(h, gmlp_w_in, layer=slot, tm=TM, tn=1024)
            r = jnp.arange(tile)
            ws_p = jnp.where((r[:, None] >= r[None, :])[None], gmlp_w_s[slot][:, :tile, :tile], 0.0)
            same = (r[:, None] // L == r[None, :] // L) & (r[:, None] >= r[None, :])
            ws_s = jnp.where(same[None], jnp.tile(gmlp_w_s[slot][:, :L, :L], (1, tile // L, tile // L)), 0.0)
            bs_p = gmlp_b_s[slot][:, :tile]
            bs_s = jnp.tile(gmlp_b_s[slot][:, :L], (1, tile // L))
            ws = jnp.stack([ws_p, ws_s]).astype(BF16)
            bs = jnp.stack([bs_p, bs_s])[..., None]
            yg, vn = gmlp_gate(uv, gmlp_ln_g[slot], ws, bs, n_prompt_tiles=NP // tile, tile=tile)
            gv.append((vn[:tile].reshape(B, tile, -1), vn[tile:].reshape(BS, L, -1)))
            x = matmul(yg, gmlp_w_out, layer=slot, tm=TMS, tn=512, res=x)

        if i % 2 == 0:
            (h,) = rmsnorm(x, g_ffn[i], [BF16], TMS)
            x = dense_ffn(x, h, ffn_w_gu[i // 2], ffn_w_down[i // 2], tm_up=TM, tm_down=TMS)
            (hn,) = rmsnorm(x, g_ple[i], [BF16], TMS)
        else:
            h, h_slab = rmsnorm(x, g_ffn[i], [BF16], TMS, slab_f32=True)
            x, hn = moe(x, h, h_slab, moe_w_router, moe_w_gu, moe_w_down, g_ple[i], layer=i // 2, tm_tok=TMS)

        x = ple(x, hn, p_all, ple_w_gate, ple_w_in, layer=i, tm=TMS, tn=1024)

    (y,) = rmsnorm(x, g_final, [F32], TMS)
    y_prompt = y[:NP].reshape(B, S, D)
    y_sample = y[NP:].reshape(BS, L, D)
    return (y_prompt, y_sample, jnp.stack(ret_p), jnp.stack(ret_s), jnp.stack(k_p), jnp.stack(v_p),
            jnp.stack(k_s), jnp.stack(v_s), jnp.stack([a for a, _ in gv]), jnp.stack([b for _, b in gv]))
```

```python
import functools

import jax
import jax.numpy as jnp
from jax import lax
from jax.experimental import pallas as pl
from jax.experimental.pallas import tpu as pltpu

F32 = jnp.float32
BF16 = jnp.bfloat16

EPS = 1e-6
NEG_INF = -1e30
CHUNK = 64
ROPE_BASE = 10000.0
RET_HEADS, RET_DK, RET_DV = 8, 256, 512
ATT_HEADS, ATT_DH = 16, 128
BAND_PAST_CHUNKS = 8
REL_CLIP = 128
GMLP_CHUNK, GMLP_GROUPS = 128, 8
N_EXPERTS, TOP_K = 8, 2
PAST_LEN = 2048

LANES = 128
DMA_PRIORITIES = 2
V7X_VMEM_LIMIT_BYTES = 60 * 1000 * 1024

RET_CHUNK = 256
ATT_QBLOCK = 2 * CHUNK
MOE_TM = 512
GATHER_SRC_BLOCK = 512


def _params(*sem):
    return pltpu.CompilerParams(dimension_semantics=sem, vmem_limit_bytes=V7X_VMEM_LIMIT_BYTES)


def _silu(x):
    return x * jax.nn.sigmoid(x)


def _slab_chunk(c, n_tokens, nc):
    return pl.ds(c, n_tokens, stride=nc)


def _rmsnorm_kernel(x_ref, g_ref, *o_refs):
    x = x_ref[...]
    y = x * lax.rsqrt(jnp.mean(x * x, axis=-1, keepdims=True) + EPS) * g_ref[...]
    for o_ref in o_refs:
        o_ref[...] = y.astype(o_ref.dtype)


def rmsnorm(x, g, dtypes, tm):
    T, D = x.shape
    out_specs = [pl.BlockSpec((tm, D), lambda i: (i, 0)) for _ in dtypes]
    out_shape = [jax.ShapeDtypeStruct((T, D), dt) for dt in dtypes]
    return pl.pallas_call(
        _rmsnorm_kernel,
        grid=(T // tm,),
        in_specs=[pl.BlockSpec((tm, D), lambda i: (i, 0)), pl.BlockSpec((1, D), lambda i: (0, 0))],
        out_specs=out_specs,
        out_shape=out_shape,
        compiler_params=_params("arbitrary"),
        name="rmsnorm",
    )(x, g.reshape(1, D))


def _mm_kernel(a_ref, w_ref, *rest, has_res, cast_w):
    if has_res:
        r_ref, o_ref, *scratch = rest
    else:
        o_ref, *scratch = rest
    if cast_w:
        w_bf = scratch[0]

        @pl.when(pl.program_id(1) == 0)
        def _():
            w_bf[...] = w_ref[...].astype(BF16)

        w = w_bf[...]
    else:
        w = w_ref[...]
    acc = jnp.dot(a_ref[...], w, preferred_element_type=F32)
    if has_res:
        acc = r_ref[...] + acc
    o_ref[...] = acc.astype(o_ref.dtype)


def _weight_spec(w, layer, tn):
    K = w.shape[-2]
    if w.ndim == 2:
        return pl.BlockSpec((K, tn), lambda n, m: (0, n))
    return pl.BlockSpec((None, K, tn), lambda n, m: (layer, 0, n))


def matmul(a, w, *, tm, tn, layer=0, out_dtype=F32, res=None):
    M, K = a.shape
    N = w.shape[-1]
    cast_w = w.dtype != BF16
    in_specs = [pl.BlockSpec((tm, K), lambda n, m: (m, 0)), _weight_spec(w, layer, tn)]
    args = [a, w]
    if res is not None:
        in_specs.append(pl.BlockSpec((tm, tn), lambda n, m: (m, n)))
        args.append(res)
    return pl.pallas_call(
        functools.partial(_mm_kernel, has_res=res is not None, cast_w=cast_w),
        grid=(N // tn, M // tm),
        in_specs=in_specs,
        out_specs=pl.BlockSpec((tm, tn), lambda n, m: (m, n)),
        out_shape=jax.ShapeDtypeStruct((M, N), out_dtype),
        scratch_shapes=[pltpu.VMEM((K, tn), BF16)] if cast_w else [],
        compiler_params=_params("arbitrary", "arbitrary"),
        name="matmul",
    )(*args)


def _swiglu_up_kernel(x_ref, wa_ref, wb_ref, o_ref):
    x = x_ref[...]
    a = jnp.dot(x, wa_ref[...], preferred_element_type=F32)
    b = jnp.dot(x, wb_ref[...], preferred_element_type=F32)
    o_ref[...] = (_silu(a) * b).astype(o_ref.dtype)


def swiglu_up(x, w_gu, *, tm, tn):
    T, D = x.shape
    F = w_gu.shape[1] // 2
    nb = F // tn
    return pl.pallas_call(
        _swiglu_up_kernel,
        grid=(nb, T // tm),
        in_specs=[pl.BlockSpec((tm, D), lambda n, m: (m, 0)),
                  pl.BlockSpec((D, tn), lambda n, m: (0, n)),
                  pl.BlockSpec((D, tn), lambda n, m: (0, n + nb))],
        out_specs=pl.BlockSpec((tm, tn), lambda n, m: (m, n)),
        out_shape=jax.ShapeDtypeStruct((T, F), BF16),
        compiler_params=_params("arbitrary", "arbitrary"),
        name="swiglu_up",
    )(x, w_gu, w_gu)


def dense_ffn(x, h, w_gu, w_down, *, tm_up, tm_down):
    tn = 512
    F = w_gu.shape[1] // 2
    FP = -(-F // tn) * tn
    w_gu_p = jnp.concatenate([_pad_cols(w_gu[:, :F], FP), _pad_cols(w_gu[:, F:], FP)], axis=1).astype(BF16)
    w_dn_p = jnp.pad(w_down, ((0, FP - F), (0, 0))).astype(BF16)
    hm = swiglu_up(h, w_gu_p, tm=tm_up, tn=tn)
    return matmul(hm, w_dn_p, tm=tm_down, tn=tn, res=x)


def _pad_cols(w, n):
    return jnp.pad(w, ((0, 0), (0, n - w.shape[1])))


def _ple_kernel(hn_ref, wg_ref, p_ref, wi_ref, x_ref, o_ref, wg_bf, wi_bf):
    @pl.when(pl.program_id(1) == 0)
    def _():
        wg_bf[...] = wg_ref[...].astype(BF16)
        wi_bf[...] = wi_ref[...].astype(BF16)

    gate = jax.nn.sigmoid(jnp.dot(hn_ref[...], wg_bf[...], preferred_element_type=F32))
    emb = jnp.dot(p_ref[...].astype(BF16), wi_bf[...], preferred_element_type=F32)
    o_ref[...] = x_ref[...] + gate * emb


def ple(x, hn, p, w_gate, w_in, *, layer, tm, tn):
    T, D = x.shape
    DP = p.shape[2]
    return pl.pallas_call(
        _ple_kernel,
        grid=(D // tn, T // tm),
        in_specs=[pl.BlockSpec((tm, D), lambda n, m: (m, 0)),
                  _weight_spec(w_gate, layer, tn),
                  pl.BlockSpec((None, tm, DP), lambda n, m: (layer, m, 0)),
                  _weight_spec(w_in, layer, tn),
                  pl.BlockSpec((tm, tn), lambda n, m: (m, n))],
        out_specs=pl.BlockSpec((tm, tn), lambda n, m: (m, n)),
        out_shape=jax.ShapeDtypeStruct((T, D), F32),
        scratch_shapes=[pltpu.VMEM((D, tn), BF16), pltpu.VMEM((DP, tn), BF16)],
        compiler_params=_params("arbitrary", "arbitrary"),
        name="ple",
    )(hn, w_gate, p, w_in, x)


def _retention_kernel(lg_ref, q_ref, k_ref, v_ref, g_ref, cos_ref, sin_ref, *rest, has_state0):
    if has_state0:
        s0_ref, y_ref, sout_ref, state = rest
    else:
        y_ref, sout_ref, state = rest
    hb = pl.program_id(1)
    c = pl.program_id(2)
    C = q_ref.shape[0]
    hp = state.shape[0]
    dk, dv = RET_DK, RET_DV
    half = dk // 2

    @pl.when(c == 0)
    def _():
        if has_state0:
            state[...] = s0_ref[...]
        else:
            state[...] = jnp.zeros_like(state)

    cos = cos_ref[...]
    sin = sin_ref[...]

    def rotary(x):
        x1, x2 = x[:, :half], x[:, half:]
        return jnp.concatenate([x1 * cos - x2 * sin, x1 * sin + x2 * cos], axis=-1)

    row = lax.broadcasted_iota(jnp.int32, (C, C), 0)
    col = lax.broadcasted_iota(jnp.int32, (C, C), 1)
    rel = (row - col).astype(F32)
    rel_pos = jnp.maximum(rel, 0.0)
    idx = lax.broadcasted_iota(jnp.int32, (C, 1), 0).astype(F32)

    for j in range(hp):
        lg = lg_ref[hb * hp + j]
        q = rotary(q_ref[:, j * dk:(j + 1) * dk])
        k = rotary(k_ref[:, j * dk:(j + 1) * dk]) * dk ** -0.5
        vb = v_ref[:, j * dv:(j + 1) * dv].astype(BF16)
        decay = jnp.where(rel >= 0, jnp.exp(rel_pos * lg), 0.0)
        q_decay = jnp.exp((idx + 1.0) * lg)
        k_decay = jnp.exp((C - 1.0 - idx) * lg)
        chunk_decay = jnp.exp(jnp.zeros((1, 1), F32) + C * lg)

        qb = q.astype(BF16)
        scores = lax.dot_general(qb, k.astype(BF16), (((1,), (1,)), ((), ())), preferred_element_type=F32) * decay
        o = jnp.dot(scores.astype(BF16), vb, preferred_element_type=F32)
        st = state[j]
        o = o + jnp.dot(qb, st.astype(BF16), preferred_element_type=F32) * q_decay
        kv = lax.dot_general((k * k_decay).astype(BF16), vb, (((0,), (0,)), ((), ())), preferred_element_type=F32)
        st = st * chunk_decay + kv
        state[j] = st
        sout_ref[j] = st

        o = o * lax.rsqrt(jnp.mean(o * o, axis=-1, keepdims=True) + EPS)
        y_ref[:, j * dv:(j + 1) * dv] = (o * _silu(g_ref[:, j * dv:(j + 1) * dv])).astype(y_ref.dtype)


def retention(qkvg, cos, sin, log_g, state0, *, batch, seq, chunk, row0, heads_per_step, state_layer=0):
    H, dk, dv = RET_HEADS, RET_DK, RET_DV
    hp = heads_per_step
    nhb = H // hp
    nc = seq // chunk
    rb0 = row0 // chunk
    kq = nhb
    kv = (2 * H * dk) // (hp * dv)
    kg = kv + nhb

    def rows(b, c):
        return rb0 + b * nc + c

    in_specs = [
        pl.BlockSpec((chunk, hp * dk), lambda b, h, c, lg: (rows(b, c), h)),
        pl.BlockSpec((chunk, hp * dk), lambda b, h, c, lg: (rows(b, c), kq + h)),
        pl.BlockSpec((chunk, hp * dv), lambda b, h, c, lg: (rows(b, c), kv + h)),
        pl.BlockSpec((chunk, hp * dv), lambda b, h, c, lg: (rows(b, c), kg + h)),
        pl.BlockSpec((chunk, dk // 2), lambda b, h, c, lg: (rows(b, c), 0)),
        pl.BlockSpec((chunk, dk // 2), lambda b, h, c, lg: (rows(b, c), 0)),
    ]
    args = [qkvg, qkvg, qkvg, qkvg, cos, sin]
    if state0 is not None:
        in_specs.append(pl.BlockSpec((None, None, hp, dk, dv), lambda b, h, c, lg: (state_layer, b, h, 0, 0)))
        args.append(state0)
    grid_spec = pltpu.PrefetchScalarGridSpec(
        num_scalar_prefetch=1,
        grid=(batch, nhb, nc),
        in_specs=in_specs,
        out_specs=[pl.BlockSpec((chunk, hp * dv), lambda b, h, c, lg: (b * nc + c, h)),
                   pl.BlockSpec((None, hp, dk, dv), lambda b, h, c, lg: (b, h, 0, 0))],
        scratch_shapes=[pltpu.VMEM((hp, dk, dv), F32)],
    )
    return pl.pallas_call(
        functools.partial(_retention_kernel, has_state0=state0 is not None),
        grid_spec=grid_spec,
        out_shape=[jax.ShapeDtypeStruct((batch * seq, H * dv), BF16),
                   jax.ShapeDtypeStruct((batch, H, dk, dv), F32)],
        compiler_params=_params("arbitrary", "arbitrary", "arbitrary"),
        name="retention",
    )(log_g, *args)


def _att_prompt_kernel(q_ref, k_ref, v_ref, bias_ref, o_ref, kpad, vpad):
    S = q_ref.shape[0]
    QB = ATT_QBLOCK
    past = BAND_PAST_CHUNKS * CHUNK
    win = past + QB
    kpad[:past, :] = jnp.zeros((past, ATT_DH), BF16)
    vpad[:past, :] = jnp.zeros((past, ATT_DH), BF16)
    kpad[past:, :] = k_ref[...].astype(BF16)
    vpad[past:, :] = v_ref[...].astype(BF16)
    bias = bias_ref[...]
    qi = lax.broadcasted_iota(jnp.int32, (QB, win), 0)
    kj = lax.broadcasted_iota(jnp.int32, (QB, win), 1)
    chunk_shift = CHUNK.bit_length() - 1
    qc = jnp.right_shift(qi, chunk_shift)
    kc = jnp.right_shift(kj, chunk_shift)
    band = (kc >= qc) & (kc <= qc + BAND_PAST_CHUNKS)

    def body(i, carry):
        q0 = pl.multiple_of(i * QB, QB)
        qb = q_ref[pl.ds(q0, QB), :].astype(BF16)
        kw = kpad[pl.ds(q0, win), :]
        vw = vpad[pl.ds(q0, win), :]
        s = lax.dot_general(qb, kw, (((1,), (1,)), ((), ())), preferred_element_type=F32)
        s = s * ATT_DH ** -0.5 + bias
        valid = band & (kj >= past - q0)
        s = jnp.where(valid, s, NEG_INF)
        m = jnp.max(s, axis=-1, keepdims=True)
        e = jnp.exp(s - m)
        pr = (e / jnp.sum(e, axis=-1, keepdims=True)).astype(BF16)
        o_ref[pl.ds(q0, QB), :] = jnp.dot(pr, vw, preferred_element_type=F32).astype(o_ref.dtype)
        return carry

    lax.fori_loop(0, S // QB, body, 0, unroll=4)


def attention_prompt(qkv, bias, *, seq):
    H, dh = ATT_HEADS, ATT_DH
    past = BAND_PAST_CHUNKS * CHUNK
    return pl.pallas_call(
        _att_prompt_kernel,
        grid=(H,),
        in_specs=[pl.BlockSpec((seq, dh), lambda h: (0, h)),
                  pl.BlockSpec((seq, dh), lambda h: (0, H + h)),
                  pl.BlockSpec((seq, dh), lambda h: (0, 2 * H + h)),
                  pl.BlockSpec((None, ATT_QBLOCK, past + ATT_QBLOCK), lambda h: (h, 0, 0))],
        out_specs=pl.BlockSpec((seq, dh), lambda h: (0, h)),
        out_shape=jax.ShapeDtypeStruct((seq, H * dh), BF16),
        scratch_shapes=[pltpu.VMEM((past + seq, dh), BF16), pltpu.VMEM((past + seq, dh), BF16)],
        compiler_params=_params("arbitrary"),
        name="attention_prompt",
    )(qkv, qkv, qkv, bias)


def _att_sample_kernel(q_ref, kn_ref, vn_ref, ck_ref, cv_ref, bp_ref, bn_ref, o_ref):
    dh = ATT_DH
    H = ATT_HEADS
    W = ck_ref.shape[0] // H
    for h in range(H):
        sl = slice(h * dh, (h + 1) * dh)
        qb = q_ref[:, sl].astype(BF16)
        kn = kn_ref[:, sl].astype(BF16)
        vn = vn_ref[:, sl].astype(BF16)
        kp = ck_ref[pl.ds(h, W, stride=H), :].astype(BF16)
        vp = cv_ref[pl.ds(h, W, stride=H), :].astype(BF16)
        sp = lax.dot_general(qb, kp, (((1,), (1,)), ((), ())), preferred_element_type=F32)
        sn = lax.dot_general(qb, kn, (((1,), (1,)), ((), ())), preferred_element_type=F32)
        sp = sp * dh ** -0.5 + bp_ref[h]
        sn = sn * dh ** -0.5 + bn_ref[h]
        m = jnp.maximum(jnp.max(sp, axis=-1, keepdims=True), jnp.max(sn, axis=-1, keepdims=True))
        ep = jnp.exp(sp - m)
        en = jnp.exp(sn - m)
        den = jnp.sum(ep, axis=-1, keepdims=True) + jnp.sum(en, axis=-1, keepdims=True)
        o = jnp.dot((ep / den).astype(BF16), vp, preferred_element_type=F32)
        o = o + jnp.dot((en / den).astype(BF16), vn, preferred_element_type=F32)
        o_ref[:, sl] = o.astype(o_ref.dtype)


def attention_sample(qkv, cache_k, cache_v, bias_past, bias_new, *, batch, seq, row0, layer):
    H, dh = ATT_HEADS, ATT_DH
    W = cache_k.shape[2]
    rb0 = row0 // seq
    cache_k = cache_k.reshape(-1, dh)
    cache_v = cache_v.reshape(-1, dh)
    return pl.pallas_call(
        _att_sample_kernel,
        grid=(batch,),
        in_specs=[pl.BlockSpec((seq, H * dh), lambda b: (rb0 + b, 0)),
                  pl.BlockSpec((seq, H * dh), lambda b: (rb0 + b, 1)),
                  pl.BlockSpec((seq, H * dh), lambda b: (rb0 + b, 2)),
                  pl.BlockSpec((W * H, dh), lambda b: (layer * batch + b, 0)),
                  pl.BlockSpec((W * H, dh), lambda b: (layer * batch + b, 0)),
                  pl.BlockSpec((H, seq, W), lambda b: (0, 0, 0)),
                  pl.BlockSpec((H, seq, seq), lambda b: (0, 0, 0))],
        out_specs=pl.BlockSpec((seq, H * dh), lambda b: (b, 0)),
        out_shape=jax.ShapeDtypeStruct((batch * seq, H * dh), BF16),
        compiler_params=_params("arbitrary"),
        name="attention_sample",
    )(qkv, qkv, qkv, cache_k, cache_v, bias_past, bias_new)


def _rel_bias(table, n_q, n_k, offset):
    period = n_q + n_k
    d = jnp.arange(period)
    d = jnp.where(d < n_k, d, d - period)
    u = jnp.take(table, jnp.clip(offset - d, -REL_CLIP, REL_CLIP) + REL_CLIP, axis=0).T.astype(F32)
    H = u.shape[0]
    skew = jnp.tile(u, (1, n_q))[:, :n_q * (period - 1)].reshape(H, n_q, period - 1)
    return skew[:, :, :n_k]


def _gmlp_kernel(uv_ref, lng_ref, ws_ref, bs_ref, y_ref, vn_ref):
    DV = y_ref.shape[1]
    cg = DV // GMLP_GROUPS
    v = jax.nn.gelu(uv_ref[:, DV:])
    mu = jnp.mean(v, axis=-1, keepdims=True)
    d = v - mu
    var = jnp.mean(d * d, axis=-1, keepdims=True)
    vn = d * lax.rsqrt(var + EPS) * lng_ref[...]
    vn_ref[...] = vn
    for g in range(GMLP_GROUPS):
        sl = slice(g * cg, (g + 1) * cg)
        mixed = jnp.dot(ws_ref[g], vn[:, sl].astype(BF16), preferred_element_type=F32) + bs_ref[g]
        y_ref[:, sl] = (jax.nn.gelu(uv_ref[:, sl]) * mixed).astype(y_ref.dtype)


def gmlp_gate(uv, ln_g, ws, bs, *, n_prompt_tiles, tile):
    T = uv.shape[0]
    DV = uv.shape[1] // 2
    G = GMLP_GROUPS
    nt = T // tile
    last = n_prompt_tiles - 1
    return pl.pallas_call(
        _gmlp_kernel,
        grid=(nt,),
        in_specs=[pl.BlockSpec((tile, 2 * DV), lambda i: (i, 0)),
                  pl.BlockSpec((1, DV), lambda i: (0, 0)),
                  pl.BlockSpec((None, G, tile, tile), lambda i: (i // n_prompt_tiles, 0, 0, 0)),
                  pl.BlockSpec((None, G, tile, 1), lambda i: (i // n_prompt_tiles, 0, 0, 0))],
        out_specs=[pl.BlockSpec((tile, DV), lambda i: (i, 0)),
                   pl.BlockSpec((tile, DV), lambda i: (jnp.maximum(i - last, 0), 0))],
        out_shape=[jax.ShapeDtypeStruct((T, DV), BF16),
                   jax.ShapeDtypeStruct(((nt - last) * tile, DV), F32)],
        compiler_params=_params("arbitrary"),
        name="gmlp_gate",
    )(uv, ln_g.reshape(1, DV), ws, bs)


def _router_kernel(h_ref, w_ref, idx_ref, gate_ref):
    logits = jnp.dot(h_ref[...], w_ref[...].astype(BF16), preferred_element_type=F32)
    lane = lax.broadcasted_iota(jnp.int32, logits.shape, 1)
    lane_f = lane.astype(F32)
    l1 = jnp.where(lane < N_EXPERTS, logits, -jnp.inf)
    v1 = jnp.max(l1, axis=-1, keepdims=True)
    i1 = jnp.min(jnp.where(l1 == v1, lane_f, float(LANES)), axis=-1, keepdims=True)
    l2 = jnp.where(lane_f == i1, -jnp.inf, l1)
    v2 = jnp.max(l2, axis=-1, keepdims=True)
    i2 = jnp.min(jnp.where(l2 == v2, lane_f, float(LANES)), axis=-1, keepdims=True)
    e2 = jnp.exp(v2 - v1)
    den = 1.0 + e2
    idx_ref[...] = jnp.where(lane == 0, i1, jnp.where(lane == 1, i2, 0.0)).astype(jnp.int32)
    gate_ref[...] = jnp.where(lane == 0, 1.0 / den, jnp.where(lane == 1, e2 / den, 0.0))


def router(h, w_router_padded, *, tm):
    T, D = h.shape
    return pl.pallas_call(
        _router_kernel,
        grid=(T // tm,),
        in_specs=[pl.BlockSpec((tm, D), lambda i: (i, 0)), pl.BlockSpec((D, LANES), lambda i: (0, 0))],
        out_specs=[pl.BlockSpec((tm, LANES), lambda i: (i, 0)), pl.BlockSpec((tm, LANES), lambda i: (i, 0))],
        out_shape=[jax.ShapeDtypeStruct((T, LANES), jnp.int32), jax.ShapeDtypeStruct((T, LANES), F32)],
        compiler_params=_params("arbitrary"),
        name="router",
    )(h, w_router_padded)


def _token_copy(src_ref, dst_ref, src_tok, dst_tok, nc, sem):
    src = src_ref.at[pl.ds(pl.multiple_of(src_tok * nc, nc), nc), :]
    dst = dst_ref.at[pl.ds(pl.multiple_of(dst_tok * nc, nc), nc), :]
    return pltpu.make_async_copy(src, dst, sem)


def _gather_kernel(lo_ref, hi_ref, tok_ref, h_ref, o_ref, acc):
    m = pl.program_id(0)
    tm = o_ref.shape[0]
    sb = GATHER_SRC_BLOCK
    tok = tok_ref[...]
    col = lax.broadcasted_iota(jnp.int32, (tm, sb), 1)
    acc[...] = jnp.zeros_like(acc)

    def body(b, carry):
        base = pl.multiple_of(b * sb, sb)
        select = jnp.where(tok - base == col, 1.0, 0.0).astype(h_ref.dtype)
        acc[...] += jnp.dot(select, h_ref[pl.ds(base, sb), :], preferred_element_type=F32)
        return carry

    lax.fori_loop(lo_ref[m], hi_ref[m] + 1, body, 0)
    o_ref[...] = acc[...].astype(o_ref.dtype)


def gather_rows(h, tok_sorted, blk_lo, blk_hi, *, tm):
    T, D = h.shape
    P = tok_sorted.shape[0]
    grid_spec = pltpu.PrefetchScalarGridSpec(
        num_scalar_prefetch=2,
        grid=(P // tm,),
        in_specs=[pl.BlockSpec((tm, 1), lambda i, lo, hi: (i, 0)),
                  pl.BlockSpec((T, D), lambda i, lo, hi: (0, 0), pipeline_mode=pl.Buffered(1))],
        out_specs=pl.BlockSpec((tm, D), lambda i, lo, hi: (i, 0)),
        scratch_shapes=[pltpu.VMEM((tm, D), F32)],
    )
    return pl.pallas_call(
        _gather_kernel,
        grid_spec=grid_spec,
        out_shape=jax.ShapeDtypeStruct((P, D), h.dtype),
        compiler_params=_params("arbitrary"),
        name="moe_gather",
    )(blk_lo, blk_hi, tok_sorted.reshape(P, 1), h)


def _on_valid_rows(valid, o_ref, compute):
    tm = o_ref.shape[0]
    half = tm // 2

    @pl.when(valid > half)
    def _():
        compute(slice(0, tm))

    @pl.when((valid > 0) & (valid <= half))
    def _():
        compute(slice(0, half))
        o_ref[half:, :] = jnp.zeros((tm - half, o_ref.shape[1]), o_ref.dtype)

    @pl.when(valid == 0)
    def _():
        o_ref[...] = jnp.zeros_like(o_ref)


def _group_weights(te_ref, first_ref, nxt_ref, wrap_ref, copies, cast):
    n = pl.program_id(0)
    m = pl.program_id(1)

    @pl.when(first_ref[m] == 1)
    def _():
        @pl.when((n == 0) & (m == 0))
        def _():
            for cp in copies(te_ref[0], 0):
                cp.start()

        for cp in copies(te_ref[m], n):
            cp.wait()
        cast()
        n_next = n + wrap_ref[m]

        @pl.when(n_next < pl.num_programs(0))
        def _():
            for cp in copies(nxt_ref[m], n_next):
                cp.start()


def _moe_up_kernel(te_ref, first_ref, valid_ref, nxt_ref, wrap_ref, x_ref, w_hbm, o_ref,
                   stage_a, stage_b, wa_bf, wb_bf, sems):
    m = pl.program_id(1)
    tn = wa_bf.shape[1]
    F = w_hbm.shape[2] // 2

    def copies(e, n):
        col = pl.multiple_of(n * tn, tn)
        return (pltpu.make_async_copy(w_hbm.at[e, :, pl.ds(col, tn)], stage_a, sems.at[0]),
                pltpu.make_async_copy(w_hbm.at[e, :, pl.ds(pl.multiple_of(F + col, tn), tn)], stage_b, sems.at[1]))

    def cast():
        wa_bf[...] = stage_a[...].astype(BF16)
        wb_bf[...] = stage_b[...].astype(BF16)

    _group_weights(te_ref, first_ref, nxt_ref, wrap_ref, copies, cast)

    def compute(rows):
        x = x_ref[rows, :]
        a = jnp.dot(x, wa_bf[...], preferred_element_type=F32)
        b = jnp.dot(x, wb_bf[...], preferred_element_type=F32)
        o_ref[rows, :] = (_silu(a) * b).astype(o_ref.dtype)

    _on_valid_rows(valid_ref[m], o_ref, compute)


def moe_up(xs, w_gu, plan, *, tm, tn):
    P, D = xs.shape
    F = w_gu.shape[2] // 2
    grid_spec = pltpu.PrefetchScalarGridSpec(
        num_scalar_prefetch=len(plan),
        grid=(F // tn, P // tm),
        in_specs=[pl.BlockSpec((tm, D), lambda n, m, *_: (m, 0)),
                  pl.BlockSpec(memory_space=pl.ANY)],
        out_specs=pl.BlockSpec((tm, tn), lambda n, m, *_: (m, n)),
        scratch_shapes=[pltpu.VMEM((D, tn), w_gu.dtype), pltpu.VMEM((D, tn), w_gu.dtype),
                        pltpu.VMEM((D, tn), BF16), pltpu.VMEM((D, tn), BF16), pltpu.SemaphoreType.DMA((2,))],
    )
    return pl.pallas_call(
        _moe_up_kernel,
        grid_spec=grid_spec,
        out_shape=jax.ShapeDtypeStruct((P, F), BF16),
        compiler_params=_params("arbitrary", "arbitrary"),
        name="moe_up",
    )(*plan, xs, w_gu)


def _moe_down_kernel(te_ref, first_ref, valid_ref, nxt_ref, wrap_ref, x_ref, w_hbm, o_ref, stage, w_bf, sem):
    m = pl.program_id(1)
    tn = w_bf.shape[1]

    def copies(e, n):
        return (pltpu.make_async_copy(w_hbm.at[e, :, pl.ds(pl.multiple_of(n * tn, tn), tn)], stage, sem),)

    def cast():
        w_bf[...] = stage[...].astype(BF16)

    _group_weights(te_ref, first_ref, nxt_ref, wrap_ref, copies, cast)

    def compute(rows):
        o_ref[rows, :] = jnp.dot(x_ref[rows, :], w_bf[...], preferred_element_type=F32)

    _on_valid_rows(valid_ref[m], o_ref, compute)


def moe_down(hm, w_down, plan, *, tm, tn):
    P, F = hm.shape
    D = w_down.shape[2]
    grid_spec = pltpu.PrefetchScalarGridSpec(
        num_scalar_prefetch=len(plan),
        grid=(D // tn, P // tm),
        in_specs=[pl.BlockSpec((tm, F), lambda n, m, *_: (m, 0)),
                  pl.BlockSpec(memory_space=pl.ANY)],
        out_specs=pl.BlockSpec((tm, tn), lambda n, m, *_: (m, n)),
        scratch_shapes=[pltpu.VMEM((F, tn), w_down.dtype), pltpu.VMEM((F, tn), BF16), pltpu.SemaphoreType.DMA(())],
    )
    return pl.pallas_call(
        _moe_down_kernel,
        grid_spec=grid_spec,
        out_shape=jax.ShapeDtypeStruct((P, D), F32),
        compiler_params=_params("arbitrary", "arbitrary"),
        name="moe_down",
    )(*plan, hm, w_down)


def _scatter_kernel(dst_ref, nvalid_ref, y_ref, c_hbm, buf, sem):
    m = pl.program_id(0)
    tm, D = y_ref.shape
    nc = D // LANES
    base = m * tm
    n = nvalid_ref[m]

    @pl.when(n > 0)
    def _():
        for c in range(nc):
            buf[_slab_chunk(c, tm, nc), :] = y_ref[:, c * LANES:(c + 1) * LANES]

        def start_one(r, p):
            _token_copy(buf, c_hbm, r, dst_ref[base + r], nc, sem).start(priority=p)

        def start(i, carry):
            for p in range(DMA_PRIORITIES):
                start_one(i * DMA_PRIORITIES + p, p)
            return carry

        def start_tail(r, carry):
            start_one(r, 0)
            return carry

        def wait(r, carry):
            _token_copy(buf, c_hbm, r, 0, nc, sem).wait()
            return carry

        n_full = lax.div(n, DMA_PRIORITIES)
        lax.fori_loop(0, n_full, start, 0)
        lax.fori_loop(n_full * DMA_PRIORITIES, n, start_tail, 0)
        lax.fori_loop(0, n, wait, 0)


def scatter_rows(ys, dst_sorted, tile_valid, n_out, *, tm):
    P, D = ys.shape
    nc = D // LANES
    grid_spec = pltpu.PrefetchScalarGridSpec(
        num_scalar_prefetch=2,
        grid=(P // tm,),
        in_specs=[pl.BlockSpec((tm, D), lambda i, dst, nv: (i, 0))],
        out_specs=pl.BlockSpec(memory_space=pl.ANY),
        scratch_shapes=[pltpu.VMEM((tm * nc, LANES), F32), pltpu.SemaphoreType.DMA(())],
    )
    return pl.pallas_call(
        _scatter_kernel,
        grid_spec=grid_spec,
        out_shape=jax.ShapeDtypeStruct((n_out * nc, LANES), F32),
        compiler_params=_params("arbitrary"),
        name="moe_scatter",
    )(dst_sorted, tile_valid, ys)


def _combine_norm_kernel(x_ref, gate_ref, c0_ref, c1_ref, g_ref, xo_ref, hn_ref):
    gate = gate_ref[...]
    g0, g1 = gate[:, 0:1], gate[:, 1:2]
    tm, D = x_ref.shape
    nc = D // LANES
    ss = jnp.zeros((tm, 1), F32)
    for c in range(nc):
        sl = slice(c * LANES, (c + 1) * LANES)
        chunk = _slab_chunk(c, tm, nc)
        xn = x_ref[:, sl] + (g0 * c0_ref[chunk, :] + g1 * c1_ref[chunk, :])
        xo_ref[:, sl] = xn
        ss = ss + jnp.sum(xn * xn, axis=-1, keepdims=True)
    xn = xo_ref[...]
    hn_ref[...] = (xn * lax.rsqrt(ss / D + EPS) * g_ref[...]).astype(hn_ref.dtype)


def moe_combine_norm(x, gates, contrib, g, *, tm):
    T, D = x.shape
    nc = D // LANES
    nt = T // tm
    return pl.pallas_call(
        _combine_norm_kernel,
        grid=(nt,),
        in_specs=[pl.BlockSpec((tm, D), lambda i: (i, 0)),
                  pl.BlockSpec((tm, LANES), lambda i: (i, 0)),
                  pl.BlockSpec((tm * nc, LANES), lambda i: (i, 0)),
                  pl.BlockSpec((tm * nc, LANES), lambda i: (i + nt, 0)),
                  pl.BlockSpec((1, D), lambda i: (0, 0))],
        out_specs=[pl.BlockSpec((tm, D), lambda i: (i, 0)), pl.BlockSpec((tm, D), lambda i: (i, 0))],
        out_shape=[jax.ShapeDtypeStruct((T, D), F32), jax.ShapeDtypeStruct((T, D), BF16)],
        compiler_params=_params("arbitrary"),
        name="moe_combine_norm",
    )(x, gates, contrib, contrib, g.reshape(1, D))


def _route_plan(idx, tm):
    T = idx.shape[0]
    E = N_EXPERTS
    n_tiles = (TOP_K * T) // tm + E
    experts = jnp.arange(E, dtype=jnp.int32)
    flat_e = idx.reshape(-1)
    onehot = (flat_e[:, None] == experts[None, :]).astype(jnp.int32)
    csum = jnp.cumsum(onehot, axis=0)
    rank = jnp.sum((csum - onehot) * onehot, axis=1)
    counts = csum[-1]
    tiles_e = (counts + tm - 1) // tm
    tile_end = jnp.cumsum(tiles_e)
    tile_start = tile_end - tiles_e
    pos = (jnp.sum(onehot * tile_start[None, :], axis=1) * tm + rank).astype(jnp.int32)
    assign = jnp.zeros((n_tiles * tm,), jnp.int32).at[pos].set(jnp.arange(TOP_K * T, dtype=jnp.int32))
    tok_sorted = assign // TOP_K
    dst_sorted = (assign % TOP_K) * T + tok_sorted
    n_used = tile_end[-1]
    m_ids = jnp.arange(n_tiles, dtype=jnp.int32)
    te = jnp.sum((jnp.minimum(m_ids, n_used - 1)[:, None] >= tile_end[None, :]).astype(jnp.int32), axis=1)
    first = ((m_ids == 0) | (te != jnp.roll(te, 1))) & (m_ids < n_used)
    te_onehot = (te[:, None] == experts[None, :]).astype(jnp.int32)
    rows_before = (m_ids - jnp.sum(te_onehot * tile_start[None, :], axis=1)) * tm
    valid = jnp.clip(jnp.sum(te_onehot * counts[None, :], axis=1) - rows_before, 0, tm)
    valid = jnp.where(m_ids < n_used, valid, 0).astype(jnp.int32)
    nonempty = jnp.where(tiles_e > 0, experts, E)
    after = jnp.min(jnp.where(experts[None, :] > experts[:, None], nonempty[None, :], E), axis=1)
    nxt_e = jnp.where(after < E, after, jnp.min(nonempty))
    nxt = jnp.sum(te_onehot * nxt_e[None, :], axis=1).astype(jnp.int32)
    wrap = jnp.sum(te_onehot * (after == E).astype(jnp.int32)[None, :], axis=1).astype(jnp.int32)
    tok_tiles = tok_sorted.reshape(n_tiles, tm)
    tok_last = jnp.take_along_axis(tok_tiles, jnp.maximum(valid - 1, 0)[:, None], axis=1)[:, 0]
    blk_lo = jnp.where(valid > 0, tok_tiles[:, 0] // GATHER_SRC_BLOCK, 0).astype(jnp.int32)
    blk_hi = jnp.where(valid > 0, tok_last // GATHER_SRC_BLOCK, -1).astype(jnp.int32)
    return (tok_sorted, dst_sorted, blk_lo, blk_hi,
            (te.astype(jnp.int32), first.astype(jnp.int32), valid, nxt, wrap))


def moe(x, h, w_router, w_gu, w_down, g_next, *, layer, tm_tok):
    T = x.shape[0]
    E = N_EXPERTS
    w_r = jnp.pad(w_router[layer], ((0, 0), (0, LANES - E)))
    idx, gates = router(h, w_r, tm=tm_tok)
    tok_sorted, dst_sorted, blk_lo, blk_hi, (te, first, valid, nxt, wrap) = _route_plan(idx[:, :TOP_K], MOE_TM)
    plan = (te + layer * E, first, valid, nxt + layer * E, wrap)
    xs = gather_rows(h, tok_sorted, blk_lo, blk_hi, tm=MOE_TM)
    hm = moe_up(xs, w_gu.reshape((-1,) + w_gu.shape[2:]), plan, tm=MOE_TM, tn=1024)
    ys = moe_down(hm, w_down.reshape((-1,) + w_down.shape[2:]), plan, tm=MOE_TM, tn=512)
    contrib = scatter_rows(ys, dst_sorted, valid, TOP_K * T, tm=MOE_TM)
    return moe_combine_norm(x, gates, contrib, g_next, tm=tm_tok)


def kernel(x_prompt, x_sample, state_ret, cache_k, cache_v, p_prompt, p_sample, g_mix, g_ffn, g_ple, g_final,
           ret_w_in, ret_w_out, att_w_in, att_w_out, att_rel_bias, gmlp_w_in, gmlp_ln_g, gmlp_w_s, gmlp_b_s,
           gmlp_w_out, ffn_w_gu, ffn_w_down, moe_w_router, moe_w_gu, moe_w_down, ple_w_in, ple_w_gate):
    B, S, D = x_prompt.shape
    BS, L, _ = x_sample.shape
    depth = g_mix.shape[0]
    NP, NS = B * S, BS * L
    T = NP + NS
    assert B == 1 and S % RET_CHUNK == 0 and S % ATT_QBLOCK == 0 and L <= CHUNK
    TM = T // 8
    TMS = T // 16

    x = jnp.concatenate([x_prompt.reshape(NP, D), x_sample.reshape(NS, D)], axis=0)
    p_all = jnp.concatenate([p_prompt.reshape(depth, NP, -1), p_sample.reshape(depth, NS, -1)], axis=1)

    pos = jnp.concatenate([jnp.arange(S, dtype=jnp.int32), jnp.tile(PAST_LEN + jnp.arange(L, dtype=jnp.int32), BS)])
    freqs = ROPE_BASE ** (-jnp.arange(0, RET_DK, 2, dtype=F32) / RET_DK)
    ang = pos.astype(F32)[:, None] * freqs[None, :]
    cos, sin = jnp.cos(ang), jnp.sin(ang)
    log_g = jnp.log1p(-jnp.exp2(-5.0 - jnp.arange(RET_HEADS, dtype=F32)))

    ret_p, ret_s, k_p, v_p, k_s, v_s, gv = [], [], [], [], [], [], []
    for i in range(depth):
        kind, slot = i % 3, i // 3
        (h,) = rmsnorm(x, g_mix[i], [BF16], TMS)
        if kind == 0:
            qkvg = matmul(h, ret_w_in, layer=slot, tm=TM, tn=1024)
            y_p, st_p = retention(qkvg, cos, sin, log_g, None, batch=B, seq=S, chunk=RET_CHUNK, row0=0,
                                  heads_per_step=RET_HEADS // 2)
            y_s, st_s = retention(qkvg, cos, sin, log_g, state_ret, batch=BS, seq=L, chunk=L, row0=NP,
                                  heads_per_step=RET_HEADS, state_layer=slot)
            ret_p.append(st_p)
            ret_s.append(st_s)
            y = jnp.concatenate([y_p, y_s], axis=0)
            x = matmul(y, ret_w_out, layer=slot, tm=TMS, tn=512, res=x)
        elif kind == 1:
            HD = ATT_HEADS * ATT_DH
            W = cache_k.shape[2]
            past = BAND_PAST_CHUNKS * CHUNK
            qkv = matmul(h, att_w_in, layer=slot, tm=TM, tn=1024)
            bias_p = _rel_bias(att_rel_bias[slot], ATT_QBLOCK, past + ATT_QBLOCK, past)
            bias_s = _rel_bias(att_rel_bias[slot], L, W + L, W)
            o_p = attention_prompt(qkv, bias_p, seq=S)
            o_s = attention_sample(qkv, cache_k, cache_v, bias_s[:, :, :W], bias_s[:, :, W:],
                                   batch=BS, seq=L, row0=NP, layer=slot)
            w = min(past, S)
            k_p.append(qkv[NP - w:NP, HD:2 * HD].reshape(B, w, ATT_HEADS, ATT_DH))
            v_p.append(qkv[NP - w:NP, 2 * HD:].reshape(B, w, ATT_HEADS, ATT_DH))
            k_s.append(qkv[NP:, HD:2 * HD].reshape(BS, L, ATT_HEADS, ATT_DH))
            v_s.append(qkv[NP:, 2 * HD:].reshape(BS, L, ATT_HEADS, ATT_DH))
            o = jnp.concatenate([o_p, o_s], axis=0)
            x = matmul(o, att_w_out, layer=slot, tm=TM, tn=1024, res=x)
        else:
            tile = GMLP_CHUNK
            uv = matmul(h, gmlp_w_in, layer=slot, tm=TM, tn=1024)
            r = jnp.arange(tile)
            ws_p = jnp.where((r[:, None] >= r[None, :])[None], gmlp_w_s[slot][:, :tile, :tile], 0.0)
            same = (r[:, None] // L == r[None, :] // L) & (r[:, None] >= r[None, :])
            ws_s = jnp.where(same[None], jnp.tile(gmlp_w_s[slot][:, :L, :L], (1, tile // L, tile // L)), 0.0)
            bs_p = gmlp_b_s[slot][:, :tile]
            bs_s = jnp.tile(gmlp_b_s[slot][:, :L], (1, tile // L))
            ws = jnp.stack([ws_p, ws_s]).astype(BF16)
            bs = jnp.stack([bs_p, bs_s])[..., None]
            yg, vn = gmlp_gate(uv, gmlp_ln_g[slot], ws, bs, n_prompt_tiles=NP // tile, tile=tile)
            gv.append((vn[:tile].reshape(B, tile, -1), vn[tile:].reshape(BS, L, -1)))
            x = matmul(yg, gmlp_w_out, layer=slot, tm=TMS, tn=512, res=x)

        if i % 2 == 0:
            (h,) = rmsnorm(x, g_ffn[i], [BF16], TMS)
            x = dense_ffn(x, h, ffn_w_gu[i // 2], ffn_w_down[i // 2], tm_up=TM, tm_down=TMS)
            (hn,) = rmsnorm(x, g_ple[i], [BF16], TMS)
        else:
            (h,) = rmsnorm(x, g_ffn[i], [BF16], TMS)
            x, hn = moe(x, h, moe_w_router, moe_w_gu, moe_w_down, g_ple[i], layer=i // 2, tm_tok=TMS)

        x = ple(x, hn, p_all, ple_w_gate, ple_w_in, layer=i, tm=TMS, tn=1024)

    (y,) = rmsnorm(x, g_final, [F32], TMS)
    y_prompt = y[:NP].reshape(B, S, D)
    y_sample = y[NP:].reshape(BS, L, D)
    return (y_prompt, y_sample, jnp.stack(ret_p), jnp.stack(ret_s), jnp.stack(k_p), jnp.stack(v_p),
            jnp.stack(k_s), jnp.stack(v_s), jnp.stack([a for a, _ in gv]), jnp.stack([b for _, b in gv]))
```

```python
import functools

import jax
import jax.numpy as jnp
from jax import lax
from jax.experimental import pallas as pl
from jax.experimental.pallas import tpu as pltpu

F32 = jnp.float32
BF16 = jnp.bfloat16

EPS = 1e-6
NEG_INF = -1e30
CHUNK = 64
ROPE_BASE = 10000.0
RET_HEADS, RET_DK, RET_DV = 8, 256, 512
ATT_HEADS, ATT_DH = 16, 128
BAND_PAST_CHUNKS = 8
REL_CLIP = 128
GMLP_CHUNK, GMLP_GROUPS = 128, 8
N_EXPERTS, TOP_K = 8, 2
PAST_LEN = 2048

LANES = 128
DMA_PRIORITIES = 2
V7X_VMEM_LIMIT_BYTES = 60 * 1000 * 1024

RET_CHUNK = 256
ATT_QBLOCK = 2 * CHUNK
MOE_TM = 512
GATHER_SRC_BLOCK = 512


def _params(*sem):
    return pltpu.CompilerParams(dimension_semantics=sem, vmem_limit_bytes=V7X_VMEM_LIMIT_BYTES)


def _silu(x):
    return x * jax.nn.sigmoid(x)


def _rmsnorm_kernel(x_ref, g_ref, *o_refs):
    x = x_ref[...]
    y = x * lax.rsqrt(jnp.mean(x * x, axis=-1, keepdims=True) + EPS) * g_ref[...]
    for o_ref in o_refs:
        o_ref[...] = y.astype(o_ref.dtype)


def rmsnorm(x, g, dtypes, tm):
    T, D = x.shape
    out_specs = [pl.BlockSpec((tm, D), lambda i: (i, 0)) for _ in dtypes]
    out_shape = [jax.ShapeDtypeStruct((T, D), dt) for dt in dtypes]
    return pl.pallas_call(
        _rmsnorm_kernel,
        grid=(T // tm,),
        in_specs=[pl.BlockSpec((tm, D), lambda i: (i, 0)), pl.BlockSpec((1, D), lambda i: (0, 0))],
        out_specs=out_specs,
        out_shape=out_shape,
        compiler_params=_params("arbitrary"),
        name="rmsnorm",
    )(x, g.reshape(1, D))


def _mm_kernel(a_ref, w_ref, *rest, has_res, cast_w):
    if has_res:
        r_ref, o_ref, *scratch = rest
    else:
        o_ref, *scratch = rest
    if cast_w:
        w_bf = scratch[0]

        @pl.when(pl.program_id(1) == 0)
        def _():
            w_bf[...] = w_ref[...].astype(BF16)

        w = w_bf[...]
    else:
        w = w_ref[...]
    acc = jnp.dot(a_ref[...], w, preferred_element_type=F32)
    if has_res:
        acc = r_ref[...] + acc
    o_ref[...] = acc.astype(o_ref.dtype)


def _weight_spec(w, layer, tn):
    K = w.shape[-2]
    if w.ndim == 2:
        return pl.BlockSpec((K, tn), lambda n, m: (0, n))
    return pl.BlockSpec((None, K, tn), lambda n, m: (layer, 0, n))


def matmul(a, w, *, tm, tn, layer=0, out_dtype=F32, res=None):
    M, K = a.shape
    N = w.shape[-1]
    cast_w = w.dtype != BF16
    in_specs = [pl.BlockSpec((tm, K), lambda n, m: (m, 0)), _weight_spec(w, layer, tn)]
    args = [a, w]
    if res is not None:
        in_specs.append(pl.BlockSpec((tm, tn), lambda n, m: (m, n)))
        args.append(res)
    return pl.pallas_call(
        functools.partial(_mm_kernel, has_res=res is not None, cast_w=cast_w),
        grid=(N // tn, M // tm),
        in_specs=in_specs,
        out_specs=pl.BlockSpec((tm, tn), lambda n, m: (m, n)),
        out_shape=jax.ShapeDtypeStruct((M, N), out_dtype),
        scratch_shapes=[pltpu.VMEM((K, tn), BF16)] if cast_w else [],
        compiler_params=_params("arbitrary", "arbitrary"),
        name="matmul",
    )(*args)


def _swiglu_up_kernel(x_ref, w_hbm, o_ref, stage_a, stage_b, wa_bf, wb_bf, sems, *, layer, tail):
    n = pl.program_id(0)
    nb = pl.num_programs(0)
    tn = wa_bf.shape[1]
    F = w_hbm.shape[2] // 2

    def copies(nn, width):
        col = nn * tn
        a = pltpu.make_async_copy(w_hbm.at[layer, :, pl.ds(pl.multiple_of(col, LANES), width)],
                                  stage_a.at[:, pl.ds(0, width)], sems.at[0])
        b = pltpu.make_async_copy(w_hbm.at[layer, :, pl.ds(pl.multiple_of(F + col, LANES), width)],
                                  stage_b.at[:, pl.ds(0, width)], sems.at[1])
        return a, b

    def for_tile(nn, action):
        if tail == tn:
            for cp in copies(nn, tn):
                action(cp)
        else:
            @pl.when(nn < nb - 1)
            def _():
                for cp in copies(nn, tn):
                    action(cp)

            @pl.when(nn == nb - 1)
            def _():
                for cp in copies(nn, tail):
                    action(cp)

    @pl.when(pl.program_id(1) == 0)
    def _():
        @pl.when(n == 0)
        def _():
            for_tile(n, lambda cp: cp.start())

        for_tile(n, lambda cp: cp.wait())
        wa_bf[...] = stage_a[...].astype(BF16)
        wb_bf[...] = stage_b[...].astype(BF16)

        @pl.when(n + 1 < nb)
        def _():
            for_tile(n + 1, lambda cp: cp.start())

    x = x_ref[...]
    a = jnp.dot(x, wa_bf[...], preferred_element_type=F32)
    b = jnp.dot(x, wb_bf[...], preferred_element_type=F32)
    o_ref[...] = (_silu(a) * b).astype(o_ref.dtype)


def swiglu_up(x, w_gu, *, layer, tm, tn):
    T, D = x.shape
    F = w_gu.shape[2] // 2
    nb = pl.cdiv(F, tn)
    tail = F - (nb - 1) * tn
    assert F % LANES == 0 and (nb > 1 or tail == tn)
    return pl.pallas_call(
        functools.partial(_swiglu_up_kernel, layer=layer, tail=tail),
        grid=(nb, T // tm),
        in_specs=[pl.BlockSpec((tm, D), lambda n, m: (m, 0)), pl.BlockSpec(memory_space=pl.ANY)],
        out_specs=pl.BlockSpec((tm, tn), lambda n, m: (m, n)),
        out_shape=jax.ShapeDtypeStruct((T, F), BF16),
        scratch_shapes=[pltpu.VMEM((D, tn), w_gu.dtype), pltpu.VMEM((D, tn), w_gu.dtype),
                        pltpu.VMEM((D, tn), BF16), pltpu.VMEM((D, tn), BF16), pltpu.SemaphoreType.DMA((2,))],
        compiler_params=_params("arbitrary", "arbitrary"),
        name="swiglu_up",
    )(x, w_gu)


def dense_ffn(x, h, w_gu, w_down, *, layer, tm_up, tm_down):
    hm = swiglu_up(h, w_gu, layer=layer, tm=tm_up, tn=512)
    return matmul(hm, w_down, layer=layer, tm=tm_down, tn=512, res=x)


def _ple_kernel(hn_ref, wg_ref, p_ref, wi_ref, x_ref, o_ref, wg_bf, wi_bf):
    @pl.when(pl.program_id(1) == 0)
    def _():
        wg_bf[...] = wg_ref[...].astype(BF16)
        wi_bf[...] = wi_ref[...].astype(BF16)

    gate = jax.nn.sigmoid(jnp.dot(hn_ref[...], wg_bf[...], preferred_element_type=F32))
    emb = jnp.dot(p_ref[...].astype(BF16), wi_bf[...], preferred_element_type=F32)
    o_ref[...] = x_ref[...] + gate * emb


def ple(x, hn, p, w_gate, w_in, *, layer, tm, tn):
    T, D = x.shape
    DP = p.shape[2]
    return pl.pallas_call(
        _ple_kernel,
        grid=(D // tn, T // tm),
        in_specs=[pl.BlockSpec((tm, D), lambda n, m: (m, 0)),
                  _weight_spec(w_gate, layer, tn),
                  pl.BlockSpec((None, tm, DP), lambda n, m: (layer, m, 0)),
                  _weight_spec(w_in, layer, tn),
                  pl.BlockSpec((tm, tn), lambda n, m: (m, n))],
        out_specs=pl.BlockSpec((tm, tn), lambda n, m: (m, n)),
        out_shape=jax.ShapeDtypeStruct((T, D), F32),
        scratch_shapes=[pltpu.VMEM((D, tn), BF16), pltpu.VMEM((DP, tn), BF16)],
        compiler_params=_params("arbitrary", "arbitrary"),
        name="ple",
    )(hn, w_gate, p, w_in, x)


def _retention_kernel(lg_ref, q_ref, k_ref, v_ref, g_ref, cos_ref, sin_ref, *rest, has_state0):
    if has_state0:
        s0_ref, y_ref, sout_ref, state = rest
    else:
        y_ref, sout_ref, state = rest
    hb = pl.program_id(1)
    c = pl.program_id(2)
    C = q_ref.shape[0]
    hp = state.shape[0]
    dk, dv = RET_DK, RET_DV
    half = dk // 2

    @pl.when(c == 0)
    def _():
        if has_state0:
            state[...] = s0_ref[...]
        else:
            state[...] = jnp.zeros_like(state)

    cos = cos_ref[...]
    sin = sin_ref[...]

    def rotary(x):
        x1, x2 = x[:, :half], x[:, half:]
        return jnp.concatenate([x1 * cos - x2 * sin, x1 * sin + x2 * cos], axis=-1)

    row = lax.broadcasted_iota(jnp.int32, (C, C), 0)
    col = lax.broadcasted_iota(jnp.int32, (C, C), 1)
    rel = (row - col).astype(F32)
    rel_pos = jnp.maximum(rel, 0.0)
    idx = lax.broadcasted_iota(jnp.int32, (C, 1), 0).astype(F32)

    for j in range(hp):
        lg = lg_ref[hb * hp + j]
        q = rotary(q_ref[:, j * dk:(j + 1) * dk])
        k = rotary(k_ref[:, j * dk:(j + 1) * dk]) * dk ** -0.5
        vb = v_ref[:, j * dv:(j + 1) * dv].astype(BF16)
        decay = jnp.where(rel >= 0, jnp.exp(rel_pos * lg), 0.0)
        q_decay = jnp.exp((idx + 1.0) * lg)
        k_decay = jnp.exp((C - 1.0 - idx) * lg)
        chunk_decay = jnp.exp(jnp.zeros((1, 1), F32) + C * lg)

        qb = q.astype(BF16)
        scores = lax.dot_general(qb, k.astype(BF16), (((1,), (1,)), ((), ())), preferred_element_type=F32) * decay
        o = jnp.dot(scores.astype(BF16), vb, preferred_element_type=F32)
        st = state[j]
        o = o + jnp.dot(qb, st.astype(BF16), preferred_element_type=F32) * q_decay
        kv = lax.dot_general((k * k_decay).astype(BF16), vb, (((0,), (0,)), ((), ())), preferred_element_type=F32)
        st = st * chunk_decay + kv
        state[j] = st
        sout_ref[j] = st

        o = o * lax.rsqrt(jnp.mean(o * o, axis=-1, keepdims=True) + EPS)
        y_ref[:, j * dv:(j + 1) * dv] = (o * _silu(g_ref[:, j * dv:(j + 1) * dv])).astype(y_ref.dtype)


def retention(qkvg, cos, sin, log_g, state0, *, batch, seq, chunk, row0, heads_per_step, state_layer=0):
    H, dk, dv = RET_HEADS, RET_DK, RET_DV
    hp = heads_per_step
    nhb = H // hp
    nc = seq // chunk
    rb0 = row0 // chunk
    kq = nhb
    kv = (2 * H * dk) // (hp * dv)
    kg = kv + nhb

    def rows(b, c):
        return rb0 + b * nc + c

    in_specs = [
        pl.BlockSpec((chunk, hp * dk), lambda b, h, c, lg: (rows(b, c), h)),
        pl.BlockSpec((chunk, hp * dk), lambda b, h, c, lg: (rows(b, c), kq + h)),
        pl.BlockSpec((chunk, hp * dv), lambda b, h, c, lg: (rows(b, c), kv + h)),
        pl.BlockSpec((chunk, hp * dv), lambda b, h, c, lg: (rows(b, c), kg + h)),
        pl.BlockSpec((chunk, dk // 2), lambda b, h, c, lg: (rows(b, c), 0)),
        pl.BlockSpec((chunk, dk // 2), lambda b, h, c, lg: (rows(b, c), 0)),
    ]
    args = [qkvg, qkvg, qkvg, qkvg, cos, sin]
    if state0 is not None:
        in_specs.append(pl.BlockSpec((None, None, hp, dk, dv), lambda b, h, c, lg: (state_layer, b, h, 0, 0)))
        args.append(state0)
    grid_spec = pltpu.PrefetchScalarGridSpec(
        num_scalar_prefetch=1,
        grid=(batch, nhb, nc),
        in_specs=in_specs,
        out_specs=[pl.BlockSpec((chunk, hp * dv), lambda b, h, c, lg: (b * nc + c, h)),
                   pl.BlockSpec((None, hp, dk, dv), lambda b, h, c, lg: (b, h, 0, 0))],
        scratch_shapes=[pltpu.VMEM((hp, dk, dv), F32)],
    )
    return pl.pallas_call(
        functools.partial(_retention_kernel, has_state0=state0 is not None),
        grid_spec=grid_spec,
        out_shape=[jax.ShapeDtypeStruct((batch * seq, H * dv), BF16),
                   jax.ShapeDtypeStruct((batch, H, dk, dv), F32)],
        compiler_params=_params("arbitrary", "arbitrary", "arbitrary"),
        name="retention",
    )(log_g, *args)


def _att_prompt_kernel(q_ref, k_ref, v_ref, bias_ref, o_ref, kpad, vpad):
    S = q_ref.shape[0]
    QB = ATT_QBLOCK
    past = BAND_PAST_CHUNKS * CHUNK
    win = past + QB
    kpad[:past, :] = jnp.zeros((past, ATT_DH), BF16)
    vpad[:past, :] = jnp.zeros((past, ATT_DH), BF16)
    kpad[past:, :] = k_ref[...].astype(BF16)
    vpad[past:, :] = v_ref[...].astype(BF16)
    bias = bias_ref[...]
    qi = lax.broadcasted_iota(jnp.int32, (QB, win), 0)
    kj = lax.broadcasted_iota(jnp.int32, (QB, win), 1)
    chunk_shift = CHUNK.bit_length() - 1
    qc = jnp.right_shift(qi, chunk_shift)
    kc = jnp.right_shift(kj, chunk_shift)
    band = (kc >= qc) & (kc <= qc + BAND_PAST_CHUNKS)

    def body(i, carry):
        q0 = pl.multiple_of(i * QB, QB)
        qb = q_ref[pl.ds(q0, QB), :].astype(BF16)
        kw = kpad[pl.ds(q0, win), :]
        vw = vpad[pl.ds(q0, win), :]
        s = lax.dot_general(qb, kw, (((1,), (1,)), ((), ())), preferred_element_type=F32)
        s = s * ATT_DH ** -0.5 + bias
        valid = band & (kj >= past - q0)
        s = jnp.where(valid, s, NEG_INF)
        m = jnp.max(s, axis=-1, keepdims=True)
        e = jnp.exp(s - m)
        pr = (e / jnp.sum(e, axis=-1, keepdims=True)).astype(BF16)
        o_ref[pl.ds(q0, QB), :] = jnp.dot(pr, vw, preferred_element_type=F32).astype(o_ref.dtype)
        return carry

    lax.fori_loop(0, S // QB, body, 0, unroll=4)


def attention_prompt(qkv, bias, *, seq):
    H, dh = ATT_HEADS, ATT_DH
    past = BAND_PAST_CHUNKS * CHUNK
    return pl.pallas_call(
        _att_prompt_kernel,
        grid=(H,),
        in_specs=[pl.BlockSpec((seq, dh), lambda h: (0, h)),
                  pl.BlockSpec((seq, dh), lambda h: (0, H + h)),
                  pl.BlockSpec((seq, dh), lambda h: (0, 2 * H + h)),
                  pl.BlockSpec((None, ATT_QBLOCK, past + ATT_QBLOCK), lambda h: (h, 0, 0))],
        out_specs=pl.BlockSpec((seq, dh), lambda h: (0, h)),
        out_shape=jax.ShapeDtypeStruct((seq, H * dh), BF16),
        scratch_shapes=[pltpu.VMEM((past + seq, dh), BF16), pltpu.VMEM((past + seq, dh), BF16)],
        compiler_params=_params("arbitrary"),
        name="attention_prompt",
    )(qkv, qkv, qkv, bias)


def _att_sample_kernel(q_ref, kn_ref, vn_ref, ck_ref, cv_ref, bp_ref, bn_ref, o_ref):
    dh = ATT_DH
    H = ATT_HEADS
    W = ck_ref.shape[0] // H
    for h in range(H):
        sl = slice(h * dh, (h + 1) * dh)
        qb = q_ref[:, sl].astype(BF16)
        kn = kn_ref[:, sl].astype(BF16)
        vn = vn_ref[:, sl].astype(BF16)
        kp = ck_ref[pl.ds(h, W, stride=H), :].astype(BF16)
        vp = cv_ref[pl.ds(h, W, stride=H), :].astype(BF16)
        sp = lax.dot_general(qb, kp, (((1,), (1,)), ((), ())), preferred_element_type=F32)
        sn = lax.dot_general(qb, kn, (((1,), (1,)), ((), ())), preferred_element_type=F32)
        sp = sp * dh ** -0.5 + bp_ref[h]
        sn = sn * dh ** -0.5 + bn_ref[h]
        m = jnp.maximum(jnp.max(sp, axis=-1, keepdims=True), jnp.max(sn, axis=-1, keepdims=True))
        ep = jnp.exp(sp - m)
        en = jnp.exp(sn - m)
        den = jnp.sum(ep, axis=-1, keepdims=True) + jnp.sum(en, axis=-1, keepdims=True)
        o = jnp.dot((ep / den).astype(BF16), vp, preferred_element_type=F32)
        o = o + jnp.dot((en / den).astype(BF16), vn, preferred_element_type=F32)
        o_ref[:, sl] = o.astype(o_ref.dtype)


def attention_sample(qkv, cache_k, cache_v, bias_past, bias_new, *, batch, seq, row0, layer):
    H, dh = ATT_HEADS, ATT_DH
    W = cache_k.shape[2]
    rb0 = row0 // seq
    cache_k = cache_k.reshape(-1, dh)
    cache_v = cache_v.reshape(-1, dh)
    return pl.pallas_call(
        _att_sample_kernel,
        grid=(batch,),
        in_specs=[pl.BlockSpec((seq, H * dh), lambda b: (rb0 + b, 0)),
                  pl.BlockSpec((seq, H * dh), lambda b: (rb0 + b, 1)),
                  pl.BlockSpec((seq, H * dh), lambda b: (rb0 + b, 2)),
                  pl.BlockSpec((W * H, dh), lambda b: (layer * batch + b, 0)),
                  pl.BlockSpec((W * H, dh), lambda b: (layer * batch + b, 0)),
                  pl.BlockSpec((H, seq, W), lambda b: (0, 0, 0)),
                  pl.BlockSpec((H, seq, seq), lambda b: (0, 0, 0))],
        out_specs=pl.BlockSpec((seq, H * dh), lambda b: (b, 0)),
        out_shape=jax.ShapeDtypeStruct((batch * seq, H * dh), BF16),
        compiler_params=_params("arbitrary"),
        name="attention_sample",
    )(qkv, qkv, qkv, cache_k, cache_v, bias_past, bias_new)


def _rel_bias(table, n_q, n_k, offset):
    period = n_q + n_k
    d = jnp.arange(period)
    d = jnp.where(d < n_k, d, d - period)
    u = jnp.take(table, jnp.clip(offset - d, -REL_CLIP, REL_CLIP) + REL_CLIP, axis=0).T.astype(F32)
    H = u.shape[0]
    skew = jnp.tile(u, (1, n_q))[:, :n_q * (period - 1)].reshape(H, n_q, period - 1)
    return skew[:, :, :n_k]


def _gmlp_kernel(uv_ref, lng_ref, ws_ref, bs_ref, y_ref, vn_ref):
    DV = y_ref.shape[1]
    cg = DV // GMLP_GROUPS
    v = jax.nn.gelu(uv_ref[:, DV:])
    mu = jnp.mean(v, axis=-1, keepdims=True)
    d = v - mu
    var = jnp.mean(d * d, axis=-1, keepdims=True)
    vn = d * lax.rsqrt(var + EPS) * lng_ref[...]
    vn_ref[...] = vn
    for g in range(GMLP_GROUPS):
        sl = slice(g * cg, (g + 1) * cg)
        mixed = jnp.dot(ws_ref[g], vn[:, sl].astype(BF16), preferred_element_type=F32) + bs_ref[g]
        y_ref[:, sl] = (jax.nn.gelu(uv_ref[:, sl]) * mixed).astype(y_ref.dtype)


def gmlp_gate(uv, ln_g, ws, bs, *, n_prompt_tiles, tile):
    T = uv.shape[0]
    DV = uv.shape[1] // 2
    G = GMLP_GROUPS
    nt = T // tile
    last = n_prompt_tiles - 1
    return pl.pallas_call(
        _gmlp_kernel,
        grid=(nt,),
        in_specs=[pl.BlockSpec((tile, 2 * DV), lambda i: (i, 0)),
                  pl.BlockSpec((1, DV), lambda i: (0, 0)),
                  pl.BlockSpec((None, G, tile, tile), lambda i: (i // n_prompt_tiles, 0, 0, 0)),
                  pl.BlockSpec((None, G, tile, 1), lambda i: (i // n_prompt_tiles, 0, 0, 0))],
        out_specs=[pl.BlockSpec((tile, DV), lambda i: (i, 0)),
                   pl.BlockSpec((tile, DV), lambda i: (jnp.maximum(i - last, 0), 0))],
        out_shape=[jax.ShapeDtypeStruct((T, DV), BF16),
                   jax.ShapeDtypeStruct(((nt - last) * tile, DV), F32)],
        compiler_params=_params("arbitrary"),
        name="gmlp_gate",
    )(uv, ln_g.reshape(1, DV), ws, bs)


def _router_kernel(h_ref, w_ref, idx_ref, gate_ref):
    logits = jnp.dot(h_ref[...], w_ref[...].astype(BF16), preferred_element_type=F32)
    lane = lax.broadcasted_iota(jnp.int32, logits.shape, 1)
    lane_f = lane.astype(F32)
    l1 = jnp.where(lane < N_EXPERTS, logits, -jnp.inf)
    v1 = jnp.max(l1, axis=-1, keepdims=True)
    i1 = jnp.min(jnp.where(l1 == v1, lane_f, float(LANES)), axis=-1, keepdims=True)
    l2 = jnp.where(lane_f == i1, -jnp.inf, l1)
    v2 = jnp.max(l2, axis=-1, keepdims=True)
    i2 = jnp.min(jnp.where(l2 == v2, lane_f, float(LANES)), axis=-1, keepdims=True)
    e2 = jnp.exp(v2 - v1)
    den = 1.0 + e2
    idx_ref[...] = jnp.where(lane == 0, i1, jnp.where(lane == 1, i2, 0.0)).astype(jnp.int32)
    gate_ref[...] = jnp.where(lane == 0, 1.0 / den, jnp.where(lane == 1, e2 / den, 0.0))


def router(h, w_router_padded, *, tm):
    T, D = h.shape
    return pl.pallas_call(
        _router_kernel,
        grid=(T // tm,),
        in_specs=[pl.BlockSpec((tm, D), lambda i: (i, 0)), pl.BlockSpec((D, LANES), lambda i: (0, 0))],
        out_specs=[pl.BlockSpec((tm, LANES), lambda i: (i, 0)), pl.BlockSpec((tm, LANES), lambda i: (i, 0))],
        out_shape=[jax.ShapeDtypeStruct((T, LANES), jnp.int32), jax.ShapeDtypeStruct((T, LANES), F32)],
        compiler_params=_params("arbitrary"),
        name="router",
    )(h, w_router_padded)


def _gather_kernel(lo_ref, hi_ref, p0_ref, p1_ref, h_ref, o_ref, acc):
    m = pl.program_id(0)
    tm = o_ref.shape[0]
    sb = GATHER_SRC_BLOCK
    row = m * tm + lax.broadcasted_iota(jnp.int32, (tm, sb), 0)
    acc[...] = jnp.zeros_like(acc)

    def body(b, carry):
        select = jnp.where(p0_ref[b] == row, 1.0, jnp.where(p1_ref[b] == row, 1.0, 0.0)).astype(h_ref.dtype)
        acc[...] += jnp.dot(select, h_ref[pl.ds(pl.multiple_of(b * sb, sb), sb), :], preferred_element_type=F32)
        return carry

    lax.fori_loop(lo_ref[m], hi_ref[m] + 1, body, 0)
    o_ref[...] = acc[...].astype(o_ref.dtype)


def gather_rows(h, pos0, pos1, blk_lo, blk_hi, n_rows, *, tm):
    T, D = h.shape
    nsb = T // GATHER_SRC_BLOCK
    grid_spec = pltpu.PrefetchScalarGridSpec(
        num_scalar_prefetch=2,
        grid=(n_rows // tm,),
        in_specs=[pl.BlockSpec((nsb, 1, GATHER_SRC_BLOCK), lambda i, lo, hi: (0, 0, 0)),
                  pl.BlockSpec((nsb, 1, GATHER_SRC_BLOCK), lambda i, lo, hi: (0, 0, 0)),
                  pl.BlockSpec((T, D), lambda i, lo, hi: (0, 0), pipeline_mode=pl.Buffered(1))],
        out_specs=pl.BlockSpec((tm, D), lambda i, lo, hi: (i, 0)),
        scratch_shapes=[pltpu.VMEM((tm, D), F32)],
    )
    return pl.pallas_call(
        _gather_kernel,
        grid_spec=grid_spec,
        out_shape=jax.ShapeDtypeStruct((n_rows, D), h.dtype),
        compiler_params=_params("arbitrary"),
        name="moe_gather",
    )(blk_lo, blk_hi, pos0.reshape(nsb, 1, GATHER_SRC_BLOCK), pos1.reshape(nsb, 1, GATHER_SRC_BLOCK), h)


def _on_valid_rows(valid, o_ref, compute):
    tm = o_ref.shape[0]
    half = tm // 2

    @pl.when(valid > half)
    def _():
        compute(slice(0, tm))

    @pl.when((valid > 0) & (valid <= half))
    def _():
        compute(slice(0, half))
        o_ref[half:, :] = jnp.zeros((tm - half, o_ref.shape[1]), o_ref.dtype)

    @pl.when(valid == 0)
    def _():
        o_ref[...] = jnp.zeros_like(o_ref)


def _group_weights(te_ref, first_ref, nxt_ref, wrap_ref, copies, cast):
    n = pl.program_id(0)
    m = pl.program_id(1)

    @pl.when(first_ref[m] == 1)
    def _():
        @pl.when((n == 0) & (m == 0))
        def _():
            for cp in copies(te_ref[0], 0):
                cp.start()

        for cp in copies(te_ref[m], n):
            cp.wait()
        cast()
        n_next = n + wrap_ref[m]

        @pl.when(n_next < pl.num_programs(0))
        def _():
            for cp in copies(nxt_ref[m], n_next):
                cp.start()


def _moe_up_kernel(te_ref, first_ref, valid_ref, nxt_ref, wrap_ref, x_ref, w_hbm, o_ref,
                   stage_a, stage_b, wa_bf, wb_bf, sems):
    m = pl.program_id(1)
    tn = wa_bf.shape[1]
    F = w_hbm.shape[2] // 2

    def copies(e, n):
        col = pl.multiple_of(n * tn, tn)
        return (pltpu.make_async_copy(w_hbm.at[e, :, pl.ds(col, tn)], stage_a, sems.at[0]),
                pltpu.make_async_copy(w_hbm.at[e, :, pl.ds(pl.multiple_of(F + col, tn), tn)], stage_b, sems.at[1]))

    def cast():
        wa_bf[...] = stage_a[...].astype(BF16)
        wb_bf[...] = stage_b[...].astype(BF16)

    _group_weights(te_ref, first_ref, nxt_ref, wrap_ref, copies, cast)

    def compute(rows):
        x = x_ref[rows, :]
        a = jnp.dot(x, wa_bf[...], preferred_element_type=F32)
        b = jnp.dot(x, wb_bf[...], preferred_element_type=F32)
        o_ref[rows, :] = (_silu(a) * b).astype(o_ref.dtype)

    _on_valid_rows(valid_ref[m], o_ref, compute)


def moe_up(xs, w_gu, plan, *, tm, tn):
    P, D = xs.shape
    F = w_gu.shape[2] // 2
    grid_spec = pltpu.PrefetchScalarGridSpec(
        num_scalar_prefetch=len(plan),
        grid=(F // tn, P // tm),
        in_specs=[pl.BlockSpec((tm, D), lambda n, m, *_: (m, 0)),
                  pl.BlockSpec(memory_space=pl.ANY)],
        out_specs=pl.BlockSpec((tm, tn), lambda n, m, *_: (m, n)),
        scratch_shapes=[pltpu.VMEM((D, tn), w_gu.dtype), pltpu.VMEM((D, tn), w_gu.dtype),
                        pltpu.VMEM((D, tn), BF16), pltpu.VMEM((D, tn), BF16), pltpu.SemaphoreType.DMA((2,))],
    )
    return pl.pallas_call(
        _moe_up_kernel,
        grid_spec=grid_spec,
        out_shape=jax.ShapeDtypeStruct((P, F), BF16),
        compiler_params=_params("arbitrary", "arbitrary"),
        name="moe_up",
    )(*plan, xs, w_gu)


def _moe_down_kernel(te_ref, first_ref, valid_ref, nxt_ref, wrap_ref, x_ref, w_hbm, o_ref, stage, w_bf, sem):
    m = pl.program_id(1)
    tn = w_bf.shape[1]

    def copies(e, n):
        return (pltpu.make_async_copy(w_hbm.at[e, :, pl.ds(pl.multiple_of(n * tn, tn), tn)], stage, sem),)

    def cast():
        w_bf[...] = stage[...].astype(BF16)

    _group_weights(te_ref, first_ref, nxt_ref, wrap_ref, copies, cast)

    def compute(rows):
        o_ref[rows, :] = jnp.dot(x_ref[rows, :], w_bf[...], preferred_element_type=F32)

    _on_valid_rows(valid_ref[m], o_ref, compute)


def moe_down(hm, w_down, plan, *, tm, tn):
    P, F = hm.shape
    D = w_down.shape[2]
    grid_spec = pltpu.PrefetchScalarGridSpec(
        num_scalar_prefetch=len(plan),
        grid=(D // tn, P // tm),
        in_specs=[pl.BlockSpec((tm, F), lambda n, m, *_: (m, 0)),
                  pl.BlockSpec(memory_space=pl.ANY)],
        out_specs=pl.BlockSpec((tm, tn), lambda n, m, *_: (m, n)),
        scratch_shapes=[pltpu.VMEM((F, tn), w_down.dtype), pltpu.VMEM((F, tn), BF16), pltpu.SemaphoreType.DMA(())],
    )
    return pl.pallas_call(
        _moe_down_kernel,
        grid_spec=grid_spec,
        out_shape=jax.ShapeDtypeStruct((P, D), F32),
        compiler_params=_params("arbitrary", "arbitrary"),
        name="moe_down",
    )(*plan, hm, w_down)


def _row_copy(y_hbm, buf, src_row, dst_row, sem):
    return pltpu.make_async_copy(y_hbm.at[pl.ds(src_row, 1), :], buf.at[pl.ds(dst_row, 1), :], sem)


def _combine_norm_kernel(p0_ref, p1_ref, x_ref, gate_ref, g_ref, y_hbm, xo_ref, hn_ref, buf0, buf1, sems):
    tm = x_ref.shape[0]
    base = pl.program_id(0) * tm

    def start(r, carry):
        _row_copy(y_hbm, buf0, p0_ref[base + r], r, sems.at[0]).start(priority=0)
        _row_copy(y_hbm, buf1, p1_ref[base + r], r, sems.at[1]).start(priority=DMA_PRIORITIES - 1)
        return carry

    def wait(r, carry):
        _row_copy(y_hbm, buf0, 0, r, sems.at[0]).wait()
        _row_copy(y_hbm, buf1, 0, r, sems.at[1]).wait()
        return carry

    lax.fori_loop(0, tm, start, 0)
    lax.fori_loop(0, tm, wait, 0)
    gate = gate_ref[...]
    xn = x_ref[...] + (gate[:, 0:1] * buf0[...] + gate[:, 1:2] * buf1[...])
    xo_ref[...] = xn
    hn = xn * lax.rsqrt(jnp.mean(xn * xn, axis=-1, keepdims=True) + EPS) * g_ref[...]
    hn_ref[...] = hn.astype(hn_ref.dtype)


def moe_combine_norm(x, gates, ys, pos0, pos1, g, *, tm):
    T, D = x.shape
    grid_spec = pltpu.PrefetchScalarGridSpec(
        num_scalar_prefetch=2,
        grid=(T // tm,),
        in_specs=[pl.BlockSpec((tm, D), lambda i, a, b: (i, 0)),
                  pl.BlockSpec((tm, LANES), lambda i, a, b: (i, 0)),
                  pl.BlockSpec((1, D), lambda i, a, b: (0, 0)),
                  pl.BlockSpec(memory_space=pl.ANY)],
        out_specs=[pl.BlockSpec((tm, D), lambda i, a, b: (i, 0)), pl.BlockSpec((tm, D), lambda i, a, b: (i, 0))],
        scratch_shapes=[pltpu.VMEM((tm, D), F32), pltpu.VMEM((tm, D), F32), pltpu.SemaphoreType.DMA((2,))],
    )
    return pl.pallas_call(
        _combine_norm_kernel,
        grid_spec=grid_spec,
        out_shape=[jax.ShapeDtypeStruct((T, D), F32), jax.ShapeDtypeStruct((T, D), BF16)],
        compiler_params=_params("arbitrary"),
        name="moe_combine_norm",
    )(pos0, pos1, x, gates, g.reshape(1, D), ys)


def _route_plan(idx, tm):
    T = idx.shape[0]
    E = N_EXPERTS
    n_tiles = (TOP_K * T) // tm + E
    experts = jnp.arange(E, dtype=jnp.int32)
    flat_e = idx.reshape(-1)
    onehot = (flat_e[:, None] == experts[None, :]).astype(jnp.int32)
    csum = jnp.cumsum(onehot, axis=0)
    rank = jnp.sum((csum - onehot) * onehot, axis=1)
    counts = csum[-1]
    tiles_e = (counts + tm - 1) // tm
    tile_end = jnp.cumsum(tiles_e)
    tile_start = tile_end - tiles_e
    pos = (jnp.sum(onehot * tile_start[None, :], axis=1) * tm + rank).astype(jnp.int32)
    n_used = tile_end[-1]
    m_ids = jnp.arange(n_tiles, dtype=jnp.int32)
    te = jnp.sum((jnp.minimum(m_ids, n_used - 1)[:, None] >= tile_end[None, :]).astype(jnp.int32), axis=1)
    first = ((m_ids == 0) | (te != jnp.roll(te, 1))) & (m_ids < n_used)
    te_onehot = (te[:, None] == experts[None, :]).astype(jnp.int32)
    rows_before = (m_ids - jnp.sum(te_onehot * tile_start[None, :], axis=1)) * tm
    valid = jnp.clip(jnp.sum(te_onehot * counts[None, :], axis=1) - rows_before, 0, tm)
    valid = jnp.where(m_ids < n_used, valid, 0).astype(jnp.int32)
    nonempty = jnp.where(tiles_e > 0, experts, E)
    after = jnp.min(jnp.where(experts[None, :] > experts[:, None], nonempty[None, :], E), axis=1)
    nxt_e = jnp.where(after < E, after, jnp.min(nonempty))
    nxt = jnp.sum(te_onehot * nxt_e[None, :], axis=1).astype(jnp.int32)
    wrap = jnp.sum(te_onehot * (after == E).astype(jnp.int32)[None, :], axis=1).astype(jnp.int32)
    csum_tile = jnp.sum(csum[:, :, None] * te_onehot.T[None, :, :], axis=1)
    pair_first = jnp.sum((csum_tile <= rows_before[None, :]).astype(jnp.int32), axis=0)
    pair_last = jnp.sum((csum_tile <= (rows_before + valid - 1)[None, :]).astype(jnp.int32), axis=0)
    blk_lo = jnp.where(valid > 0, pair_first // (TOP_K * GATHER_SRC_BLOCK), 0).astype(jnp.int32)
    blk_hi = jnp.where(valid > 0, pair_last // (TOP_K * GATHER_SRC_BLOCK), -1).astype(jnp.int32)
    return (pos[0::TOP_K], pos[1::TOP_K], blk_lo, blk_hi,
            (te.astype(jnp.int32), first.astype(jnp.int32), valid, nxt, wrap))


def moe(x, h, w_router, w_gu, w_down, g_next, *, layer, tm_tok):
    T = x.shape[0]
    E = N_EXPERTS
    w_r = jnp.pad(w_router[layer], ((0, 0), (0, LANES - E)))
    idx, gates = router(h, w_r, tm=tm_tok)
    pos0, pos1, blk_lo, blk_hi, (te, first, valid, nxt, wrap) = _route_plan(idx[:, :TOP_K], MOE_TM)
    plan = (te + layer * E, first, valid, nxt + layer * E, wrap)
    n_rows = te.shape[0] * MOE_TM
    xs = gather_rows(h, pos0, pos1, blk_lo, blk_hi, n_rows, tm=MOE_TM)
    hm = moe_up(xs, w_gu.reshape((-1,) + w_gu.shape[2:]), plan, tm=MOE_TM, tn=1024)
    ys = moe_down(hm, w_down.reshape((-1,) + w_down.shape[2:]), plan, tm=MOE_TM, tn=512)
    return moe_combine_norm(x, gates, ys, pos0, pos1, g_next, tm=tm_tok)


def kernel(x_prompt, x_sample, state_ret, cache_k, cache_v, p_prompt, p_sample, g_mix, g_ffn, g_ple, g_final,
           ret_w_in, ret_w_out, att_w_in, att_w_out, att_rel_bias, gmlp_w_in, gmlp_ln_g, gmlp_w_s, gmlp_b_s,
           gmlp_w_out, ffn_w_gu, ffn_w_down, moe_w_router, moe_w_gu, moe_w_down, ple_w_in, ple_w_gate):
    B, S, D = x_prompt.shape
    BS, L, _ = x_sample.shape
    depth = g_mix.shape[0]
    NP, NS = B * S, BS * L
    T = NP + NS
    assert B == 1 and S % RET_CHUNK == 0 and S % ATT_QBLOCK == 0 and L <= CHUNK
    TM = T // 8
    TMS = T // 16

    x = jnp.concatenate([x_prompt.reshape(NP, D), x_sample.reshape(NS, D)], axis=0)
    p_all = jnp.concatenate([p_prompt.reshape(depth, NP, -1), p_sample.reshape(depth, NS, -1)], axis=1)

    pos = jnp.concatenate([jnp.arange(S, dtype=jnp.int32), jnp.tile(PAST_LEN + jnp.arange(L, dtype=jnp.int32), BS)])
    freqs = ROPE_BASE ** (-jnp.arange(0, RET_DK, 2, dtype=F32) / RET_DK)
    ang = pos.astype(F32)[:, None] * freqs[None, :]
    cos, sin = jnp.cos(ang), jnp.sin(ang)
    log_g = jnp.log1p(-jnp.exp2(-5.0 - jnp.arange(RET_HEADS, dtype=F32)))

    ret_p, ret_s, k_p, v_p, k_s, v_s, gv = [], [], [], [], [], [], []
    for i in range(depth):
        kind, slot = i % 3, i // 3
        (h,) = rmsnorm(x, g_mix[i], [BF16], TMS)
        if kind == 0:
            qkvg = matmul(h, ret_w_in, layer=slot, tm=TM, tn=1024)
            y_p, st_p = retention(qkvg, cos, sin, log_g, None, batch=B, seq=S, chunk=RET_CHUNK, row0=0,
                                  heads_per_step=RET_HEADS // 2)
            y_s, st_s = retention(qkvg, cos, sin, log_g, state_ret, batch=BS, seq=L, chunk=L, row0=NP,
                                  heads_per_step=RET_HEADS, state_layer=slot)
            ret_p.append(st_p)
            ret_s.append(st_s)
            y = jnp.concatenate([y_p, y_s], axis=0)
            x = matmul(y, ret_w_out, layer=slot, tm=TMS, tn=512, res=x)
        elif kind == 1:
            HD = ATT_HEADS * ATT_DH
            W = cache_k.shape[2]
            past = BAND_PAST_CHUNKS * CHUNK
            qkv = matmul(h, att_w_in, layer=slot, tm=TM, tn=1024)
            bias_p = _rel_bias(att_rel_bias[slot], ATT_QBLOCK, past + ATT_QBLOCK, past)
            bias_s = _rel_bias(att_rel_bias[slot], L, W + L, W)
            o_p = attention_prompt(qkv, bias_p, seq=S)
            o_s = attention_sample(qkv, cache_k, cache_v, bias_s[:, :, :W], bias_s[:, :, W:],
                                   batch=BS, seq=L, row0=NP, layer=slot)
            w = min(past, S)
            k_p.append(qkv[NP - w:NP, HD:2 * HD].reshape(B, w, ATT_HEADS, ATT_DH))
            v_p.append(qkv[NP - w:NP, 2 * HD:].reshape(B, w, ATT_HEADS, ATT_DH))
            k_s.append(qkv[NP:, HD:2 * HD].reshape(BS, L, ATT_HEADS, ATT_DH))
            v_s.append(qkv[NP:, 2 * HD:].reshape(BS, L, ATT_HEADS, ATT_DH))
            o = jnp.concatenate([o_p, o_s], axis=0)
            x = matmul(o, att_w_out, layer=slot, tm=TM, tn=1024, res=x)
        else:
            tile = GMLP_CHUNK
            uv = matmul(h, gmlp_w_in, layer=slot, tm=TM, tn=1024)
            r = jnp.arange(tile)
            ws_p = jnp.where((r[:, None] >= r[None, :])[None], gmlp_w_s[slot][:, :tile, :tile], 0.0)
            same = (r[:, None] // L == r[None, :] // L) & (r[:, None] >= r[None, :])
            ws_s = jnp.where(same[None], jnp.tile(gmlp_w_s[slot][:, :L, :L], (1, tile // L, tile // L)), 0.0)
            bs_p = gmlp_b_s[slot][:, :tile]
            bs_s = jnp.tile(gmlp_b_s[slot][:, :L], (1, tile // L))
            ws = jnp.stack([ws_p, ws_s]).astype(BF16)
            bs = jnp.stack([bs_p, bs_s])[..., None]
            yg, vn = gmlp_gate(uv, gmlp_ln_g[slot], ws, bs, n_prompt_tiles=NP // tile, tile=tile)
            gv.append((vn[:tile].reshape(B, tile, -1), vn[tile:].reshape(BS, L, -1)))
            x = matmul(yg, gmlp_w_out, layer=slot, tm=TMS, tn=512, res=x)

        if i % 2 == 0:
            (h,) = rmsnorm(x, g_ffn[i], [BF16], TMS)
            x = dense_ffn(x, h, ffn_w_gu, ffn_w_down, layer=i // 2, tm_up=TM, tm_down=TMS)
            (hn,) = rmsnorm(x, g_ple[i], [BF16], TMS)
        else:
            (h,) = rmsnorm(x, g_ffn[i], [BF16], TMS)
            x, hn = moe(x, h, moe_w_router, moe_w_gu, moe_w_down, g_ple[i], layer=i // 2, tm_tok=TMS)

        x = ple(x, hn, p_all, ple_w_gate, ple_w_in, layer=i, tm=TMS, tn=1024)

    (y,) = rmsnorm(x, g_final, [F32], TMS)
    y_prompt = y[:NP].reshape(B, S, D)
    y_sample = y[NP:].reshape(BS, L, D)
    return (y_prompt, y_sample, jnp.stack(ret_p), jnp.stack(ret_s), jnp.stack(k_p), jnp.stack(v_p),
            jnp.stack(k_s), jnp.stack(v_s), jnp.stack([a for a, _ in gv]), jnp.stack([b for _, b in gv]))
```

```python
import functools

import jax
import jax.numpy as jnp
from jax import lax
from jax.experimental import pallas as pl
from jax.experimental.pallas import tpu as pltpu

F32 = jnp.float32
BF16 = jnp.bfloat16

EPS = 1e-6
NEG_INF = -1e30
CHUNK = 64
ROPE_BASE = 10000.0
RET_HEADS, RET_DK, RET_DV = 8, 256, 512
ATT_HEADS, ATT_DH = 16, 128
BAND_PAST_CHUNKS = 8
REL_CLIP = 128
GMLP_CHUNK, GMLP_GROUPS = 128, 8
N_EXPERTS, TOP_K = 8, 2
PAST_LEN = 2048

LANES = 128
DMA_PRIORITIES = 2
V7X_VMEM_LIMIT_BYTES = 60 * 1000 * 1024

RET_CHUNK = 256
ATT_QBLOCK = 4 * CHUNK
ATT_UNROLL = 4
MOE_TM = 512
GATHER_SRC_BLOCK = 512


def _params(*sem):
    return pltpu.CompilerParams(dimension_semantics=sem, vmem_limit_bytes=V7X_VMEM_LIMIT_BYTES)


def _silu(x):
    return x * jax.nn.sigmoid(x)


def _rmsnorm_kernel(x_ref, g_ref, *o_refs):
    x = x_ref[...]
    y = x * lax.rsqrt(jnp.mean(x * x, axis=-1, keepdims=True) + EPS) * g_ref[...]
    for o_ref in o_refs:
        o_ref[...] = y.astype(o_ref.dtype)


def rmsnorm(x, g, dtypes, tm):
    T, D = x.shape
    out_specs = [pl.BlockSpec((tm, D), lambda i: (i, 0)) for _ in dtypes]
    out_shape = [jax.ShapeDtypeStruct((T, D), dt) for dt in dtypes]
    return pl.pallas_call(
        _rmsnorm_kernel,
        grid=(T // tm,),
        in_specs=[pl.BlockSpec((tm, D), lambda i: (i, 0)), pl.BlockSpec((1, D), lambda i: (0, 0))],
        out_specs=out_specs,
        out_shape=out_shape,
        compiler_params=_params("arbitrary"),
        name="rmsnorm",
    )(x, g.reshape(1, D))


def _mm_kernel(a_ref, w_ref, *rest, has_res, cast_w):
    if has_res:
        r_ref, o_ref, *scratch = rest
    else:
        o_ref, *scratch = rest
    if cast_w:
        w_bf = scratch[0]

        @pl.when(pl.program_id(1) == 0)
        def _():
            w_bf[...] = w_ref[...].astype(BF16)

        w = w_bf[...]
    else:
        w = w_ref[...]
    acc = jnp.dot(a_ref[...], w, preferred_element_type=F32)
    if has_res:
        acc = r_ref[...] + acc
    o_ref[...] = acc.astype(o_ref.dtype)


def _weight_spec(w, layer, tn):
    K = w.shape[-2]
    if w.ndim == 2:
        return pl.BlockSpec((K, tn), lambda n, m: (0, n))
    return pl.BlockSpec((None, K, tn), lambda n, m: (layer, 0, n))


def matmul(a, w, *, tm, tn, layer=0, out_dtype=F32, res=None):
    M, K = a.shape
    N = w.shape[-1]
    cast_w = w.dtype != BF16
    in_specs = [pl.BlockSpec((tm, K), lambda n, m: (m, 0)), _weight_spec(w, layer, tn)]
    args = [a, w]
    if res is not None:
        in_specs.append(pl.BlockSpec((tm, tn), lambda n, m: (m, n)))
        args.append(res)
    return pl.pallas_call(
        functools.partial(_mm_kernel, has_res=res is not None, cast_w=cast_w),
        grid=(N // tn, M // tm),
        in_specs=in_specs,
        out_specs=pl.BlockSpec((tm, tn), lambda n, m: (m, n)),
        out_shape=jax.ShapeDtypeStruct((M, N), out_dtype),
        scratch_shapes=[pltpu.VMEM((K, tn), BF16)] if cast_w else [],
        compiler_params=_params("arbitrary", "arbitrary"),
        name="matmul",
    )(*args)


def _swiglu_up_kernel(x_ref, w_hbm, o_ref, stage_a, stage_b, wa_bf, wb_bf, sems, *, layer, tail):
    n = pl.program_id(0)
    nb = pl.num_programs(0)
    tn = wa_bf.shape[1]
    F = w_hbm.shape[2] // 2

    def copies(nn, width):
        col = nn * tn
        a = pltpu.make_async_copy(w_hbm.at[layer, :, pl.ds(pl.multiple_of(col, LANES), width)],
                                  stage_a.at[:, pl.ds(0, width)], sems.at[0])
        b = pltpu.make_async_copy(w_hbm.at[layer, :, pl.ds(pl.multiple_of(F + col, LANES), width)],
                                  stage_b.at[:, pl.ds(0, width)], sems.at[1])
        return a, b

    def for_tile(nn, action):
        if tail == tn:
            for cp in copies(nn, tn):
                action(cp)
        else:
            @pl.when(nn < nb - 1)
            def _():
                for cp in copies(nn, tn):
                    action(cp)

            @pl.when(nn == nb - 1)
            def _():
                for cp in copies(nn, tail):
                    action(cp)

    @pl.when(pl.program_id(1) == 0)
    def _():
        @pl.when(n == 0)
        def _():
            for_tile(n, lambda cp: cp.start())

        for_tile(n, lambda cp: cp.wait())
        wa_bf[...] = stage_a[...].astype(BF16)
        wb_bf[...] = stage_b[...].astype(BF16)

        @pl.when(n + 1 < nb)
        def _():
            for_tile(n + 1, lambda cp: cp.start())

    x = x_ref[...]
    a = jnp.dot(x, wa_bf[...], preferred_element_type=F32)
    b = jnp.dot(x, wb_bf[...], preferred_element_type=F32)
    o_ref[...] = (_silu(a) * b).astype(o_ref.dtype)


def swiglu_up(x, w_gu, *, layer, tm, tn):
    T, D = x.shape
    F = w_gu.shape[2] // 2
    nb = pl.cdiv(F, tn)
    tail = F - (nb - 1) * tn
    assert F % LANES == 0 and (nb > 1 or tail == tn)
    return pl.pallas_call(
        functools.partial(_swiglu_up_kernel, layer=layer, tail=tail),
        grid=(nb, T // tm),
        in_specs=[pl.BlockSpec((tm, D), lambda n, m: (m, 0)), pl.BlockSpec(memory_space=pl.ANY)],
        out_specs=pl.BlockSpec((tm, tn), lambda n, m: (m, n)),
        out_shape=jax.ShapeDtypeStruct((T, F), BF16),
        scratch_shapes=[pltpu.VMEM((D, tn), w_gu.dtype), pltpu.VMEM((D, tn), w_gu.dtype),
                        pltpu.VMEM((D, tn), BF16), pltpu.VMEM((D, tn), BF16), pltpu.SemaphoreType.DMA((2,))],
        compiler_params=_params("arbitrary", "arbitrary"),
        name="swiglu_up",
    )(x, w_gu)


def dense_ffn(x, h, w_gu, w_down, *, layer, tm_up, tm_down):
    hm = swiglu_up(h, w_gu, layer=layer, tm=tm_up, tn=512)
    return matmul(hm, w_down, layer=layer, tm=tm_down, tn=512, res=x)


def _ple_kernel(hn_ref, wg_ref, p_ref, wi_ref, x_ref, o_ref, wg_bf, wi_bf):
    @pl.when(pl.program_id(1) == 0)
    def _():
        wg_bf[...] = wg_ref[...].astype(BF16)
        wi_bf[...] = wi_ref[...].astype(BF16)

    gate = jax.nn.sigmoid(jnp.dot(hn_ref[...], wg_bf[...], preferred_element_type=F32))
    emb = jnp.dot(p_ref[...].astype(BF16), wi_bf[...], preferred_element_type=F32)
    o_ref[...] = x_ref[...] + gate * emb


def ple(x, hn, p, w_gate, w_in, *, layer, tm, tn):
    T, D = x.shape
    DP = p.shape[2]
    return pl.pallas_call(
        _ple_kernel,
        grid=(D // tn, T // tm),
        in_specs=[pl.BlockSpec((tm, D), lambda n, m: (m, 0)),
                  _weight_spec(w_gate, layer, tn),
                  pl.BlockSpec((None, tm, DP), lambda n, m: (layer, m, 0)),
                  _weight_spec(w_in, layer, tn),
                  pl.BlockSpec((tm, tn), lambda n, m: (m, n))],
        out_specs=pl.BlockSpec((tm, tn), lambda n, m: (m, n)),
        out_shape=jax.ShapeDtypeStruct((T, D), F32),
        scratch_shapes=[pltpu.VMEM((D, tn), BF16), pltpu.VMEM((DP, tn), BF16)],
        compiler_params=_params("arbitrary", "arbitrary"),
        name="ple",
    )(hn, w_gate, p, w_in, x)


def _retention_kernel(lg_ref, q_ref, k_ref, v_ref, g_ref, cos_ref, sin_ref, *rest, has_state0, has_prev):
    rest = list(rest)
    s0_ref = rest.pop(0) if has_state0 else None
    prev_ref = rest.pop(0) if has_prev else None
    y_ref, sout_ref, state = rest
    n_prev = sout_ref.shape[0] - 1
    hb = pl.program_id(1)
    c = pl.program_id(2)
    C = q_ref.shape[0]
    hp = state.shape[0]
    dk, dv = RET_DK, RET_DV
    half = dk // 2

    @pl.when(c == 0)
    def _():
        if has_state0:
            state[...] = s0_ref[...]
        else:
            state[...] = jnp.zeros_like(state)

    cos = cos_ref[...]
    sin = sin_ref[...]

    def rotary(x):
        x1, x2 = x[:, :half], x[:, half:]
        return jnp.concatenate([x1 * cos - x2 * sin, x1 * sin + x2 * cos], axis=-1)

    row = lax.broadcasted_iota(jnp.int32, (C, C), 0)
    col = lax.broadcasted_iota(jnp.int32, (C, C), 1)
    rel = (row - col).astype(F32)
    rel_pos = jnp.maximum(rel, 0.0)
    idx = lax.broadcasted_iota(jnp.int32, (C, 1), 0).astype(F32)

    for j in range(hp):
        lg = lg_ref[hb * hp + j]
        q = rotary(q_ref[:, j * dk:(j + 1) * dk])
        k = rotary(k_ref[:, j * dk:(j + 1) * dk]) * dk ** -0.5
        vb = v_ref[:, j * dv:(j + 1) * dv].astype(BF16)
        decay = jnp.where(rel >= 0, jnp.exp(rel_pos * lg), 0.0)
        q_decay = jnp.exp((idx + 1.0) * lg)
        k_decay = jnp.exp((C - 1.0 - idx) * lg)
        chunk_decay = jnp.exp(jnp.zeros((1, 1), F32) + C * lg)

        qb = q.astype(BF16)
        scores = lax.dot_general(qb, k.astype(BF16), (((1,), (1,)), ((), ())), preferred_element_type=F32) * decay
        o = jnp.dot(scores.astype(BF16), vb, preferred_element_type=F32)
        st = state[j]
        o = o + jnp.dot(qb, st.astype(BF16), preferred_element_type=F32) * q_decay
        kv = lax.dot_general((k * k_decay).astype(BF16), vb, (((0,), (0,)), ((), ())), preferred_element_type=F32)
        st = st * chunk_decay + kv
        state[j] = st
        sout_ref[n_prev, j] = st

        o = o * lax.rsqrt(jnp.mean(o * o, axis=-1, keepdims=True) + EPS)
        y_ref[:, j * dv:(j + 1) * dv] = (o * _silu(g_ref[:, j * dv:(j + 1) * dv])).astype(y_ref.dtype)

    if has_prev:
        @pl.when(c == pl.num_programs(2) - 1)
        def _():
            sout_ref[:n_prev] = prev_ref[...]


def retention(qkvg, cos, sin, log_g, state0, prev_states, *, batch, seq, chunk, row0, heads_per_step, state_layer=0):
    H, dk, dv = RET_HEADS, RET_DK, RET_DV
    hp = heads_per_step
    nhb = H // hp
    nc = seq // chunk
    rb0 = row0 // chunk
    kq = nhb
    kv = (2 * H * dk) // (hp * dv)
    kg = kv + nhb

    def rows(b, c):
        return rb0 + b * nc + c

    in_specs = [
        pl.BlockSpec((chunk, hp * dk), lambda b, h, c, lg: (rows(b, c), h)),
        pl.BlockSpec((chunk, hp * dk), lambda b, h, c, lg: (rows(b, c), kq + h)),
        pl.BlockSpec((chunk, hp * dv), lambda b, h, c, lg: (rows(b, c), kv + h)),
        pl.BlockSpec((chunk, hp * dv), lambda b, h, c, lg: (rows(b, c), kg + h)),
        pl.BlockSpec((chunk, dk // 2), lambda b, h, c, lg: (rows(b, c), 0)),
        pl.BlockSpec((chunk, dk // 2), lambda b, h, c, lg: (rows(b, c), 0)),
    ]
    args = [qkvg, qkvg, qkvg, qkvg, cos, sin]
    if state0 is not None:
        in_specs.append(pl.BlockSpec((None, None, hp, dk, dv), lambda b, h, c, lg: (state_layer, b, h, 0, 0)))
        args.append(state0)
    n_prev = 0 if prev_states is None else prev_states.shape[0]
    if prev_states is not None:
        in_specs.append(pl.BlockSpec((n_prev, None, hp, dk, dv), lambda b, h, c, lg: (0, b, h, 0, 0)))
        args.append(prev_states)
    grid_spec = pltpu.PrefetchScalarGridSpec(
        num_scalar_prefetch=1,
        grid=(batch, nhb, nc),
        in_specs=in_specs,
        out_specs=[pl.BlockSpec((chunk, hp * dv), lambda b, h, c, lg: (b * nc + c, h)),
                   pl.BlockSpec((n_prev + 1, None, hp, dk, dv), lambda b, h, c, lg: (0, b, h, 0, 0))],
        scratch_shapes=[pltpu.VMEM((hp, dk, dv), F32)],
    )
    return pl.pallas_call(
        functools.partial(_retention_kernel, has_state0=state0 is not None, has_prev=prev_states is not None),
        grid_spec=grid_spec,
        out_shape=[jax.ShapeDtypeStruct((batch * seq, H * dv), BF16),
                   jax.ShapeDtypeStruct((n_prev + 1, batch, H, dk, dv), F32)],
        compiler_params=_params("arbitrary", "arbitrary", "arbitrary"),
        name="retention",
    )(log_g, *args)


def _att_prompt_kernel(q_ref, k_ref, v_ref, bias_ref, o_ref, kpad, vpad):
    S = q_ref.shape[0]
    QB = ATT_QBLOCK
    past = BAND_PAST_CHUNKS * CHUNK
    win = past + QB
    kpad[:past, :] = jnp.zeros((past, ATT_DH), BF16)
    vpad[:past, :] = jnp.zeros((past, ATT_DH), BF16)
    kpad[past:, :] = k_ref[...].astype(BF16)
    vpad[past:, :] = v_ref[...].astype(BF16)
    bias = bias_ref[...]
    qi = lax.broadcasted_iota(jnp.int32, (QB, win), 0)
    kj = lax.broadcasted_iota(jnp.int32, (QB, win), 1)
    chunk_shift = CHUNK.bit_length() - 1
    qc = jnp.right_shift(qi, chunk_shift)
    kc = jnp.right_shift(kj, chunk_shift)
    band = (kc >= qc) & (kc <= qc + BAND_PAST_CHUNKS)

    def body(i, carry):
        q0 = pl.multiple_of(i * QB, QB)
        qb = q_ref[pl.ds(q0, QB), :].astype(BF16)
        kw = kpad[pl.ds(q0, win), :]
        vw = vpad[pl.ds(q0, win), :]
        s = lax.dot_general(qb, kw, (((1,), (1,)), ((), ())), preferred_element_type=F32)
        s = s * ATT_DH ** -0.5 + bias
        valid = band & (kj >= past - q0)
        s = jnp.where(valid, s, NEG_INF)
        m = jnp.max(s, axis=-1, keepdims=True)
        e = jnp.exp(s - m)
        pr = (e / jnp.sum(e, axis=-1, keepdims=True)).astype(BF16)
        o_ref[pl.ds(q0, QB), :] = jnp.dot(pr, vw, preferred_element_type=F32).astype(o_ref.dtype)
        return carry

    lax.fori_loop(0, S // QB, body, 0, unroll=ATT_UNROLL)


def attention_prompt(qkv, bias, *, seq):
    H, dh = ATT_HEADS, ATT_DH
    past = BAND_PAST_CHUNKS * CHUNK
    return pl.pallas_call(
        _att_prompt_kernel,
        grid=(H,),
        in_specs=[pl.BlockSpec((seq, dh), lambda h: (0, h)),
                  pl.BlockSpec((seq, dh), lambda h: (0, H + h)),
                  pl.BlockSpec((seq, dh), lambda h: (0, 2 * H + h)),
                  pl.BlockSpec((None, ATT_QBLOCK, past + ATT_QBLOCK), lambda h: (h, 0, 0))],
        out_specs=pl.BlockSpec((seq, dh), lambda h: (0, h)),
        out_shape=jax.ShapeDtypeStruct((seq, H * dh), BF16),
        scratch_shapes=[pltpu.VMEM((past + seq, dh), BF16), pltpu.VMEM((past + seq, dh), BF16)],
        compiler_params=_params("arbitrary"),
        name="attention_prompt",
    )(qkv, qkv, qkv, bias)


def _att_sample_kernel(q_ref, kn_ref, vn_ref, ck_ref, cv_ref, bp_ref, bn_ref, o_ref):
    dh = ATT_DH
    H = ATT_HEADS
    W = ck_ref.shape[0] // H
    for h in range(H):
        sl = slice(h * dh, (h + 1) * dh)
        qb = q_ref[:, sl].astype(BF16)
        kn = kn_ref[:, sl].astype(BF16)
        vn = vn_ref[:, sl].astype(BF16)
        kp = ck_ref[pl.ds(h, W, stride=H), :].astype(BF16)
        vp = cv_ref[pl.ds(h, W, stride=H), :].astype(BF16)
        sp = lax.dot_general(qb, kp, (((1,), (1,)), ((), ())), preferred_element_type=F32)
        sn = lax.dot_general(qb, kn, (((1,), (1,)), ((), ())), preferred_element_type=F32)
        sp = sp * dh ** -0.5 + bp_ref[h]
        sn = sn * dh ** -0.5 + bn_ref[h]
        m = jnp.maximum(jnp.max(sp, axis=-1, keepdims=True), jnp.max(sn, axis=-1, keepdims=True))
        ep = jnp.exp(sp - m)
        en = jnp.exp(sn - m)
        den = jnp.sum(ep, axis=-1, keepdims=True) + jnp.sum(en, axis=-1, keepdims=True)
        o = jnp.dot((ep / den).astype(BF16), vp, preferred_element_type=F32)
        o = o + jnp.dot((en / den).astype(BF16), vn, preferred_element_type=F32)
        o_ref[:, sl] = o.astype(o_ref.dtype)


def attention_sample(qkv, cache_k, cache_v, bias_past, bias_new, *, batch, seq, row0, layer):
    H, dh = ATT_HEADS, ATT_DH
    W = cache_k.shape[2]
    rb0 = row0 // seq
    cache_k = cache_k.reshape(-1, dh)
    cache_v = cache_v.reshape(-1, dh)
    return pl.pallas_call(
        _att_sample_kernel,
        grid=(batch,),
        in_specs=[pl.BlockSpec((seq, H * dh), lambda b: (rb0 + b, 0)),
                  pl.BlockSpec((seq, H * dh), lambda b: (rb0 + b, 1)),
                  pl.BlockSpec((seq, H * dh), lambda b: (rb0 + b, 2)),
                  pl.BlockSpec((W * H, dh), lambda b: (layer * batch + b, 0)),
                  pl.BlockSpec((W * H, dh), lambda b: (layer * batch + b, 0)),
                  pl.BlockSpec((H, seq, W), lambda b: (0, 0, 0)),
                  pl.BlockSpec((H, seq, seq), lambda b: (0, 0, 0))],
        out_specs=pl.BlockSpec((seq, H * dh), lambda b: (b, 0)),
        out_shape=jax.ShapeDtypeStruct((batch * seq, H * dh), BF16),
        compiler_params=_params("arbitrary"),
        name="attention_sample",
    )(qkv, qkv, qkv, cache_k, cache_v, bias_past, bias_new)


def _rel_bias(table, n_q, n_k, offset):
    period = n_q + n_k
    d = jnp.arange(period)
    d = jnp.where(d < n_k, d, d - period)
    u = jnp.take(table, jnp.clip(offset - d, -REL_CLIP, REL_CLIP) + REL_CLIP, axis=0).T.astype(F32)
    H = u.shape[0]
    skew = jnp.tile(u, (1, n_q))[:, :n_q * (period - 1)].reshape(H, n_q, period - 1)
    return skew[:, :, :n_k]


def _gmlp_kernel(uv_ref, lng_ref, ws_ref, bs_ref, y_ref, vn_ref):
    DV = y_ref.shape[1]
    cg = DV // GMLP_GROUPS
    v = jax.nn.gelu(uv_ref[:, DV:])
    mu = jnp.mean(v, axis=-1, keepdims=True)
    d = v - mu
    var = jnp.mean(d * d, axis=-1, keepdims=True)
    vn = d * lax.rsqrt(var + EPS) * lng_ref[...]
    vn_ref[...] = vn
    for g in range(GMLP_GROUPS):
        sl = slice(g * cg, (g + 1) * cg)
        mixed = jnp.dot(ws_ref[g], vn[:, sl].astype(BF16), preferred_element_type=F32) + bs_ref[g]
        y_ref[:, sl] = (jax.nn.gelu(uv_ref[:, sl]) * mixed).astype(y_ref.dtype)


def gmlp_gate(uv, ln_g, ws, bs, *, n_prompt_tiles, tile):
    T = uv.shape[0]
    DV = uv.shape[1] // 2
    G = GMLP_GROUPS
    nt = T // tile
    last = n_prompt_tiles - 1
    return pl.pallas_call(
        _gmlp_kernel,
        grid=(nt,),
        in_specs=[pl.BlockSpec((tile, 2 * DV), lambda i: (i, 0)),
                  pl.BlockSpec((1, DV), lambda i: (0, 0)),
                  pl.BlockSpec((None, G, tile, tile), lambda i: (i // n_prompt_tiles, 0, 0, 0)),
                  pl.BlockSpec((None, G, tile, 1), lambda i: (i // n_prompt_tiles, 0, 0, 0))],
        out_specs=[pl.BlockSpec((tile, DV), lambda i: (i, 0)),
                   pl.BlockSpec((tile, DV), lambda i: (jnp.maximum(i - last, 0), 0))],
        out_shape=[jax.ShapeDtypeStruct((T, DV), BF16),
                   jax.ShapeDtypeStruct(((nt - last) * tile, DV), F32)],
        compiler_params=_params("arbitrary"),
        name="gmlp_gate",
    )(uv, ln_g.reshape(1, DV), ws, bs)


def _router_kernel(h_ref, w_ref, idx_ref, gate_ref):
    logits = jnp.dot(h_ref[...], w_ref[...].astype(BF16), preferred_element_type=F32)
    lane = lax.broadcasted_iota(jnp.int32, logits.shape, 1)
    lane_f = lane.astype(F32)
    l1 = jnp.where(lane < N_EXPERTS, logits, -jnp.inf)
    v1 = jnp.max(l1, axis=-1, keepdims=True)
    i1 = jnp.min(jnp.where(l1 == v1, lane_f, float(LANES)), axis=-1, keepdims=True)
    l2 = jnp.where(lane_f == i1, -jnp.inf, l1)
    v2 = jnp.max(l2, axis=-1, keepdims=True)
    i2 = jnp.min(jnp.where(l2 == v2, lane_f, float(LANES)), axis=-1, keepdims=True)
    e2 = jnp.exp(v2 - v1)
    den = 1.0 + e2
    idx_ref[...] = jnp.where(lane == 0, i1, jnp.where(lane == 1, i2, 0.0)).astype(jnp.int32)
    gate_ref[...] = jnp.where(lane == 0, 1.0 / den, jnp.where(lane == 1, e2 / den, 0.0))


def router(h, w_router_padded, *, tm):
    T, D = h.shape
    return pl.pallas_call(
        _router_kernel,
        grid=(T // tm,),
        in_specs=[pl.BlockSpec((tm, D), lambda i: (i, 0)), pl.BlockSpec((D, LANES), lambda i: (0, 0))],
        out_specs=[pl.BlockSpec((tm, LANES), lambda i: (i, 0)), pl.BlockSpec((tm, LANES), lambda i: (i, 0))],
        out_shape=[jax.ShapeDtypeStruct((T, LANES), jnp.int32), jax.ShapeDtypeStruct((T, LANES), F32)],
        compiler_params=_params("arbitrary"),
        name="router",
    )(h, w_router_padded)


def _gather_kernel(lo_ref, hi_ref, p0_ref, p1_ref, h_ref, o_ref, acc):
    m = pl.program_id(0)
    tm = o_ref.shape[0]
    sb = GATHER_SRC_BLOCK
    row = m * tm + lax.broadcasted_iota(jnp.int32, (tm, sb), 0)
    acc[...] = jnp.zeros_like(acc)

    def body(b, carry):
        select = jnp.where(p0_ref[b] == row, 1.0, jnp.where(p1_ref[b] == row, 1.0, 0.0)).astype(h_ref.dtype)
        acc[...] += jnp.dot(select, h_ref[pl.ds(pl.multiple_of(b * sb, sb), sb), :], preferred_element_type=F32)
        return carry

    lax.fori_loop(lo_ref[m], hi_ref[m] + 1, body, 0)
    o_ref[...] = acc[...].astype(o_ref.dtype)


def gather_rows(h, pos0, pos1, blk_lo, blk_hi, n_rows, *, tm):
    T, D = h.shape
    nsb = T // GATHER_SRC_BLOCK
    grid_spec = pltpu.PrefetchScalarGridSpec(
        num_scalar_prefetch=2,
        grid=(n_rows // tm,),
        in_specs=[pl.BlockSpec((nsb, 1, GATHER_SRC_BLOCK), lambda i, lo, hi: (0, 0, 0)),
                  pl.BlockSpec((nsb, 1, GATHER_SRC_BLOCK), lambda i, lo, hi: (0, 0, 0)),
                  pl.BlockSpec((T, D), lambda i, lo, hi: (0, 0), pipeline_mode=pl.Buffered(1))],
        out_specs=pl.BlockSpec((tm, D), lambda i, lo, hi: (i, 0)),
        scratch_shapes=[pltpu.VMEM((tm, D), F32)],
    )
    return pl.pallas_call(
        _gather_kernel,
        grid_spec=grid_spec,
        out_shape=jax.ShapeDtypeStruct((n_rows, D), h.dtype),
        compiler_params=_params("arbitrary"),
        name="moe_gather",
    )(blk_lo, blk_hi, pos0.reshape(nsb, 1, GATHER_SRC_BLOCK), pos1.reshape(nsb, 1, GATHER_SRC_BLOCK), h)


def _on_valid_rows(valid, o_ref, compute):
    tm = o_ref.shape[0]
    half = tm // 2

    @pl.when(valid > half)
    def _():
        compute(slice(0, tm))

    @pl.when((valid > 0) & (valid <= half))
    def _():
        compute(slice(0, half))
        o_ref[half:, :] = jnp.zeros((tm - half, o_ref.shape[1]), o_ref.dtype)

    @pl.when(valid == 0)
    def _():
        o_ref[...] = jnp.zeros_like(o_ref)


def _group_weights(te_ref, first_ref, nxt_ref, wrap_ref, copies, cast):
    n = pl.program_id(0)
    m = pl.program_id(1)

    @pl.when(first_ref[m] == 1)
    def _():
        @pl.when((n == 0) & (m == 0))
        def _():
            for cp in copies(te_ref[0], 0):
                cp.start()

        for cp in copies(te_ref[m], n):
            cp.wait()
        cast()
        n_next = n + wrap_ref[m]

        @pl.when(n_next < pl.num_programs(0))
        def _():
            for cp in copies(nxt_ref[m], n_next):
                cp.start()


def _moe_up_kernel(te_ref, first_ref, valid_ref, nxt_ref, wrap_ref, x_ref, w_hbm, o_ref,
                   stage_a, stage_b, wa_bf, wb_bf, sems):
    m = pl.program_id(1)
    tn = wa_bf.shape[1]
    F = w_hbm.shape[2] // 2

    def copies(e, n):
        col = pl.multiple_of(n * tn, tn)
        return (pltpu.make_async_copy(w_hbm.at[e, :, pl.ds(col, tn)], stage_a, sems.at[0]),
                pltpu.make_async_copy(w_hbm.at[e, :, pl.ds(pl.multiple_of(F + col, tn), tn)], stage_b, sems.at[1]))

    def cast():
        wa_bf[...] = stage_a[...].astype(BF16)
        wb_bf[...] = stage_b[...].astype(BF16)

    _group_weights(te_ref, first_ref, nxt_ref, wrap_ref, copies, cast)

    def compute(rows):
        x = x_ref[rows, :]
        a = jnp.dot(x, wa_bf[...], preferred_element_type=F32)
        b = jnp.dot(x, wb_bf[...], preferred_element_type=F32)
        o_ref[rows, :] = (_silu(a) * b).astype(o_ref.dtype)

    _on_valid_rows(valid_ref[m], o_ref, compute)


def moe_up(xs, w_gu, plan, *, tm, tn):
    P, D = xs.shape
    F = w_gu.shape[2] // 2
    grid_spec = pltpu.PrefetchScalarGridSpec(
        num_scalar_prefetch=len(plan),
        grid=(F // tn, P // tm),
        in_specs=[pl.BlockSpec((tm, D), lambda n, m, *_: (m, 0)),
                  pl.BlockSpec(memory_space=pl.ANY)],
        out_specs=pl.BlockSpec((tm, tn), lambda n, m, *_: (m, n)),
        scratch_shapes=[pltpu.VMEM((D, tn), w_gu.dtype), pltpu.VMEM((D, tn), w_gu.dtype),
                        pltpu.VMEM((D, tn), BF16), pltpu.VMEM((D, tn), BF16), pltpu.SemaphoreType.DMA((2,))],
    )
    return pl.pallas_call(
        _moe_up_kernel,
        grid_spec=grid_spec,
        out_shape=jax.ShapeDtypeStruct((P, F), BF16),
        compiler_params=_params("arbitrary", "arbitrary"),
        name="moe_up",
    )(*plan, xs, w_gu)


def _moe_down_kernel(te_ref, first_ref, valid_ref, nxt_ref, wrap_ref, x_ref, w_hbm, o_ref, stage, w_bf, sem):
    m = pl.program_id(1)
    tn = w_bf.shape[1]

    def copies(e, n):
        return (pltpu.make_async_copy(w_hbm.at[e, :, pl.ds(pl.multiple_of(n * tn, tn), tn)], stage, sem),)

    def cast():
        w_bf[...] = stage[...].astype(BF16)

    _group_weights(te_ref, first_ref, nxt_ref, wrap_ref, copies, cast)

    def compute(rows):
        o_ref[rows, :] = jnp.dot(x_ref[rows, :], w_bf[...], preferred_element_type=F32)

    _on_valid_rows(valid_ref[m], o_ref, compute)


def moe_down(hm, w_down, plan, *, tm, tn):
    P, F = hm.shape
    D = w_down.shape[2]
    grid_spec = pltpu.PrefetchScalarGridSpec(
        num_scalar_prefetch=len(plan),
        grid=(D // tn, P // tm),
        in_specs=[pl.BlockSpec((tm, F), lambda n, m, *_: (m, 0)),
                  pl.BlockSpec(memory_space=pl.ANY)],
        out_specs=pl.BlockSpec((tm, tn), lambda n, m, *_: (m, n)),
        scratch_shapes=[pltpu.VMEM((F, tn), w_down.dtype), pltpu.VMEM((F, tn), BF16), pltpu.SemaphoreType.DMA(())],
    )
    return pl.pallas_call(
        _moe_down_kernel,
        grid_spec=grid_spec,
        out_shape=jax.ShapeDtypeStruct((P, D), F32),
        compiler_params=_params("arbitrary", "arbitrary"),
        name="moe_down",
    )(*plan, hm, w_down)


def _row_copy(y_hbm, buf, src_row, dst_row, sem):
    return pltpu.make_async_copy(y_hbm.at[pl.ds(src_row, 1), :], buf.at[pl.ds(dst_row, 1), :], sem)


def _combine_norm_kernel(p0_ref, p1_ref, x_ref, gate_ref, g_ref, y_hbm, xo_ref, hn_ref, buf0, buf1, sems):
    tm = x_ref.shape[0]
    base = pl.program_id(0) * tm

    def start(r, carry):
        _row_copy(y_hbm, buf0, p0_ref[base + r], r, sems.at[0]).start(priority=0)
        _row_copy(y_hbm, buf1, p1_ref[base + r], r, sems.at[1]).start(priority=DMA_PRIORITIES - 1)
        return carry

    def wait(r, carry):
        _row_copy(y_hbm, buf0, 0, r, sems.at[0]).wait()
        _row_copy(y_hbm, buf1, 0, r, sems.at[1]).wait()
        return carry

    lax.fori_loop(0, tm, start, 0)
    lax.fori_loop(0, tm, wait, 0)
    gate = gate_ref[...]
    xn = x_ref[...] + (gate[:, 0:1] * buf0[...] + gate[:, 1:2] * buf1[...])
    xo_ref[...] = xn
    hn = xn * lax.rsqrt(jnp.mean(xn * xn, axis=-1, keepdims=True) + EPS) * g_ref[...]
    hn_ref[...] = hn.astype(hn_ref.dtype)


def moe_combine_norm(x, gates, ys, pos0, pos1, g, *, tm):
    T, D = x.shape
    grid_spec = pltpu.PrefetchScalarGridSpec(
        num_scalar_prefetch=2,
        grid=(T // tm,),
        in_specs=[pl.BlockSpec((tm, D), lambda i, a, b: (i, 0)),
                  pl.BlockSpec((tm, LANES), lambda i, a, b: (i, 0)),
                  pl.BlockSpec((1, D), lambda i, a, b: (0, 0)),
                  pl.BlockSpec(memory_space=pl.ANY)],
        out_specs=[pl.BlockSpec((tm, D), lambda i, a, b: (i, 0)), pl.BlockSpec((tm, D), lambda i, a, b: (i, 0))],
        scratch_shapes=[pltpu.VMEM((tm, D), F32), pltpu.VMEM((tm, D), F32), pltpu.SemaphoreType.DMA((2,))],
    )
    return pl.pallas_call(
        _combine_norm_kernel,
        grid_spec=grid_spec,
        out_shape=[jax.ShapeDtypeStruct((T, D), F32), jax.ShapeDtypeStruct((T, D), BF16)],
        compiler_params=_params("arbitrary"),
        name="moe_combine_norm",
    )(pos0, pos1, x, gates, g.reshape(1, D), ys)


def _route_plan(idx, tm):
    T = idx.shape[0]
    E = N_EXPERTS
    n_tiles = (TOP_K * T) // tm + E
    experts = jnp.arange(E, dtype=jnp.int32)
    flat_e = idx.reshape(-1)
    onehot = (flat_e[:, None] == experts[None, :]).astype(jnp.int32)
    csum = jnp.cumsum(onehot, axis=0)
    rank = jnp.sum((csum - onehot) * onehot, axis=1)
    counts = csum[-1]
    tiles_e = (counts + tm - 1) // tm
    tile_end = jnp.cumsum(tiles_e)
    tile_start = tile_end - tiles_e
    pos = (jnp.sum(onehot * tile_start[None, :], axis=1) * tm + rank).astype(jnp.int32)
    n_used = tile_end[-1]
    m_ids = jnp.arange(n_tiles, dtype=jnp.int32)
    te = jnp.sum((jnp.minimum(m_ids, n_used - 1)[:, None] >= tile_end[None, :]).astype(jnp.int32), axis=1)
    first = ((m_ids == 0) | (te != jnp.roll(te, 1))) & (m_ids < n_used)
    te_onehot = (te[:, None] == experts[None, :]).astype(jnp.int32)
    rows_before = (m_ids - jnp.sum(te_onehot * tile_start[None, :], axis=1)) * tm
    valid = jnp.clip(jnp.sum(te_onehot * counts[None, :], axis=1) - rows_before, 0, tm)
    valid = jnp.where(m_ids < n_used, valid, 0).astype(jnp.int32)
    nonempty = jnp.where(tiles_e > 0, experts, E)
    after = jnp.min(jnp.where(experts[None, :] > experts[:, None], nonempty[None, :], E), axis=1)
    nxt_e = jnp.where(after < E, after, jnp.min(nonempty))
    nxt = jnp.sum(te_onehot * nxt_e[None, :], axis=1).astype(jnp.int32)
    wrap = jnp.sum(te_onehot * (after == E).astype(jnp.int32)[None, :], axis=1).astype(jnp.int32)
    csum_tile = jnp.sum(csum[:, :, None] * te_onehot.T[None, :, :], axis=1)
    pair_first = jnp.sum((csum_tile <= rows_before[None, :]).astype(jnp.int32), axis=0)
    pair_last = jnp.sum((csum_tile <= (rows_before + valid - 1)[None, :]).astype(jnp.int32), axis=0)
    blk_lo = jnp.where(valid > 0, pair_first // (TOP_K * GATHER_SRC_BLOCK), 0).astype(jnp.int32)
    blk_hi = jnp.where(valid > 0, pair_last // (TOP_K * GATHER_SRC_BLOCK), -1).astype(jnp.int32)
    return (pos[0::TOP_K], pos[1::TOP_K], blk_lo, blk_hi,
            (te.astype(jnp.int32), first.astype(jnp.int32), valid, nxt, wrap))


def moe(x, h, w_router, w_gu, w_down, g_next, *, layer, tm_tok):
    T = x.shape[0]
    E = N_EXPERTS
    w_r = jnp.pad(w_router[layer], ((0, 0), (0, LANES - E)))
    idx, gates = router(h, w_r, tm=tm_tok)
    pos0, pos1, blk_lo, blk_hi, (te, first, valid, nxt, wrap) = _route_plan(idx[:, :TOP_K], MOE_TM)
    plan = (te + layer * E, first, valid, nxt + layer * E, wrap)
    n_rows = te.shape[0] * MOE_TM
    xs = gather_rows(h, pos0, pos1, blk_lo, blk_hi, n_rows, tm=MOE_TM)
    hm = moe_up(xs, w_gu.reshape((-1,) + w_gu.shape[2:]), plan, tm=MOE_TM, tn=1024)
    ys = moe_down(hm, w_down.reshape((-1,) + w_down.shape[2:]), plan, tm=MOE_TM, tn=512)
    return moe_combine_norm(x, gates, ys, pos0, pos1, g_next, tm=tm_tok)


def kernel(x_prompt, x_sample, state_ret, cache_k, cache_v, p_prompt, p_sample, g_mix, g_ffn, g_ple, g_final,
           ret_w_in, ret_w_out, att_w_in, att_w_out, att_rel_bias, gmlp_w_in, gmlp_ln_g, gmlp_w_s, gmlp_b_s,
           gmlp_w_out, ffn_w_gu, ffn_w_down, moe_w_router, moe_w_gu, moe_w_down, ple_w_in, ple_w_gate):
    B, S, D = x_prompt.shape
    BS, L, _ = x_sample.shape
    depth = g_mix.shape[0]
    NP, NS = B * S, BS * L
    T = NP + NS
    assert B == 1 and S % RET_CHUNK == 0 and S % ATT_QBLOCK == 0 and L <= CHUNK
    TM = T // 8
    TMS = T // 16

    x = jnp.concatenate([x_prompt.reshape(NP, D), x_sample.reshape(NS, D)], axis=0)
    p_all = jnp.concatenate([p_prompt.reshape(depth, NP, -1), p_sample.reshape(depth, NS, -1)], axis=1)

    pos = jnp.concatenate([jnp.arange(S, dtype=jnp.int32), jnp.tile(PAST_LEN + jnp.arange(L, dtype=jnp.int32), BS)])
    freqs = ROPE_BASE ** (-jnp.arange(0, RET_DK, 2, dtype=F32) / RET_DK)
    ang = pos.astype(F32)[:, None] * freqs[None, :]
    cos, sin = jnp.cos(ang), jnp.sin(ang)
    log_g = jnp.log1p(-jnp.exp2(-5.0 - jnp.arange(RET_HEADS, dtype=F32)))

    ret_p, ret_s = None, None
    k_p, v_p, k_s, v_s, gv = [], [], [], [], []
    for i in range(depth):
        kind, slot = i % 3, i // 3
        (h,) = rmsnorm(x, g_mix[i], [BF16], TMS)
        if kind == 0:
            qkvg = matmul(h, ret_w_in, layer=slot, tm=TM, tn=1024)
            y_p, ret_p = retention(qkvg, cos, sin, log_g, None, ret_p, batch=B, seq=S, chunk=RET_CHUNK, row0=0,
                                   heads_per_step=RET_HEADS // 2)
            y_s, ret_s = retention(qkvg, cos, sin, log_g, state_ret, ret_s, batch=BS, seq=L, chunk=L, row0=NP,
                                   heads_per_step=RET_HEADS, state_layer=slot)
            y = jnp.concatenate([y_p, y_s], axis=0)
            x = matmul(y, ret_w_out, layer=slot, tm=TMS, tn=512, res=x)
        elif kind == 1:
            HD = ATT_HEADS * ATT_DH
            W = cache_k.shape[2]
            past = BAND_PAST_CHUNKS * CHUNK
            qkv = matmul(h, att_w_in, layer=slot, tm=TM, tn=1024)
            bias_p = _rel_bias(att_rel_bias[slot], ATT_QBLOCK, past + ATT_QBLOCK, past)
            bias_s = _rel_bias(att_rel_bias[slot], L, W + L, W)
            o_p = attention_prompt(qkv, bias_p, seq=S)
            o_s = attention_sample(qkv, cache_k, cache_v, bias_s[:, :, :W], bias_s[:, :, W:],
                                   batch=BS, seq=L, row0=NP, layer=slot)
            w = min(past, S)
            k_p.append(qkv[NP - w:NP, HD:2 * HD].reshape(B, w, ATT_HEADS, ATT_DH))
            v_p.append(qkv[NP - w:NP, 2 * HD:].reshape(B, w, ATT_HEADS, ATT_DH))
            k_s.append(qkv[NP:, HD:2 * HD].reshape(BS, L, ATT_HEADS, ATT_DH))
            v_s.append(qkv[NP:, 2 * HD:].reshape(BS, L, ATT_HEADS, ATT_DH))
            o = jnp.concatenate([o_p, o_s], axis=0)
            x = matmul(o, att_w_out, layer=slot, tm=TM, tn=1024, res=x)
        else:
            tile = GMLP_CHUNK
            uv = matmul(h, gmlp_w_in, layer=slot, tm=TM, tn=1024)
            r = jnp.arange(tile)
            ws_p = jnp.where((r[:, None] >= r[None, :])[None], gmlp_w_s[slot][:, :tile, :tile], 0.0)
            same = (r[:, None] // L == r[None, :] // L) & (r[:, None] >= r[None, :])
            ws_s = jnp.where(same[None], jnp.tile(gmlp_w_s[slot][:, :L, :L], (1, tile // L, tile // L)), 0.0)
            bs_p = gmlp_b_s[slot][:, :tile]
            bs_s = jnp.tile(gmlp_b_s[slot][:, :L], (1, tile // L))
            ws = jnp.stack([ws_p, ws_s]).astype(BF16)
            bs = jnp.stack([bs_p, bs_s])[..., None]
            yg, vn = gmlp_gate(uv, gmlp_ln_g[slot], ws, bs, n_prompt_tiles=NP // tile, tile=tile)
            gv.append((vn[:tile].reshape(B, tile, -1), vn[tile:].reshape(BS, L, -1)))
            x = matmul(yg, gmlp_w_out, layer=slot, tm=TMS, tn=512, res=x)

        if i % 2 == 0:
            (h,) = rmsnorm(x, g_ffn[i], [BF16], TMS)
            x = dense_ffn(x, h, ffn_w_gu, ffn_w_down, layer=i // 2, tm_up=TM, tm_down=TMS)
            (hn,) = rmsnorm(x, g_ple[i], [BF16], TMS)
        else:
            (h,) = rmsnorm(x, g_ffn[i], [BF16], TMS)
            x, hn = moe(x, h, moe_w_router, moe_w_gu, moe_w_down, g_ple[i], layer=i // 2, tm_tok=TMS)

        x = ple(x, hn, p_all, ple_w_gate, ple_w_in, layer=i, tm=TMS, tn=1024)

    (y,) = rmsnorm(x, g_final, [F32], TMS)
    y_prompt = y[:NP].reshape(B, S, D)
    y_sample = y[NP:].reshape(BS, L, D)
    return (y_prompt, y_sample, ret_p, ret_s, jnp.stack(k_p), jnp.stack(v_p),
            jnp.stack(k_s), jnp.stack(v_s), jnp.stack([a for a, _ in gv]), jnp.stack([b for _, b in gv]))
```

```python
import functools

import jax
import jax.numpy as jnp
from jax import lax
from jax.experimental import pallas as pl
from jax.experimental.pallas import tpu as pltpu

F32 = jnp.float32
BF16 = jnp.bfloat16

EPS = 1e-6
NEG_INF = -1e30
CHUNK = 64
ROPE_BASE = 10000.0
RET_HEADS, RET_DK, RET_DV = 8, 256, 512
ATT_HEADS, ATT_DH = 16, 128
BAND_PAST_CHUNKS = 8
REL_CLIP = 128
GMLP_CHUNK, GMLP_GROUPS = 128, 8
N_EXPERTS, TOP_K = 8, 2
PAST_LEN = 2048

LANES = 128
DMA_PRIORITIES = 2
V7X_VMEM_LIMIT_BYTES = 60 * 1000 * 1024

RET_CHUNK = 256
ATT_QBLOCK = 4 * CHUNK
ATT_UNROLL = 4
MOE_TM = 512
GATHER_SRC_BLOCK = 512


def _params(*sem):
    return pltpu.CompilerParams(dimension_semantics=sem, vmem_limit_bytes=V7X_VMEM_LIMIT_BYTES)


def _silu(x):
    return x * jax.nn.sigmoid(x)


def _rmsnorm_kernel(x_ref, g_ref, *o_refs):
    x = x_ref[...]
    y = x * lax.rsqrt(jnp.mean(x * x, axis=-1, keepdims=True) + EPS) * g_ref[...]
    for o_ref in o_refs:
        o_ref[...] = y.astype(o_ref.dtype)


def rmsnorm(x, g, dtypes, tm):
    T, D = x.shape
    out_specs = [pl.BlockSpec((tm, D), lambda i: (i, 0)) for _ in dtypes]
    out_shape = [jax.ShapeDtypeStruct((T, D), dt) for dt in dtypes]
    return pl.pallas_call(
        _rmsnorm_kernel,
        grid=(T // tm,),
        in_specs=[pl.BlockSpec((tm, D), lambda i: (i, 0)), pl.BlockSpec((1, D), lambda i: (0, 0))],
        out_specs=out_specs,
        out_shape=out_shape,
        compiler_params=_params("arbitrary"),
        name="rmsnorm",
    )(x, g.reshape(1, D))


def _mm_kernel(a_ref, *rest, n_head, has_res, cast_w):
    rest = list(rest)
    t_ref = rest.pop(0) if n_head is not None else None
    w_ref = rest.pop(0)
    r_ref = rest.pop(0) if has_res else None
    o_ref, *scratch = rest
    a = a_ref[...]
    if n_head is not None:
        a = jnp.where(pl.program_id(1) < n_head, a, t_ref[...])
    if cast_w:
        w_bf = scratch[0]

        @pl.when(pl.program_id(1) == 0)
        def _():
            w_bf[...] = w_ref[...].astype(BF16)

        w = w_bf[...]
    else:
        w = w_ref[...]
    acc = jnp.dot(a, w, preferred_element_type=F32)
    if has_res:
        acc = r_ref[...] + acc
    o_ref[...] = acc.astype(o_ref.dtype)


def _weight_spec(w, layer, tn):
    K = w.shape[-2]
    if w.ndim == 2:
        return pl.BlockSpec((K, tn), lambda n, m: (0, n))
    return pl.BlockSpec((None, K, tn), lambda n, m: (layer, 0, n))


def matmul(a, w, *, tm, tn, layer=0, out_dtype=F32, res=None, a_tail=None):
    K = a.shape[1]
    N = w.shape[-1]
    cast_w = w.dtype != BF16
    n_head = None
    M = a.shape[0]
    in_specs = [pl.BlockSpec((tm, K), lambda n, m: (m, 0))]
    args = [a]
    if a_tail is not None:
        assert a.shape[0] % tm == 0 and a_tail.shape[0] % tm == 0
        n_head = a.shape[0] // tm
        M += a_tail.shape[0]
        in_specs = [pl.BlockSpec((tm, K), lambda n, m: (jnp.minimum(m, n_head - 1), 0)),
                    pl.BlockSpec((tm, K), lambda n, m: (jnp.maximum(m - n_head, 0), 0))]
        args.append(a_tail)
    in_specs.append(_weight_spec(w, layer, tn))
    args.append(w)
    if res is not None:
        in_specs.append(pl.BlockSpec((tm, tn), lambda n, m: (m, n)))
        args.append(res)
    return pl.pallas_call(
        functools.partial(_mm_kernel, n_head=n_head, has_res=res is not None, cast_w=cast_w),
        grid=(N // tn, M // tm),
        in_specs=in_specs,
        out_specs=pl.BlockSpec((tm, tn), lambda n, m: (m, n)),
        out_shape=jax.ShapeDtypeStruct((M, N), out_dtype),
        scratch_shapes=[pltpu.VMEM((K, tn), BF16)] if cast_w else [],
        compiler_params=_params("arbitrary", "arbitrary"),
        name="matmul",
    )(*args)


def _swiglu_up_kernel(x_ref, w_hbm, o_ref, stage_a, stage_b, wa_bf, wb_bf, sems, *, layer, tail):
    n = pl.program_id(0)
    nb = pl.num_programs(0)
    tn = wa_bf.shape[1]
    F = w_hbm.shape[2] // 2

    def copies(nn, width):
        col = nn * tn
        a = pltpu.make_async_copy(w_hbm.at[layer, :, pl.ds(pl.multiple_of(col, LANES), width)],
                                  stage_a.at[:, pl.ds(0, width)], sems.at[0])
        b = pltpu.make_async_copy(w_hbm.at[layer, :, pl.ds(pl.multiple_of(F + col, LANES), width)],
                                  stage_b.at[:, pl.ds(0, width)], sems.at[1])
        return a, b

    def for_tile(nn, action):
        if tail == tn:
            for cp in copies(nn, tn):
                action(cp)
        else:
            @pl.when(nn < nb - 1)
            def _():
                for cp in copies(nn, tn):
                    action(cp)

            @pl.when(nn == nb - 1)
            def _():
                for cp in copies(nn, tail):
                    action(cp)

    @pl.when(pl.program_id(1) == 0)
    def _():
        @pl.when(n == 0)
        def _():
            for_tile(n, lambda cp: cp.start())

        for_tile(n, lambda cp: cp.wait())
        wa_bf[...] = stage_a[...].astype(BF16)
        wb_bf[...] = stage_b[...].astype(BF16)

        @pl.when(n + 1 < nb)
        def _():
            for_tile(n + 1, lambda cp: cp.start())

    x = x_ref[...]
    a = jnp.dot(x, wa_bf[...], preferred_element_type=F32)
    b = jnp.dot(x, wb_bf[...], preferred_element_type=F32)
    o_ref[...] = (_silu(a) * b).astype(o_ref.dtype)


def swiglu_up(x, w_gu, *, layer, tm, tn):
    T, D = x.shape
    F = w_gu.shape[2] // 2
    nb = pl.cdiv(F, tn)
    tail = F - (nb - 1) * tn
    assert F % LANES == 0 and (nb > 1 or tail == tn)
    return pl.pallas_call(
        functools.partial(_swiglu_up_kernel, layer=layer, tail=tail),
        grid=(nb, T // tm),
        in_specs=[pl.BlockSpec((tm, D), lambda n, m: (m, 0)), pl.BlockSpec(memory_space=pl.ANY)],
        out_specs=pl.BlockSpec((tm, tn), lambda n, m: (m, n)),
        out_shape=jax.ShapeDtypeStruct((T, F), BF16),
        scratch_shapes=[pltpu.VMEM((D, tn), w_gu.dtype), pltpu.VMEM((D, tn), w_gu.dtype),
                        pltpu.VMEM((D, tn), BF16), pltpu.VMEM((D, tn), BF16), pltpu.SemaphoreType.DMA((2,))],
        compiler_params=_params("arbitrary", "arbitrary"),
        name="swiglu_up",
    )(x, w_gu)


def dense_ffn(x, h, w_gu, w_down, *, layer, tm_up, tm_down):
    hm = swiglu_up(h, w_gu, layer=layer, tm=tm_up, tn=512)
    return matmul(hm, w_down, layer=layer, tm=tm_down, tn=512, res=x)


def _ple_norm_kernel(hn_ref, wg_ref, p_ref, wi_ref, x_ref, gn_ref, o_ref, hnext_ref, wg_bf, wi_bf):
    @pl.when(pl.program_id(0) == 0)
    def _():
        wg_bf[...] = wg_ref[...].astype(BF16)
        wi_bf[...] = wi_ref[...].astype(BF16)

    gate = jax.nn.sigmoid(jnp.dot(hn_ref[...], wg_bf[...], preferred_element_type=F32))
    emb = jnp.dot(p_ref[...].astype(BF16), wi_bf[...], preferred_element_type=F32)
    xn = x_ref[...] + gate * emb
    o_ref[...] = xn
    hnext = xn * lax.rsqrt(jnp.mean(xn * xn, axis=-1, keepdims=True) + EPS) * gn_ref[...]
    hnext_ref[...] = hnext.astype(hnext_ref.dtype)


def ple_norm(x, hn, p, w_gate, w_in, g_next, next_dtype, *, layer, tm):
    T, D = x.shape
    DP = p.shape[2]
    resident = pl.Buffered(1)
    return pl.pallas_call(
        _ple_norm_kernel,
        grid=(T // tm,),
        in_specs=[pl.BlockSpec((tm, D), lambda m: (m, 0)),
                  pl.BlockSpec((None, D, D), lambda m: (layer, 0, 0), pipeline_mode=resident),
                  pl.BlockSpec((None, tm, DP), lambda m: (layer, m, 0)),
                  pl.BlockSpec((None, DP, D), lambda m: (layer, 0, 0), pipeline_mode=resident),
                  pl.BlockSpec((tm, D), lambda m: (m, 0)),
                  pl.BlockSpec((1, D), lambda m: (0, 0))],
        out_specs=[pl.BlockSpec((tm, D), lambda m: (m, 0)), pl.BlockSpec((tm, D), lambda m: (m, 0))],
        out_shape=[jax.ShapeDtypeStruct((T, D), F32), jax.ShapeDtypeStruct((T, D), next_dtype)],
        scratch_shapes=[pltpu.VMEM((D, D), BF16), pltpu.VMEM((DP, D), BF16)],
        compiler_params=_params("arbitrary"),
        name="ple_norm",
    )(hn, w_gate, p, w_in, x, g_next.reshape(1, D))


def _retention_kernel(lg_ref, q_ref, k_ref, v_ref, g_ref, cos_ref, sin_ref, *rest, has_state0, has_prev):
    rest = list(rest)
    s0_ref = rest.pop(0) if has_state0 else None
    prev_ref = rest.pop(0) if has_prev else None
    y_ref, sout_ref, state = rest
    n_prev = sout_ref.shape[0] - 1
    hb = pl.program_id(1)
    c = pl.program_id(2)
    C = q_ref.shape[0]
    hp = state.shape[0]
    dk, dv = RET_DK, RET_DV
    half = dk // 2

    @pl.when(c == 0)
    def _():
        if has_state0:
            state[...] = s0_ref[...]
        else:
            state[...] = jnp.zeros_like(state)

    cos = cos_ref[...]
    sin = sin_ref[...]

    def rotary(x):
        x1, x2 = x[:, :half], x[:, half:]
        return jnp.concatenate([x1 * cos - x2 * sin, x1 * sin + x2 * cos], axis=-1)

    row = lax.broadcasted_iota(jnp.int32, (C, C), 0)
    col = lax.broadcasted_iota(jnp.int32, (C, C), 1)
    rel = (row - col).astype(F32)
    rel_pos = jnp.maximum(rel, 0.0)
    idx = lax.broadcasted_iota(jnp.int32, (C, 1), 0).astype(F32)

    for j in range(hp):
        lg = lg_ref[hb * hp + j]
        q = rotary(q_ref[:, j * dk:(j + 1) * dk])
        k = rotary(k_ref[:, j * dk:(j + 1) * dk]) * dk ** -0.5
        vb = v_ref[:, j * dv:(j + 1) * dv].astype(BF16)
        decay = jnp.where(rel >= 0, jnp.exp(rel_pos * lg), 0.0)
        q_decay = jnp.exp((idx + 1.0) * lg)
        k_decay = jnp.exp((C - 1.0 - idx) * lg)
        chunk_decay = jnp.exp(jnp.zeros((1, 1), F32) + C * lg)

        qb = q.astype(BF16)
        scores = lax.dot_general(qb, k.astype(BF16), (((1,), (1,)), ((), ())), preferred_element_type=F32) * decay
        o = jnp.dot(scores.astype(BF16), vb, preferred_element_type=F32)
        st = state[j]
        o = o + jnp.dot(qb, st.astype(BF16), preferred_element_type=F32) * q_decay
        kv = lax.dot_general((k * k_decay).astype(BF16), vb, (((0,), (0,)), ((), ())), preferred_element_type=F32)
        st = st * chunk_decay + kv
        state[j] = st
        sout_ref[n_prev, j] = st

        o = o * lax.rsqrt(jnp.mean(o * o, axis=-1, keepdims=True) + EPS)
        y_ref[:, j * dv:(j + 1) * dv] = (o * _silu(g_ref[:, j * dv:(j + 1) * dv])).astype(y_ref.dtype)

    if has_prev:
        @pl.when(c == pl.num_programs(2) - 1)
        def _():
            sout_ref[:n_prev] = prev_ref[...]


def retention(qkvg, cos, sin, log_g, state0, prev_states, *, batch, seq, chunk, row0, heads_per_step, state_layer=0):
    H, dk, dv = RET_HEADS, RET_DK, RET_DV
    hp = heads_per_step
    nhb = H // hp
    nc = seq // chunk
    rb0 = row0 // chunk
    kq = nhb
    kv = (2 * H * dk) // (hp * dv)
    kg = kv + nhb

    def rows(b, c):
        return rb0 + b * nc + c

    in_specs = [
        pl.BlockSpec((chunk, hp * dk), lambda b, h, c, lg: (rows(b, c), h)),
        pl.BlockSpec((chunk, hp * dk), lambda b, h, c, lg: (rows(b, c), kq + h)),
        pl.BlockSpec((chunk, hp * dv), lambda b, h, c, lg: (rows(b, c), kv + h)),
        pl.BlockSpec((chunk, hp * dv), lambda b, h, c, lg: (rows(b, c), kg + h)),
        pl.BlockSpec((chunk, dk // 2), lambda b, h, c, lg: (rows(b, c), 0)),
        pl.BlockSpec((chunk, dk // 2), lambda b, h, c, lg: (rows(b, c), 0)),
    ]
    args = [qkvg, qkvg, qkvg, qkvg, cos, sin]
    if state0 is not None:
        in_specs.append(pl.BlockSpec((None, None, hp, dk, dv), lambda b, h, c, lg: (state_layer, b, h, 0, 0)))
        args.append(state0)
    n_prev = 0 if prev_states is None else prev_states.shape[0]
    if prev_states is not None:
        in_specs.append(pl.BlockSpec((n_prev, None, hp, dk, dv), lambda b, h, c, lg: (0, b, h, 0, 0)))
        args.append(prev_states)
    grid_spec = pltpu.PrefetchScalarGridSpec(
        num_scalar_prefetch=1,
        grid=(batch, nhb, nc),
        in_specs=in_specs,
        out_specs=[pl.BlockSpec((chunk, hp * dv), lambda b, h, c, lg: (b * nc + c, h)),
                   pl.BlockSpec((n_prev + 1, None, hp, dk, dv), lambda b, h, c, lg: (0, b, h, 0, 0))],
        scratch_shapes=[pltpu.VMEM((hp, dk, dv), F32)],
    )
    return pl.pallas_call(
        functools.partial(_retention_kernel, has_state0=state0 is not None, has_prev=prev_states is not None),
        grid_spec=grid_spec,
        out_shape=[jax.ShapeDtypeStruct((batch * seq, H * dv), BF16),
                   jax.ShapeDtypeStruct((n_prev + 1, batch, H, dk, dv), F32)],
        compiler_params=_params("arbitrary", "arbitrary", "arbitrary"),
        name="retention",
    )(log_g, *args)


def _att_prompt_kernel(q_ref, k_ref, v_ref, bias_ref, o_ref, kpad, vpad):
    S = q_ref.shape[0]
    QB = ATT_QBLOCK
    past = BAND_PAST_CHUNKS * CHUNK
    win = past + QB
    kpad[:past, :] = jnp.zeros((past, ATT_DH), BF16)
    vpad[:past, :] = jnp.zeros((past, ATT_DH), BF16)
    kpad[past:, :] = k_ref[...].astype(BF16)
    vpad[past:, :] = v_ref[...].astype(BF16)
    bias = bias_ref[...]
    qi = lax.broadcasted_iota(jnp.int32, (QB, win), 0)
    kj = lax.broadcasted_iota(jnp.int32, (QB, win), 1)
    chunk_shift = CHUNK.bit_length() - 1
    qc = jnp.right_shift(qi, chunk_shift)
    kc = jnp.right_shift(kj, chunk_shift)
    band = (kc >= qc) & (kc <= qc + BAND_PAST_CHUNKS)

    def body(i, carry):
        q0 = pl.multiple_of(i * QB, QB)
        qb = q_ref[pl.ds(q0, QB), :].astype(BF16)
        kw = kpad[pl.ds(q0, win), :]
        vw = vpad[pl.ds(q0, win), :]
        s = lax.dot_general(qb, kw, (((1,), (1,)), ((), ())), preferred_element_type=F32)
        s = s * ATT_DH ** -0.5 + bias
        valid = band & (kj >= past - q0)
        s = jnp.where(valid, s, NEG_INF)
        m = jnp.max(s, axis=-1, keepdims=True)
        e = jnp.exp(s - m)
        pr = (e / jnp.sum(e, axis=-1, keepdims=True)).astype(BF16)
        o_ref[pl.ds(q0, QB), :] = jnp.dot(pr, vw, preferred_element_type=F32).astype(o_ref.dtype)
        return carry

    lax.fori_loop(0, S // QB, body, 0, unroll=ATT_UNROLL)


def attention_prompt(qkv, bias, *, seq):
    H, dh = ATT_HEADS, ATT_DH
    past = BAND_PAST_CHUNKS * CHUNK
    return pl.pallas_call(
        _att_prompt_kernel,
        grid=(H,),
        in_specs=[pl.BlockSpec((seq, dh), lambda h: (0, h)),
                  pl.BlockSpec((seq, dh), lambda h: (0, H + h)),
                  pl.BlockSpec((seq, dh), lambda h: (0, 2 * H + h)),
                  pl.BlockSpec((None, ATT_QBLOCK, past + ATT_QBLOCK), lambda h: (h, 0, 0))],
        out_specs=pl.BlockSpec((seq, dh), lambda h: (0, h)),
        out_shape=jax.ShapeDtypeStruct((seq, H * dh), BF16),
        scratch_shapes=[pltpu.VMEM((past + seq, dh), BF16), pltpu.VMEM((past + seq, dh), BF16)],
        compiler_params=_params("arbitrary"),
        name="attention_prompt",
    )(qkv, qkv, qkv, bias)


def _att_sample_kernel(q_ref, kn_ref, vn_ref, ck_ref, cv_ref, bp_ref, bn_ref, o_ref):
    dh = ATT_DH
    H = ATT_HEADS
    W = ck_ref.shape[0] // H
    for h in range(H):
        sl = slice(h * dh, (h + 1) * dh)
        qb = q_ref[:, sl].astype(BF16)
        kn = kn_ref[:, sl].astype(BF16)
        vn = vn_ref[:, sl].astype(BF16)
        kp = ck_ref[pl.ds(h, W, stride=H), :].astype(BF16)
        vp = cv_ref[pl.ds(h, W, stride=H), :].astype(BF16)
        sp = lax.dot_general(qb, kp, (((1,), (1,)), ((), ())), preferred_element_type=F32)
        sn = lax.dot_general(qb, kn, (((1,), (1,)), ((), ())), preferred_element_type=F32)
        sp = sp * dh ** -0.5 + bp_ref[h]
        sn = sn * dh ** -0.5 + bn_ref[h]
        m = jnp.maximum(jnp.max(sp, axis=-1, keepdims=True), jnp.max(sn, axis=-1, keepdims=True))
        ep = jnp.exp(sp - m)
        en = jnp.exp(sn - m)
        den = jnp.sum(ep, axis=-1, keepdims=True) + jnp.sum(en, axis=-1, keepdims=True)
        o = jnp.dot((ep / den).astype(BF16), vp, preferred_element_type=F32)
        o = o + jnp.dot((en / den).astype(BF16), vn, preferred_element_type=F32)
        o_ref[:, sl] = o.astype(o_ref.dtype)


def attention_sample(qkv, cache_k, cache_v, bias_past, bias_new, *, batch, seq, row0, layer):
    H, dh = ATT_HEADS, ATT_DH
    W = cache_k.shape[2]
    rb0 = row0 // seq
    cache_k = cache_k.reshape(-1, dh)
    cache_v = cache_v.reshape(-1, dh)
    return pl.pallas_call(
        _att_sample_kernel,
        grid=(batch,),
        in_specs=[pl.BlockSpec((seq, H * dh), lambda b: (rb0 + b, 0)),
                  pl.BlockSpec((seq, H * dh), lambda b: (rb0 + b, 1)),
                  pl.BlockSpec((seq, H * dh), lambda b: (rb0 + b, 2)),
                  pl.BlockSpec((W * H, dh), lambda b: (layer * batch + b, 0)),
                  pl.BlockSpec((W * H, dh), lambda b: (layer * batch + b, 0)),
                  pl.BlockSpec((H, seq, W), lambda b: (0, 0, 0)),
                  pl.BlockSpec((H, seq, seq), lambda b: (0, 0, 0))],
        out_specs=pl.BlockSpec((seq, H * dh), lambda b: (b, 0)),
        out_shape=jax.ShapeDtypeStruct((batch * seq, H * dh), BF16),
        compiler_params=_params("arbitrary"),
        name="attention_sample",
    )(qkv, qkv, qkv, cache_k, cache_v, bias_past, bias_new)


def _rel_bias(table, n_q, n_k, offset):
    period = n_q + n_k
    d = jnp.arange(period)
    d = jnp.where(d < n_k, d, d - period)
    u = jnp.take(table, jnp.clip(offset - d, -REL_CLIP, REL_CLIP) + REL_CLIP, axis=0).T.astype(F32)
    H = u.shape[0]
    skew = jnp.tile(u, (1, n_q))[:, :n_q * (period - 1)].reshape(H, n_q, period - 1)
    return skew[:, :, :n_k]


def _gmlp_kernel(uv_ref, lng_ref, ws_ref, bs_ref, y_ref, vn_ref):
    DV = y_ref.shape[1]
    cg = DV // GMLP_GROUPS
    v = jax.nn.gelu(uv_ref[:, DV:])
    mu = jnp.mean(v, axis=-1, keepdims=True)
    d = v - mu
    var = jnp.mean(d * d, axis=-1, keepdims=True)
    vn = d * lax.rsqrt(var + EPS) * lng_ref[...]
    vn_ref[...] = vn
    for g in range(GMLP_GROUPS):
        sl = slice(g * cg, (g + 1) * cg)
        mixed = jnp.dot(ws_ref[g], vn[:, sl].astype(BF16), preferred_element_type=F32) + bs_ref[g]
        y_ref[:, sl] = (jax.nn.gelu(uv_ref[:, sl]) * mixed).astype(y_ref.dtype)


def gmlp_gate(uv, ln_g, ws, bs, *, n_prompt_tiles, tile):
    T = uv.shape[0]
    DV = uv.shape[1] // 2
    G = GMLP_GROUPS
    nt = T // tile
    last = n_prompt_tiles - 1
    return pl.pallas_call(
        _gmlp_kernel,
        grid=(nt,),
        in_specs=[pl.BlockSpec((tile, 2 * DV), lambda i: (i, 0)),
                  pl.BlockSpec((1, DV), lambda i: (0, 0)),
                  pl.BlockSpec((None, G, tile, tile), lambda i: (i // n_prompt_tiles, 0, 0, 0)),
                  pl.BlockSpec((None, G, tile, 1), lambda i: (i // n_prompt_tiles, 0, 0, 0))],
        out_specs=[pl.BlockSpec((tile, DV), lambda i: (i, 0)),
                   pl.BlockSpec((tile, DV), lambda i: (jnp.maximum(i - last, 0), 0))],
        out_shape=[jax.ShapeDtypeStruct((T, DV), BF16),
                   jax.ShapeDtypeStruct(((nt - last) * tile, DV), F32)],
        compiler_params=_params("arbitrary"),
        name="gmlp_gate",
    )(uv, ln_g.reshape(1, DV), ws, bs)


def _router_kernel(h_ref, w_ref, idx_ref, gate_ref):
    logits = jnp.dot(h_ref[...], w_ref[...].astype(BF16), preferred_element_type=F32)
    lane = lax.broadcasted_iota(jnp.int32, logits.shape, 1)
    lane_f = lane.astype(F32)
    l1 = jnp.where(lane < N_EXPERTS, logits, -jnp.inf)
    v1 = jnp.max(l1, axis=-1, keepdims=True)
    i1 = jnp.min(jnp.where(l1 == v1, lane_f, float(LANES)), axis=-1, keepdims=True)
    l2 = jnp.where(lane_f == i1, -jnp.inf, l1)
    v2 = jnp.max(l2, axis=-1, keepdims=True)
    i2 = jnp.min(jnp.where(l2 == v2, lane_f, float(LANES)), axis=-1, keepdims=True)
    e2 = jnp.exp(v2 - v1)
    den = 1.0 + e2
    idx_ref[...] = jnp.where(lane == 0, i1, jnp.where(lane == 1, i2, 0.0)).astype(jnp.int32)
    gate_ref[...] = jnp.where(lane == 0, 1.0 / den, jnp.where(lane == 1, e2 / den, 0.0))


def router(h, w_router_padded, *, tm):
    T, D = h.shape
    return pl.pallas_call(
        _router_kernel,
        grid=(T // tm,),
        in_specs=[pl.BlockSpec((tm, D), lambda i: (i, 0)), pl.BlockSpec((D, LANES), lambda i: (0, 0))],
        out_specs=[pl.BlockSpec((tm, LANES), lambda i: (i, 0)), pl.BlockSpec((tm, LANES), lambda i: (i, 0))],
        out_shape=[jax.ShapeDtypeStruct((T, LANES), jnp.int32), jax.ShapeDtypeStruct((T, LANES), F32)],
        compiler_params=_params("arbitrary"),
        name="router",
    )(h, w_router_padded)


def _gather_kernel(lo_ref, hi_ref, p0_ref, p1_ref, h_ref, o_ref, acc):
    m = pl.program_id(0)
    tm = o_ref.shape[0]
    sb = GATHER_SRC_BLOCK
    row = m * tm + lax.broadcasted_iota(jnp.int32, (tm, sb), 0)
    acc[...] = jnp.zeros_like(acc)

    def body(b, carry):
        select = jnp.where(p0_ref[b] == row, 1.0, jnp.where(p1_ref[b] == row, 1.0, 0.0)).astype(h_ref.dtype)
        acc[...] += jnp.dot(select, h_ref[pl.ds(pl.multiple_of(b * sb, sb), sb), :], preferred_element_type=F32)
        return carry

    lax.fori_loop(lo_ref[m], hi_ref[m] + 1, body, 0)
    o_ref[...] = acc[...].astype(o_ref.dtype)


def gather_rows(h, pos0, pos1, blk_lo, blk_hi, n_rows, *, tm):
    T, D = h.shape
    nsb = T // GATHER_SRC_BLOCK
    grid_spec = pltpu.PrefetchScalarGridSpec(
        num_scalar_prefetch=2,
        grid=(n_rows // tm,),
        in_specs=[pl.BlockSpec((nsb, 1, GATHER_SRC_BLOCK), lambda i, lo, hi: (0, 0, 0)),
                  pl.BlockSpec((nsb, 1, GATHER_SRC_BLOCK), lambda i, lo, hi: (0, 0, 0)),
                  pl.BlockSpec((T, D), lambda i, lo, hi: (0, 0), pipeline_mode=pl.Buffered(1))],
        out_specs=pl.BlockSpec((tm, D), lambda i, lo, hi: (i, 0)),
        scratch_shapes=[pltpu.VMEM((tm, D), F32)],
    )
    return pl.pallas_call(
        _gather_kernel,
        grid_spec=grid_spec,
        out_shape=jax.ShapeDtypeStruct((n_rows, D), h.dtype),
        compiler_params=_params("arbitrary"),
        name="moe_gather",
    )(blk_lo, blk_hi, pos0.reshape(nsb, 1, GATHER_SRC_BLOCK), pos1.reshape(nsb, 1, GATHER_SRC_BLOCK), h)


def _on_valid_rows(valid, o_ref, compute):
    tm = o_ref.shape[0]
    half = tm // 2

    @pl.when(valid > half)
    def _():
        compute(slice(0, tm))

    @pl.when((valid > 0) & (valid <= half))
    def _():
        compute(slice(0, half))
        o_ref[half:, :] = jnp.zeros((tm - half, o_ref.shape[1]), o_ref.dtype)

    @pl.when(valid == 0)
    def _():
        o_ref[...] = jnp.zeros_like(o_ref)


def _group_weights(te_ref, first_ref, nxt_ref, wrap_ref, copies, cast):
    n = pl.program_id(0)
    m = pl.program_id(1)

    @pl.when(first_ref[m] == 1)
    def _():
        @pl.when((n == 0) & (m == 0))
        def _():
            for cp in copies(te_ref[0], 0):
                cp.start()

        for cp in copies(te_ref[m], n):
            cp.wait()
        cast()
        n_next = n + wrap_ref[m]

        @pl.when(n_next < pl.num_programs(0))
        def _():
            for cp in copies(nxt_ref[m], n_next):
                cp.start()


def _moe_up_kernel(te_ref, first_ref, valid_ref, nxt_ref, wrap_ref, x_ref, w_hbm, o_ref,
                   stage_a, stage_b, wa_bf, wb_bf, sems):
    m = pl.program_id(1)
    tn = wa_bf.shape[1]
    F = w_hbm.shape[2] // 2

    def copies(e, n):
        col = pl.multiple_of(n * tn, tn)
        return (pltpu.make_async_copy(w_hbm.at[e, :, pl.ds(col, tn)], stage_a, sems.at[0]),
                pltpu.make_async_copy(w_hbm.at[e, :, pl.ds(pl.multiple_of(F + col, tn), tn)], stage_b, sems.at[1]))

    def cast():
        wa_bf[...] = stage_a[...].astype(BF16)
        wb_bf[...] = stage_b[...].astype(BF16)

    _group_weights(te_ref, first_ref, nxt_ref, wrap_ref, copies, cast)

    def compute(rows):
        x = x_ref[rows, :]
        a = jnp.dot(x, wa_bf[...], preferred_element_type=F32)
        b = jnp.dot(x, wb_bf[...], preferred_element_type=F32)
        o_ref[rows, :] = (_silu(a) * b).astype(o_ref.dtype)

    _on_valid_rows(valid_ref[m], o_ref, compute)


def moe_up(xs, w_gu, plan, *, tm, tn):
    P, D = xs.shape
    F = w_gu.shape[2] // 2
    grid_spec = pltpu.PrefetchScalarGridSpec(
        num_scalar_prefetch=len(plan),
        grid=(F // tn, P // tm),
        in_specs=[pl.BlockSpec((tm, D), lambda n, m, *_: (m, 0)),
                  pl.BlockSpec(memory_space=pl.ANY)],
        out_specs=pl.BlockSpec((tm, tn), lambda n, m, *_: (m, n)),
        scratch_shapes=[pltpu.VMEM((D, tn), w_gu.dtype), pltpu.VMEM((D, tn), w_gu.dtype),
                        pltpu.VMEM((D, tn), BF16), pltpu.VMEM((D, tn), BF16), pltpu.SemaphoreType.DMA((2,))],
    )
    return pl.pallas_call(
        _moe_up_kernel,
        grid_spec=grid_spec,
        out_shape=jax.ShapeDtypeStruct((P, F), BF16),
        compiler_params=_params("arbitrary", "arbitrary"),
        name="moe_up",
    )(*plan, xs, w_gu)


def _moe_down_kernel(te_ref, first_ref, valid_ref, nxt_ref, wrap_ref, x_ref, w_hbm, o_ref, stage, w_bf, sem):
    m = pl.program_id(1)
    tn = w_bf.shape[1]

    def copies(e, n):
        return (pltpu.make_async_copy(w_hbm.at[e, :, pl.ds(pl.multiple_of(n * tn, tn), tn)], stage, sem),)

    def cast():
        w_bf[...] = stage[...].astype(BF16)

    _group_weights(te_ref, first_ref, nxt_ref, wrap_ref, copies, cast)

    def compute(rows):
        o_ref[rows, :] = jnp.dot(x_ref[rows, :], w_bf[...], preferred_element_type=F32)

    _on_valid_rows(valid_ref[m], o_ref, compute)


def moe_down(hm, w_down, plan, *, tm, tn):
    P, F = hm.shape
    D = w_down.shape[2]
    grid_spec = pltpu.PrefetchScalarGridSpec(
        num_scalar_prefetch=len(plan),
        grid=(D // tn, P // tm),
        in_specs=[pl.BlockSpec((tm, F), lambda n, m, *_: (m, 0)),
                  pl.BlockSpec(memory_space=pl.ANY)],
        out_specs=pl.BlockSpec((tm, tn), lambda n, m, *_: (m, n)),
        scratch_shapes=[pltpu.VMEM((F, tn), w_down.dtype), pltpu.VMEM((F, tn), BF16), pltpu.SemaphoreType.DMA(())],
    )
    return pl.pallas_call(
        _moe_down_kernel,
        grid_spec=grid_spec,
        out_shape=jax.ShapeDtypeStruct((P, D), F32),
        compiler_params=_params("arbitrary", "arbitrary"),
        name="moe_down",
    )(*plan, hm, w_down)


def _row_copy(y_hbm, buf, src_row, dst_row, sem):
    return pltpu.make_async_copy(y_hbm.at[pl.ds(src_row, 1), :], buf.at[pl.ds(dst_row, 1), :], sem)


def _combine_norm_kernel(p0_ref, p1_ref, x_ref, gate_ref, g_ref, y_hbm, xo_ref, hn_ref, buf0, buf1, sems):
    tm = x_ref.shape[0]
    base = pl.program_id(0) * tm

    def start(r, carry):
        _row_copy(y_hbm, buf0, p0_ref[base + r], r, sems.at[0]).start(priority=0)
        _row_copy(y_hbm, buf1, p1_ref[base + r], r, sems.at[1]).start(priority=DMA_PRIORITIES - 1)
        return carry

    def wait(r, carry):
        _row_copy(y_hbm, buf0, 0, r, sems.at[0]).wait()
        _row_copy(y_hbm, buf1, 0, r, sems.at[1]).wait()
        return carry

    lax.fori_loop(0, tm, start, 0)
    lax.fori_loop(0, tm, wait, 0)
    gate = gate_ref[...]
    xn = x_ref[...] + (gate[:, 0:1] * buf0[...] + gate[:, 1:2] * buf1[...])
    xo_ref[...] = xn
    hn = xn * lax.rsqrt(jnp.mean(xn * xn, axis=-1, keepdims=True) + EPS) * g_ref[...]
    hn_ref[...] = hn.astype(hn_ref.dtype)


def moe_combine_norm(x, gates, ys, pos0, pos1, g, *, tm):
    T, D = x.shape
    grid_spec = pltpu.PrefetchScalarGridSpec(
        num_scalar_prefetch=2,
        grid=(T // tm,),
        in_specs=[pl.BlockSpec((tm, D), lambda i, a, b: (i, 0)),
                  pl.BlockSpec((tm, LANES), lambda i, a, b: (i, 0)),
                  pl.BlockSpec((1, D), lambda i, a, b: (0, 0)),
                  pl.BlockSpec(memory_space=pl.ANY)],
        out_specs=[pl.BlockSpec((tm, D), lambda i, a, b: (i, 0)), pl.BlockSpec((tm, D), lambda i, a, b: (i, 0))],
        scratch_shapes=[pltpu.VMEM((tm, D), F32), pltpu.VMEM((tm, D), F32), pltpu.SemaphoreType.DMA((2,))],
    )
    return pl.pallas_call(
        _combine_norm_kernel,
        grid_spec=grid_spec,
        out_shape=[jax.ShapeDtypeStruct((T, D), F32), jax.ShapeDtypeStruct((T, D), BF16)],
        compiler_params=_params("arbitrary"),
        name="moe_combine_norm",
    )(pos0, pos1, x, gates, g.reshape(1, D), ys)


def _route_plan(idx, tm):
    T = idx.shape[0]
    E = N_EXPERTS
    n_tiles = (TOP_K * T) // tm + E
    experts = jnp.arange(E, dtype=jnp.int32)
    flat_e = idx.reshape(-1)
    onehot = (flat_e[:, None] == experts[None, :]).astype(jnp.int32)
    csum = jnp.cumsum(onehot, axis=0)
    rank = jnp.sum((csum - onehot) * onehot, axis=1)
    counts = csum[-1]
    tiles_e = (counts + tm - 1) // tm
    tile_end = jnp.cumsum(tiles_e)
    tile_start = tile_end - tiles_e
    pos = (jnp.sum(onehot * tile_start[None, :], axis=1) * tm + rank).astype(jnp.int32)
    n_used = tile_end[-1]
    m_ids = jnp.arange(n_tiles, dtype=jnp.int32)
    te = jnp.sum((jnp.minimum(m_ids, n_used - 1)[:, None] >= tile_end[None, :]).astype(jnp.int32), axis=1)
    first = ((m_ids == 0) | (te != jnp.roll(te, 1))) & (m_ids < n_used)
    te_onehot = (te[:, None] == experts[None, :]).astype(jnp.int32)
    rows_before = (m_ids - jnp.sum(te_onehot * tile_start[None, :], axis=1)) * tm
    valid = jnp.clip(jnp.sum(te_onehot * counts[None, :], axis=1) - rows_before, 0, tm)
    valid = jnp.where(m_ids < n_used, valid, 0).astype(jnp.int32)
    nonempty = jnp.where(tiles_e > 0, experts, E)
    after = jnp.min(jnp.where(experts[None, :] > experts[:, None], nonempty[None, :], E), axis=1)
    nxt_e = jnp.where(after < E, after, jnp.min(nonempty))
    nxt = jnp.sum(te_onehot * nxt_e[None, :], axis=1).astype(jnp.int32)
    wrap = jnp.sum(te_onehot * (after == E).astype(jnp.int32)[None, :], axis=1).astype(jnp.int32)
    csum_tile = jnp.sum(csum[:, :, None] * te_onehot.T[None, :, :], axis=1)
    pair_first = jnp.sum((csum_tile <= rows_before[None, :]).astype(jnp.int32), axis=0)
    pair_last = jnp.sum((csum_tile <= (rows_before + valid - 1)[None, :]).astype(jnp.int32), axis=0)
    blk_lo = jnp.where(valid > 0, pair_first // (TOP_K * GATHER_SRC_BLOCK), 0).astype(jnp.int32)
    blk_hi = jnp.where(valid > 0, pair_last // (TOP_K * GATHER_SRC_BLOCK), -1).astype(jnp.int32)
    return (pos[0::TOP_K], pos[1::TOP_K], blk_lo, blk_hi,
            (te.astype(jnp.int32), first.astype(jnp.int32), valid, nxt, wrap))


def moe(x, h, w_router, w_gu, w_down, g_next, *, layer, tm_tok):
    T = x.shape[0]
    E = N_EXPERTS
    w_r = jnp.pad(w_router[layer], ((0, 0), (0, LANES - E)))
    idx, gates = router(h, w_r, tm=tm_tok)
    pos0, pos1, blk_lo, blk_hi, (te, first, valid, nxt, wrap) = _route_plan(idx[:, :TOP_K], MOE_TM)
    plan = (te + layer * E, first, valid, nxt + layer * E, wrap)
    n_rows = te.shape[0] * MOE_TM
    xs = gather_rows(h, pos0, pos1, blk_lo, blk_hi, n_rows, tm=MOE_TM)
    hm = moe_up(xs, w_gu.reshape((-1,) + w_gu.shape[2:]), plan, tm=MOE_TM, tn=1024)
    ys = moe_down(hm, w_down.reshape((-1,) + w_down.shape[2:]), plan, tm=MOE_TM, tn=512)
    return moe_combine_norm(x, gates, ys, pos0, pos1, g_next, tm=tm_tok)


def kernel(x_prompt, x_sample, state_ret, cache_k, cache_v, p_prompt, p_sample, g_mix, g_ffn, g_ple, g_final,
           ret_w_in, ret_w_out, att_w_in, att_w_out, att_rel_bias, gmlp_w_in, gmlp_ln_g, gmlp_w_s, gmlp_b_s,
           gmlp_w_out, ffn_w_gu, ffn_w_down, moe_w_router, moe_w_gu, moe_w_down, ple_w_in, ple_w_gate):
    B, S, D = x_prompt.shape
    BS, L, _ = x_sample.shape
    depth = g_mix.shape[0]
    NP, NS = B * S, BS * L
    T = NP + NS
    assert B == 1 and S % RET_CHUNK == 0 and S % ATT_QBLOCK == 0 and L <= CHUNK and NP % NS == 0
    TM = T // 8
    TMS = T // 16

    x = jnp.concatenate([x_prompt.reshape(NP, D), x_sample.reshape(NS, D)], axis=0)
    p_all = jnp.concatenate([p_prompt.reshape(depth, NP, -1), p_sample.reshape(depth, NS, -1)], axis=1)

    pos = jnp.concatenate([jnp.arange(S, dtype=jnp.int32), jnp.tile(PAST_LEN + jnp.arange(L, dtype=jnp.int32), BS)])
    freqs = ROPE_BASE ** (-jnp.arange(0, RET_DK, 2, dtype=F32) / RET_DK)
    ang = pos.astype(F32)[:, None] * freqs[None, :]
    cos, sin = jnp.cos(ang), jnp.sin(ang)
    log_g = jnp.log1p(-jnp.exp2(-5.0 - jnp.arange(RET_HEADS, dtype=F32)))

    ret_p, ret_s = None, None
    k_p, v_p, k_s, v_s, gv = [], [], [], [], []
    (h,) = rmsnorm(x, g_mix[0], [BF16], TMS)
    for i in range(depth):
        kind, slot = i % 3, i // 3
        if kind == 0:
            qkvg = matmul(h, ret_w_in, layer=slot, tm=TM, tn=1024)
            y_p, ret_p = retention(qkvg, cos, sin, log_g, None, ret_p, batch=B, seq=S, chunk=RET_CHUNK, row0=0,
                                   heads_per_step=RET_HEADS // 2)
            y_s, ret_s = retention(qkvg, cos, sin, log_g, state_ret, ret_s, batch=BS, seq=L, chunk=L, row0=NP,
                                   heads_per_step=RET_HEADS, state_layer=slot)
            x = matmul(y_p, ret_w_out, layer=slot, tm=NS, tn=512, res=x, a_tail=y_s)
        elif kind == 1:
            HD = ATT_HEADS * ATT_DH
            W = cache_k.shape[2]
            past = BAND_PAST_CHUNKS * CHUNK
            qkv = matmul(h, att_w_in, layer=slot, tm=TM, tn=1024)
            bias_p = _rel_bias(att_rel_bias[slot], ATT_QBLOCK, past + ATT_QBLOCK, past)
            bias_s = _rel_bias(att_rel_bias[slot], L, W + L, W)
            o_p = attention_prompt(qkv, bias_p, seq=S)
            o_s = attention_sample(qkv, cache_k, cache_v, bias_s[:, :, :W], bias_s[:, :, W:],
                                   batch=BS, seq=L, row0=NP, layer=slot)
            w = min(past, S)
            k_p.append(qkv[NP - w:NP, HD:2 * HD].reshape(B, w, ATT_HEADS, ATT_DH))
            v_p.append(qkv[NP - w:NP, 2 * HD:].reshape(B, w, ATT_HEADS, ATT_DH))
            k_s.append(qkv[NP:, HD:2 * HD].reshape(BS, L, ATT_HEADS, ATT_DH))
            v_s.append(qkv[NP:, 2 * HD:].reshape(BS, L, ATT_HEADS, ATT_DH))
            x = matmul(o_p, att_w_out, layer=slot, tm=NS, tn=1024, res=x, a_tail=o_s)
        else:
            tile = GMLP_CHUNK
            uv = matmul(h, gmlp_w_in, layer=slot, tm=TM, tn=1024)
            r = jnp.arange(tile)
            ws_p = jnp.where((r[:, None] >= r[None, :])[None], gmlp_w_s[slot][:, :tile, :tile], 0.0)
            same = (r[:, None] // L == r[None, :] // L) & (r[:, None] >= r[None, :])
            ws_s = jnp.where(same[None], jnp.tile(gmlp_w_s[slot][:, :L, :L], (1, tile // L, tile // L)), 0.0)
            bs_p = gmlp_b_s[slot][:, :tile]
            bs_s = jnp.tile(gmlp_b_s[slot][:, :L], (1, tile // L))
            ws = jnp.stack([ws_p, ws_s]).astype(BF16)
            bs = jnp.stack([bs_p, bs_s])[..., None]
            yg, vn = gmlp_gate(uv, gmlp_ln_g[slot], ws, bs, n_prompt_tiles=NP // tile, tile=tile)
            gv.append((vn[:tile].reshape(B, tile, -1), vn[tile:].reshape(BS, L, -1)))
            x = matmul(yg, gmlp_w_out, layer=slot, tm=TMS, tn=512, res=x)

        if i % 2 == 0:
            (h,) = rmsnorm(x, g_ffn[i], [BF16], TMS)
            x = dense_ffn(x, h, ffn_w_gu, ffn_w_down, layer=i // 2, tm_up=TM, tm_down=TMS)
            (hn,) = rmsnorm(x, g_ple[i], [BF16], TMS)
        else:
            (h,) = rmsnorm(x, g_ffn[i], [BF16], TMS)
            x, hn = moe(x, h, moe_w_router, moe_w_gu, moe_w_down, g_ple[i], layer=i // 2, tm_tok=TMS)

        g_next, next_dtype = (g_mix[i + 1], BF16) if i + 1 < depth else (g_final, F32)
        x, h = ple_norm(x, hn, p_all, ple_w_gate, ple_w_in, g_next, next_dtype, layer=i, tm=T // 32)

    y = h
    y_prompt = y[:NP].reshape(B, S, D)
    y_sample = y[NP:].reshape(BS, L, D)
    return (y_prompt, y_sample, ret_p, ret_s, jnp.stack(k_p), jnp.stack(v_p),
            jnp.stack(k_s), jnp.stack(v_s), jnp.stack([a for a, _ in gv]), jnp.stack([b for _, b in gv]))
```

```python
import functools

import jax
import jax.numpy as jnp
from jax import lax
from jax.experimental import pallas as pl
from jax.experimental.pallas import tpu as pltpu

F32 = jnp.float32
BF16 = jnp.bfloat16

EPS = 1e-6
NEG_INF = -1e30
CHUNK = 64
ROPE_BASE = 10000.0
RET_HEADS, RET_DK, RET_DV = 8, 256, 512
ATT_HEADS, ATT_DH = 16, 128
BAND_PAST_CHUNKS = 8
REL_CLIP = 128
GMLP_CHUNK, GMLP_GROUPS = 128, 8
N_EXPERTS, TOP_K = 8, 2
PAST_LEN = 2048

LANES = 128
DMA_PRIORITIES = 2
V7X_VMEM_LIMIT_BYTES = 60 * 1000 * 1024

RET_CHUNK = 256
ATT_QBLOCK = 4 * CHUNK
ATT_UNROLL = 4
MOE_TM = 512
GATHER_SRC_BLOCK = 512


def _params(*sem):
    return pltpu.CompilerParams(dimension_semantics=sem, vmem_limit_bytes=V7X_VMEM_LIMIT_BYTES)


def _silu(x):
    return x * jax.nn.sigmoid(x)


def _rmsnorm_kernel(x_ref, g_ref, *o_refs):
    x = x_ref[...]
    y = x * lax.rsqrt(jnp.mean(x * x, axis=-1, keepdims=True) + EPS) * g_ref[...]
    for o_ref in o_refs:
        o_ref[...] = y.astype(o_ref.dtype)


def rmsnorm(x, g, dtypes, tm):
    T, D = x.shape
    out_specs = [pl.BlockSpec((tm, D), lambda i: (i, 0)) for _ in dtypes]
    out_shape = [jax.ShapeDtypeStruct((T, D), dt) for dt in dtypes]
    return pl.pallas_call(
        _rmsnorm_kernel,
        grid=(T // tm,),
        in_specs=[pl.BlockSpec((tm, D), lambda i: (i, 0)), pl.BlockSpec((1, D), lambda i: (0, 0))],
        out_specs=out_specs,
        out_shape=out_shape,
        compiler_params=_params("arbitrary"),
        name="rmsnorm",
    )(x, g.reshape(1, D))


def _mm_kernel(a_ref, *rest, n_head, has_res, cast_w):
    rest = list(rest)
    t_ref = rest.pop(0) if n_head is not None else None
    w_ref = rest.pop(0)
    r_ref = rest.pop(0) if has_res else None
    o_ref, *scratch = rest
    if cast_w:
        w_bf = scratch[0]

        @pl.when(pl.program_id(1) == 0)
        def _():
            w_bf[...] = w_ref[...].astype(BF16)

        w = w_bf[...]
    else:
        w = w_ref[...]
    a = a_ref[...]
    if n_head is not None:
        a = jnp.where(pl.program_id(1) < n_head, a, t_ref[...])
    acc = jnp.dot(a, w, preferred_element_type=F32)
    if has_res:
        acc = r_ref[...] + acc
    o_ref[...] = acc.astype(o_ref.dtype)


def _weight_spec(w, layer, tn):
    K = w.shape[-2]
    if w.ndim == 2:
        return pl.BlockSpec((K, tn), lambda n, m: (0, n))
    return pl.BlockSpec((None, K, tn), lambda n, m: (layer, 0, n))


def matmul(a, w, *, tm, tn, layer=0, out_dtype=F32, res=None, a_tail=None):
    K = a.shape[1]
    N = w.shape[-1]
    cast_w = w.dtype != BF16
    n_head = None
    M = a.shape[0]
    in_specs = [pl.BlockSpec((tm, K), lambda n, m: (m, 0))]
    args = [a]
    if a_tail is not None:
        assert a.shape[0] % tm == 0 and a_tail.shape[0] % tm == 0
        n_head = a.shape[0] // tm
        M += a_tail.shape[0]
        in_specs = [pl.BlockSpec((tm, K), lambda n, m: (jnp.minimum(m, n_head - 1), 0)),
                    pl.BlockSpec((tm, K), lambda n, m: (jnp.maximum(m - n_head, 0), 0))]
        args.append(a_tail)
    in_specs.append(_weight_spec(w, layer, tn))
    args.append(w)
    if res is not None:
        in_specs.append(pl.BlockSpec((tm, tn), lambda n, m: (m, n)))
        args.append(res)
    return pl.pallas_call(
        functools.partial(_mm_kernel, n_head=n_head, has_res=res is not None, cast_w=cast_w),
        grid=(N // tn, M // tm),
        in_specs=in_specs,
        out_specs=pl.BlockSpec((tm, tn), lambda n, m: (m, n)),
        out_shape=jax.ShapeDtypeStruct((M, N), out_dtype),
        scratch_shapes=[pltpu.VMEM((K, tn), BF16)] if cast_w else [],
        compiler_params=_params("arbitrary", "arbitrary"),
        name="matmul",
    )(*args)


def _swiglu_up_kernel(x_ref, w_hbm, o_ref, stage_a, stage_b, wa_bf, wb_bf, sems, *, layer, tail):
    n = pl.program_id(0)
    nb = pl.num_programs(0)
    tn = wa_bf.shape[1]
    F = w_hbm.shape[2] // 2

    def copies(nn, width):
        col = nn * tn
        a = pltpu.make_async_copy(w_hbm.at[layer, :, pl.ds(pl.multiple_of(col, LANES), width)],
                                  stage_a.at[:, pl.ds(0, width)], sems.at[0])
        b = pltpu.make_async_copy(w_hbm.at[layer, :, pl.ds(pl.multiple_of(F + col, LANES), width)],
                                  stage_b.at[:, pl.ds(0, width)], sems.at[1])
        return a, b

    def for_tile(nn, action):
        if tail == tn:
            for cp in copies(nn, tn):
                action(cp)
        else:
            @pl.when(nn < nb - 1)
            def _():
                for cp in copies(nn, tn):
                    action(cp)

            @pl.when(nn == nb - 1)
            def _():
                for cp in copies(nn, tail):
                    action(cp)

    @pl.when(pl.program_id(1) == 0)
    def _():
        @pl.when(n == 0)
        def _():
            for_tile(n, lambda cp: cp.start())

        for_tile(n, lambda cp: cp.wait())
        wa_bf[...] = stage_a[...].astype(BF16)
        wb_bf[...] = stage_b[...].astype(BF16)

        @pl.when(n + 1 < nb)
        def _():
            for_tile(n + 1, lambda cp: cp.start())

    x = x_ref[...]
    a = jnp.dot(x, wa_bf[...], preferred_element_type=F32)
    b = jnp.dot(x, wb_bf[...], preferred_element_type=F32)
    o_ref[...] = (_silu(a) * b).astype(o_ref.dtype)


def swiglu_up(x, w_gu, *, layer, tm, tn):
    T, D = x.shape
    F = w_gu.shape[2] // 2
    nb = pl.cdiv(F, tn)
    tail = F - (nb - 1) * tn
    assert F % LANES == 0 and (nb > 1 or tail == tn)
    return pl.pallas_call(
        functools.partial(_swiglu_up_kernel, layer=layer, tail=tail),
        grid=(nb, T // tm),
        in_specs=[pl.BlockSpec((tm, D), lambda n, m: (m, 0)), pl.BlockSpec(memory_space=pl.ANY)],
        out_specs=pl.BlockSpec((tm, tn), lambda n, m: (m, n)),
        out_shape=jax.ShapeDtypeStruct((T, F), BF16),
        scratch_shapes=[pltpu.VMEM((D, tn), w_gu.dtype), pltpu.VMEM((D, tn), w_gu.dtype),
                        pltpu.VMEM((D, tn), BF16), pltpu.VMEM((D, tn), BF16), pltpu.SemaphoreType.DMA((2,))],
        compiler_params=_params("arbitrary", "arbitrary"),
        name="swiglu_up",
    )(x, w_gu)


def dense_ffn(x, h, w_gu, w_down, *, layer, tm_up, tm_down):
    hm = swiglu_up(h, w_gu, layer=layer, tm=tm_up, tn=512)
    return matmul(hm, w_down, layer=layer, tm=tm_down, tn=512, res=x)


def _ple_norm_kernel(hn_ref, wg_ref, p_ref, wi_ref, x_ref, gn_ref, o_ref, hnext_ref, wg_bf, wi_bf):
    @pl.when(pl.program_id(0) == 0)
    def _():
        wg_bf[...] = wg_ref[...].astype(BF16)
        wi_bf[...] = wi_ref[...].astype(BF16)

    gate = jax.nn.sigmoid(jnp.dot(hn_ref[...], wg_bf[...], preferred_element_type=F32))
    emb = jnp.dot(p_ref[...].astype(BF16), wi_bf[...], preferred_element_type=F32)
    xn = x_ref[...] + gate * emb
    o_ref[...] = xn
    hnext = xn * lax.rsqrt(jnp.mean(xn * xn, axis=-1, keepdims=True) + EPS) * gn_ref[...]
    hnext_ref[...] = hnext.astype(hnext_ref.dtype)


def ple_norm(x, hn, p, w_gate, w_in, g_next, next_dtype, *, layer, tm):
    T, D = x.shape
    DP = p.shape[2]
    resident = pl.Buffered(1)
    return pl.pallas_call(
        _ple_norm_kernel,
        grid=(T // tm,),
        in_specs=[pl.BlockSpec((tm, D), lambda m: (m, 0)),
                  pl.BlockSpec((None, D, D), lambda m: (layer, 0, 0), pipeline_mode=resident),
                  pl.BlockSpec((None, tm, DP), lambda m: (layer, m, 0)),
                  pl.BlockSpec((None, DP, D), lambda m: (layer, 0, 0), pipeline_mode=resident),
                  pl.BlockSpec((tm, D), lambda m: (m, 0)),
                  pl.BlockSpec((1, D), lambda m: (0, 0))],
        out_specs=[pl.BlockSpec((tm, D), lambda m: (m, 0)), pl.BlockSpec((tm, D), lambda m: (m, 0))],
        out_shape=[jax.ShapeDtypeStruct((T, D), F32), jax.ShapeDtypeStruct((T, D), next_dtype)],
        scratch_shapes=[pltpu.VMEM((D, D), BF16), pltpu.VMEM((DP, D), BF16)],
        compiler_params=_params("arbitrary"),
        name="ple_norm",
    )(hn, w_gate, p, w_in, x, g_next.reshape(1, D))


def _retention_kernel(lg_ref, q_ref, k_ref, v_ref, g_ref, cos_ref, sin_ref, *rest, has_state0, has_prev):
    rest = list(rest)
    s0_ref = rest.pop(0) if has_state0 else None
    prev_ref = rest.pop(0) if has_prev else None
    y_ref, sout_ref, state = rest
    n_prev = sout_ref.shape[0] - 1
    hb = pl.program_id(1)
    c = pl.program_id(2)
    C = q_ref.shape[0]
    hp = state.shape[0]
    dk, dv = RET_DK, RET_DV
    half = dk // 2

    @pl.when(c == 0)
    def _():
        if has_state0:
            state[...] = s0_ref[...]
        else:
            state[...] = jnp.zeros_like(state)

    cos = cos_ref[...]
    sin = sin_ref[...]

    def rotary(x):
        x1, x2 = x[:, :half], x[:, half:]
        return jnp.concatenate([x1 * cos - x2 * sin, x1 * sin + x2 * cos], axis=-1)

    row = lax.broadcasted_iota(jnp.int32, (C, C), 0)
    col = lax.broadcasted_iota(jnp.int32, (C, C), 1)
    rel = (row - col).astype(F32)
    rel_pos = jnp.maximum(rel, 0.0)
    idx = lax.broadcasted_iota(jnp.int32, (C, 1), 0).astype(F32)

    for j in range(hp):
        lg = lg_ref[hb * hp + j]
        q = rotary(q_ref[:, j * dk:(j + 1) * dk])
        k = rotary(k_ref[:, j * dk:(j + 1) * dk]) * dk ** -0.5
        vb = v_ref[:, j * dv:(j + 1) * dv].astype(BF16)
        decay = jnp.where(rel >= 0, jnp.exp(rel_pos * lg), 0.0)
        q_decay = jnp.exp((idx + 1.0) * lg)
        k_decay = jnp.exp((C - 1.0 - idx) * lg)
        chunk_decay = jnp.exp(jnp.zeros((1, 1), F32) + C * lg)

        qb = q.astype(BF16)
        scores = lax.dot_general(qb, k.astype(BF16), (((1,), (1,)), ((), ())), preferred_element_type=F32) * decay
        o = jnp.dot(scores.astype(BF16), vb, preferred_element_type=F32)
        st = state[j]
        o = o + jnp.dot(qb, st.astype(BF16), preferred_element_type=F32) * q_decay
        kv = lax.dot_general((k * k_decay).astype(BF16), vb, (((0,), (0,)), ((), ())), preferred_element_type=F32)
        st = st * chunk_decay + kv
        state[j] = st
        sout_ref[n_prev, j] = st

        o = o * lax.rsqrt(jnp.mean(o * o, axis=-1, keepdims=True) + EPS)
        y_ref[:, j * dv:(j + 1) * dv] = (o * _silu(g_ref[:, j * dv:(j + 1) * dv])).astype(y_ref.dtype)

    if has_prev:
        @pl.when(c == pl.num_programs(2) - 1)
        def _():
            sout_ref[:n_prev] = prev_ref[...]


def retention(qkvg, cos, sin, log_g, state0, prev_states, *, batch, seq, chunk, row0, heads_per_step, state_layer=0):
    H, dk, dv = RET_HEADS, RET_DK, RET_DV
    hp = heads_per_step
    nhb = H // hp
    nc = seq // chunk
    rb0 = row0 // chunk
    kq = nhb
    kv = (2 * H * dk) // (hp * dv)
    kg = kv + nhb

    def rows(b, c):
        return rb0 + b * nc + c

    in_specs = [
        pl.BlockSpec((chunk, hp * dk), lambda b, h, c, lg: (rows(b, c), h)),
        pl.BlockSpec((chunk, hp * dk), lambda b, h, c, lg: (rows(b, c), kq + h)),
        pl.BlockSpec((chunk, hp * dv), lambda b, h, c, lg: (rows(b, c), kv + h)),
        pl.BlockSpec((chunk, hp * dv), lambda b, h, c, lg: (rows(b, c), kg + h)),
        pl.BlockSpec((chunk, dk // 2), lambda b, h, c, lg: (rows(b, c), 0)),
        pl.BlockSpec((chunk, dk // 2), lambda b, h, c, lg: (rows(b, c), 0)),
    ]
    args = [qkvg, qkvg, qkvg, qkvg, cos, sin]
    if state0 is not None:
        in_specs.append(pl.BlockSpec((None, None, hp, dk, dv), lambda b, h, c, lg: (state_layer, b, h, 0, 0)))
        args.append(state0)
    n_prev = 0 if prev_states is None else prev_states.shape[0]
    if prev_states is not None:
        in_specs.append(pl.BlockSpec((n_prev, None, hp, dk, dv), lambda b, h, c, lg: (0, b, h, 0, 0)))
        args.append(prev_states)
    grid_spec = pltpu.PrefetchScalarGridSpec(
        num_scalar_prefetch=1,
        grid=(batch, nhb, nc),
        in_specs=in_specs,
        out_specs=[pl.BlockSpec((chunk, hp * dv), lambda b, h, c, lg: (b * nc + c, h)),
                   pl.BlockSpec((n_prev + 1, None, hp, dk, dv), lambda b, h, c, lg: (0, b, h, 0, 0))],
        scratch_shapes=[pltpu.VMEM((hp, dk, dv), F32)],
    )
    return pl.pallas_call(
        functools.partial(_retention_kernel, has_state0=state0 is not None, has_prev=prev_states is not None),
        grid_spec=grid_spec,
        out_shape=[jax.ShapeDtypeStruct((batch * seq, H * dv), BF16),
                   jax.ShapeDtypeStruct((n_prev + 1, batch, H, dk, dv), F32)],
        compiler_params=_params("arbitrary", "arbitrary", "arbitrary"),
        name="retention",
    )(log_g, *args)


def _att_prompt_kernel(q_ref, k_ref, v_ref, bias_ref, o_ref, kpad, vpad):
    S = q_ref.shape[0]
    QB = ATT_QBLOCK
    past = BAND_PAST_CHUNKS * CHUNK
    win = past + QB
    kpad[:past, :] = jnp.zeros((past, ATT_DH), BF16)
    vpad[:past, :] = jnp.zeros((past, ATT_DH), BF16)
    kpad[past:, :] = k_ref[...].astype(BF16)
    vpad[past:, :] = v_ref[...].astype(BF16)
    bias = bias_ref[...]
    qi = lax.broadcasted_iota(jnp.int32, (QB, win), 0)
    kj = lax.broadcasted_iota(jnp.int32, (QB, win), 1)
    chunk_shift = CHUNK.bit_length() - 1
    qc = jnp.right_shift(qi, chunk_shift)
    kc = jnp.right_shift(kj, chunk_shift)
    band = (kc >= qc) & (kc <= qc + BAND_PAST_CHUNKS)

    def body(i, carry):
        q0 = pl.multiple_of(i * QB, QB)
        qb = q_ref[pl.ds(q0, QB), :].astype(BF16)
        kw = kpad[pl.ds(q0, win), :]
        vw = vpad[pl.ds(q0, win), :]
        s = lax.dot_general(qb, kw, (((1,), (1,)), ((), ())), preferred_element_type=F32)
        s = s * ATT_DH ** -0.5 + bias
        valid = band & (kj >= past - q0)
        s = jnp.where(valid, s, NEG_INF)
        m = jnp.max(s, axis=-1, keepdims=True)
        e = jnp.exp(s - m)
        pr = (e / jnp.sum(e, axis=-1, keepdims=True)).astype(BF16)
        o_ref[pl.ds(q0, QB), :] = jnp.dot(pr, vw, preferred_element_type=F32).astype(o_ref.dtype)
        return carry

    lax.fori_loop(0, S // QB, body, 0, unroll=ATT_UNROLL)


def attention_prompt(qkv, bias, *, seq):
    H, dh = ATT_HEADS, ATT_DH
    past = BAND_PAST_CHUNKS * CHUNK
    return pl.pallas_call(
        _att_prompt_kernel,
        grid=(H,),
        in_specs=[pl.BlockSpec((seq, dh), lambda h: (0, h)),
                  pl.BlockSpec((seq, dh), lambda h: (0, H + h)),
                  pl.BlockSpec((seq, dh), lambda h: (0, 2 * H + h)),
                  pl.BlockSpec((None, ATT_QBLOCK, past + ATT_QBLOCK), lambda h: (h, 0, 0))],
        out_specs=pl.BlockSpec((seq, dh), lambda h: (0, h)),
        out_shape=jax.ShapeDtypeStruct((seq, H * dh), BF16),
        scratch_shapes=[pltpu.VMEM((past + seq, dh), BF16), pltpu.VMEM((past + seq, dh), BF16)],
        compiler_params=_params("arbitrary"),
        name="attention_prompt",
    )(qkv, qkv, qkv, bias)


def _att_sample_kernel(q_ref, kn_ref, vn_ref, ck_ref, cv_ref, bp_ref, bn_ref, o_ref):
    dh = ATT_DH
    H = ATT_HEADS
    W = ck_ref.shape[0] // H
    for h in range(H):
        sl = slice(h * dh, (h + 1) * dh)
        qb = q_ref[:, sl].astype(BF16)
        kn = kn_ref[:, sl].astype(BF16)
        vn = vn_ref[:, sl].astype(BF16)
        kp = ck_ref[pl.ds(h, W, stride=H), :].astype(BF16)
        vp = cv_ref[pl.ds(h, W, stride=H), :].astype(BF16)
        sp = lax.dot_general(qb, kp, (((1,), (1,)), ((), ())), preferred_element_type=F32)
        sn = lax.dot_general(qb, kn, (((1,), (1,)), ((), ())), preferred_element_type=F32)
        sp = sp * dh ** -0.5 + bp_ref[h]
        sn = sn * dh ** -0.5 + bn_ref[h]
        m = jnp.maximum(jnp.max(sp, axis=-1, keepdims=True), jnp.max(sn, axis=-1, keepdims=True))
        ep = jnp.exp(sp - m)
        en = jnp.exp(sn - m)
        den = jnp.sum(ep, axis=-1, keepdims=True) + jnp.sum(en, axis=-1, keepdims=True)
        o = jnp.dot((ep / den).astype(BF16), vp, preferred_element_type=F32)
        o = o + jnp.dot((en / den).astype(BF16), vn, preferred_element_type=F32)
        o_ref[:, sl] = o.astype(o_ref.dtype)


def attention_sample(qkv, cache_k, cache_v, bias_past, bias_new, *, batch, seq, row0, layer):
    H, dh = ATT_HEADS, ATT_DH
    W = cache_k.shape[2]
    rb0 = row0 // seq
    cache_k = cache_k.reshape(-1, dh)
    cache_v = cache_v.reshape(-1, dh)
    return pl.pallas_call(
        _att_sample_kernel,
        grid=(batch,),
        in_specs=[pl.BlockSpec((seq, H * dh), lambda b: (rb0 + b, 0)),
                  pl.BlockSpec((seq, H * dh), lambda b: (rb0 + b, 1)),
                  pl.BlockSpec((seq, H * dh), lambda b: (rb0 + b, 2)),
                  pl.BlockSpec((W * H, dh), lambda b: (layer * batch + b, 0)),
                  pl.BlockSpec((W * H, dh), lambda b: (layer * batch + b, 0)),
                  pl.BlockSpec((H, seq, W), lambda b: (0, 0, 0)),
                  pl.BlockSpec((H, seq, seq), lambda b: (0, 0, 0))],
        out_specs=pl.BlockSpec((seq, H * dh), lambda b: (b, 0)),
        out_shape=jax.ShapeDtypeStruct((batch * seq, H * dh), BF16),
        compiler_params=_params("arbitrary"),
        name="attention_sample",
    )(qkv, qkv, qkv, cache_k, cache_v, bias_past, bias_new)


def _rel_bias(table, n_q, n_k, offset):
    period = n_q + n_k
    d = jnp.arange(period)
    d = jnp.where(d < n_k, d, d - period)
    u = jnp.take(table, jnp.clip(offset - d, -REL_CLIP, REL_CLIP) + REL_CLIP, axis=0).T.astype(F32)
    H = u.shape[0]
    skew = jnp.tile(u, (1, n_q))[:, :n_q * (period - 1)].reshape(H, n_q, period - 1)
    return skew[:, :, :n_k]


def _gmlp_kernel(uv_ref, lng_ref, ws_ref, bs_ref, y_ref, vn_ref):
    DV = y_ref.shape[1]
    cg = DV // GMLP_GROUPS
    v = jax.nn.gelu(uv_ref[:, DV:])
    mu = jnp.mean(v, axis=-1, keepdims=True)
    d = v - mu
    var = jnp.mean(d * d, axis=-1, keepdims=True)
    vn = d * lax.rsqrt(var + EPS) * lng_ref[...]
    vn_ref[...] = vn
    for g in range(GMLP_GROUPS):
        sl = slice(g * cg, (g + 1) * cg)
        mixed = jnp.dot(ws_ref[g], vn[:, sl].astype(BF16), preferred_element_type=F32) + bs_ref[g]
        y_ref[:, sl] = (jax.nn.gelu(uv_ref[:, sl]) * mixed).astype(y_ref.dtype)


def gmlp_gate(uv, ln_g, ws, bs, *, n_prompt_tiles, tile):
    T = uv.shape[0]
    DV = uv.shape[1] // 2
    G = GMLP_GROUPS
    nt = T // tile
    last = n_prompt_tiles - 1
    return pl.pallas_call(
        _gmlp_kernel,
        grid=(nt,),
        in_specs=[pl.BlockSpec((tile, 2 * DV), lambda i: (i, 0)),
                  pl.BlockSpec((1, DV), lambda i: (0, 0)),
                  pl.BlockSpec((None, G, tile, tile), lambda i: (i // n_prompt_tiles, 0, 0, 0)),
                  pl.BlockSpec((None, G, tile, 1), lambda i: (i // n_prompt_tiles, 0, 0, 0))],
        out_specs=[pl.BlockSpec((tile, DV), lambda i: (i, 0)),
                   pl.BlockSpec((tile, DV), lambda i: (jnp.maximum(i - last, 0), 0))],
        out_shape=[jax.ShapeDtypeStruct((T, DV), BF16),
                   jax.ShapeDtypeStruct(((nt - last) * tile, DV), F32)],
        compiler_params=_params("arbitrary"),
        name="gmlp_gate",
    )(uv, ln_g.reshape(1, DV), ws, bs)


def _router_kernel(h_ref, w_ref, idx_ref, gate_ref):
    logits = jnp.dot(h_ref[...], w_ref[...].astype(BF16), preferred_element_type=F32)
    lane = lax.broadcasted_iota(jnp.int32, logits.shape, 1)
    lane_f = lane.astype(F32)
    l1 = jnp.where(lane < N_EXPERTS, logits, -jnp.inf)
    v1 = jnp.max(l1, axis=-1, keepdims=True)
    i1 = jnp.min(jnp.where(l1 == v1, lane_f, float(LANES)), axis=-1, keepdims=True)
    l2 = jnp.where(lane_f == i1, -jnp.inf, l1)
    v2 = jnp.max(l2, axis=-1, keepdims=True)
    i2 = jnp.min(jnp.where(l2 == v2, lane_f, float(LANES)), axis=-1, keepdims=True)
    e2 = jnp.exp(v2 - v1)
    den = 1.0 + e2
    idx_ref[...] = jnp.where(lane == 0, i1, jnp.where(lane == 1, i2, 0.0)).astype(jnp.int32)
    gate_ref[...] = jnp.where(lane == 0, 1.0 / den, jnp.where(lane == 1, e2 / den, 0.0))


def router(h, w_router_padded, *, tm):
    T, D = h.shape
    return pl.pallas_call(
        _router_kernel,
        grid=(T // tm,),
        in_specs=[pl.BlockSpec((tm, D), lambda i: (i, 0)), pl.BlockSpec((D, LANES), lambda i: (0, 0))],
        out_specs=[pl.BlockSpec((tm, LANES), lambda i: (i, 0)), pl.BlockSpec((tm, LANES), lambda i: (i, 0))],
        out_shape=[jax.ShapeDtypeStruct((T, LANES), jnp.int32), jax.ShapeDtypeStruct((T, LANES), F32)],
        compiler_params=_params("arbitrary"),
        name="router",
    )(h, w_router_padded)


def _gather_kernel(lo_ref, hi_ref, p0_ref, p1_ref, h_ref, o_ref, acc):
    m = pl.program_id(0)
    tm = o_ref.shape[0]
    sb = GATHER_SRC_BLOCK
    row = m * tm + lax.broadcasted_iota(jnp.int32, (tm, sb), 0)
    acc[...] = jnp.zeros_like(acc)

    def body(b, carry):
        select = jnp.where(p0_ref[b] == row, 1.0, jnp.where(p1_ref[b] == row, 1.0, 0.0)).astype(h_ref.dtype)
        acc[...] += jnp.dot(select, h_ref[pl.ds(pl.multiple_of(b * sb, sb), sb), :], preferred_element_type=F32)
        return carry

    lax.fori_loop(lo_ref[m], hi_ref[m] + 1, body, 0)
    o_ref[...] = acc[...].astype(o_ref.dtype)


def gather_rows(h, pos0, pos1, blk_lo, blk_hi, n_rows, *, tm):
    T, D = h.shape
    nsb = T // GATHER_SRC_BLOCK
    grid_spec = pltpu.PrefetchScalarGridSpec(
        num_scalar_prefetch=2,
        grid=(n_rows // tm,),
        in_specs=[pl.BlockSpec((nsb, 1, GATHER_SRC_BLOCK), lambda i, lo, hi: (0, 0, 0)),
                  pl.BlockSpec((nsb, 1, GATHER_SRC_BLOCK), lambda i, lo, hi: (0, 0, 0)),
                  pl.BlockSpec((T, D), lambda i, lo, hi: (0, 0), pipeline_mode=pl.Buffered(1))],
        out_specs=pl.BlockSpec((tm, D), lambda i, lo, hi: (i, 0)),
        scratch_shapes=[pltpu.VMEM((tm, D), F32)],
    )
    return pl.pallas_call(
        _gather_kernel,
        grid_spec=grid_spec,
        out_shape=jax.ShapeDtypeStruct((n_rows, D), h.dtype),
        compiler_params=_params("arbitrary"),
        name="moe_gather",
    )(blk_lo, blk_hi, pos0.reshape(nsb, 1, GATHER_SRC_BLOCK), pos1.reshape(nsb, 1, GATHER_SRC_BLOCK), h)


def _on_valid_rows(valid, o_ref, compute):
    tm = o_ref.shape[0]
    half = tm // 2

    @pl.when(valid > half)
    def _():
        compute(slice(0, tm))

    @pl.when((valid > 0) & (valid <= half))
    def _():
        compute(slice(0, half))
        o_ref[half:, :] = jnp.zeros((tm - half, o_ref.shape[1]), o_ref.dtype)

    @pl.when(valid == 0)
    def _():
        o_ref[...] = jnp.zeros_like(o_ref)


def _group_weights(te_ref, first_ref, nxt_ref, wrap_ref, copies, cast):
    n = pl.program_id(0)
    m = pl.program_id(1)

    @pl.when(first_ref[m] == 1)
    def _():
        @pl.when((n == 0) & (m == 0))
        def _():
            for cp in copies(te_ref[0], 0):
                cp.start()

        for cp in copies(te_ref[m], n):
            cp.wait()
        cast()
        n_next = n + wrap_ref[m]

        @pl.when(n_next < pl.num_programs(0))
        def _():
            for cp in copies(nxt_ref[m], n_next):
                cp.start()


def _moe_up_kernel(te_ref, first_ref, valid_ref, nxt_ref, wrap_ref, x_ref, w_hbm, o_ref,
                   stage_a, stage_b, wa_bf, wb_bf, sems):
    m = pl.program_id(1)
    tn = wa_bf.shape[1]
    F = w_hbm.shape[2] // 2

    def copies(e, n):
        col = pl.multiple_of(n * tn, tn)
        return (pltpu.make_async_copy(w_hbm.at[e, :, pl.ds(col, tn)], stage_a, sems.at[0]),
                pltpu.make_async_copy(w_hbm.at[e, :, pl.ds(pl.multiple_of(F + col, tn), tn)], stage_b, sems.at[1]))

    def cast():
        wa_bf[...] = stage_a[...].astype(BF16)
        wb_bf[...] = stage_b[...].astype(BF16)

    _group_weights(te_ref, first_ref, nxt_ref, wrap_ref, copies, cast)

    def compute(rows):
        x = x_ref[rows, :]
        a = jnp.dot(x, wa_bf[...], preferred_element_type=F32)
        b = jnp.dot(x, wb_bf[...], preferred_element_type=F32)
        o_ref[rows, :] = (_silu(a) * b).astype(o_ref.dtype)

    _on_valid_rows(valid_ref[m], o_ref, compute)


def moe_up(xs, w_gu, plan, *, tm, tn):
    P, D = xs.shape
    F = w_gu.shape[2] // 2
    grid_spec = pltpu.PrefetchScalarGridSpec(
        num_scalar_prefetch=len(plan),
        grid=(F // tn, P // tm),
        in_specs=[pl.BlockSpec((tm, D), lambda n, m, *_: (m, 0)),
                  pl.BlockSpec(memory_space=pl.ANY)],
        out_specs=pl.BlockSpec((tm, tn), lambda n, m, *_: (m, n)),
        scratch_shapes=[pltpu.VMEM((D, tn), w_gu.dtype), pltpu.VMEM((D, tn), w_gu.dtype),
                        pltpu.VMEM((D, tn), BF16), pltpu.VMEM((D, tn), BF16), pltpu.SemaphoreType.DMA((2,))],
    )
    return pl.pallas_call(
        _moe_up_kernel,
        grid_spec=grid_spec,
        out_shape=jax.ShapeDtypeStruct((P, F), BF16),
        compiler_params=_params("arbitrary", "arbitrary"),
        name="moe_up",
    )(*plan, xs, w_gu)


def _moe_down_kernel(te_ref, first_ref, valid_ref, nxt_ref, wrap_ref, x_ref, w_hbm, o_ref, stage, w_bf, sem):
    m = pl.program_id(1)
    tn = w_bf.shape[1]

    def copies(e, n):
        return (pltpu.make_async_copy(w_hbm.at[e, :, pl.ds(pl.multiple_of(n * tn, tn), tn)], stage, sem),)

    def cast():
        w_bf[...] = stage[...].astype(BF16)

    _group_weights(te_ref, first_ref, nxt_ref, wrap_ref, copies, cast)

    def compute(rows):
        o_ref[rows, :] = jnp.dot(x_ref[rows, :], w_bf[...], preferred_element_type=F32)

    _on_valid_rows(valid_ref[m], o_ref, compute)


def moe_down(hm, w_down, plan, *, tm, tn):
    P, F = hm.shape
    D = w_down.shape[2]
    grid_spec = pltpu.PrefetchScalarGridSpec(
        num_scalar_prefetch=len(plan),
        grid=(D // tn, P // tm),
        in_specs=[pl.BlockSpec((tm, F), lambda n, m, *_: (m, 0)),
                  pl.BlockSpec(memory_space=pl.ANY)],
        out_specs=pl.BlockSpec((tm, tn), lambda n, m, *_: (m, n)),
        scratch_shapes=[pltpu.VMEM((F, tn), w_down.dtype), pltpu.VMEM((F, tn), BF16), pltpu.SemaphoreType.DMA(())],
    )
    return pl.pallas_call(
        _moe_down_kernel,
        grid_spec=grid_spec,
        out_shape=jax.ShapeDtypeStruct((P, D), F32),
        compiler_params=_params("arbitrary", "arbitrary"),
        name="moe_down",
    )(*plan, hm, w_down)


def _row_copy(y_hbm, buf, src_row, dst_row, sem):
    return pltpu.make_async_copy(y_hbm.at[pl.ds(src_row, 1), :], buf.at[pl.ds(dst_row, 1), :], sem)


def _combine_norm_kernel(p0_ref, p1_ref, x_ref, gate_ref, g_ref, y_hbm, xo_ref, hn_ref, buf0, buf1, sems):
    tm = x_ref.shape[0]
    base = pl.program_id(0) * tm

    def start(r, carry):
        _row_copy(y_hbm, buf0, p0_ref[base + r], r, sems.at[0]).start(priority=0)
        _row_copy(y_hbm, buf1, p1_ref[base + r], r, sems.at[1]).start(priority=DMA_PRIORITIES - 1)
        return carry

    def wait(r, carry):
        _row_copy(y_hbm, buf0, 0, r, sems.at[0]).wait()
        _row_copy(y_hbm, buf1, 0, r, sems.at[1]).wait()
        return carry

    lax.fori_loop(0, tm, start, 0)
    lax.fori_loop(0, tm, wait, 0)
    gate = gate_ref[...]
    xn = x_ref[...] + (gate[:, 0:1] * buf0[...] + gate[:, 1:2] * buf1[...])
    xo_ref[...] = xn
    hn = xn * lax.rsqrt(jnp.mean(xn * xn, axis=-1, keepdims=True) + EPS) * g_ref[...]
    hn_ref[...] = hn.astype(hn_ref.dtype)


def moe_combine_norm(x, gates, ys, pos0, pos1, g, *, tm):
    T, D = x.shape
    grid_spec = pltpu.PrefetchScalarGridSpec(
        num_scalar_prefetch=2,
        grid=(T // tm,),
        in_specs=[pl.BlockSpec((tm, D), lambda i, a, b: (i, 0)),
                  pl.BlockSpec((tm, LANES), lambda i, a, b: (i, 0)),
                  pl.BlockSpec((1, D), lambda i, a, b: (0, 0)),
                  pl.BlockSpec(memory_space=pl.ANY)],
        out_specs=[pl.BlockSpec((tm, D), lambda i, a, b: (i, 0)), pl.BlockSpec((tm, D), lambda i, a, b: (i, 0))],
        scratch_shapes=[pltpu.VMEM((tm, D), F32), pltpu.VMEM((tm, D), F32), pltpu.SemaphoreType.DMA((2,))],
    )
    return pl.pallas_call(
        _combine_norm_kernel,
        grid_spec=grid_spec,
        out_shape=[jax.ShapeDtypeStruct((T, D), F32), jax.ShapeDtypeStruct((T, D), BF16)],
        compiler_params=_params("arbitrary"),
        name="moe_combine_norm",
    )(pos0, pos1, x, gates, g.reshape(1, D), ys)


def _route_plan(idx, tm):
    T = idx.shape[0]
    E = N_EXPERTS
    n_tiles = (TOP_K * T) // tm + E
    experts = jnp.arange(E, dtype=jnp.int32)
    flat_e = idx.reshape(-1)
    onehot = (flat_e[:, None] == experts[None, :]).astype(jnp.int32)
    csum = jnp.cumsum(onehot, axis=0)
    rank = jnp.sum((csum - onehot) * onehot, axis=1)
    counts = csum[-1]
    tiles_e = (counts + tm - 1) // tm
    tile_end = jnp.cumsum(tiles_e)
    tile_start = tile_end - tiles_e
    pos = (jnp.sum(onehot * tile_start[None, :], axis=1) * tm + rank).astype(jnp.int32)
    n_used = tile_end[-1]
    m_ids = jnp.arange(n_tiles, dtype=jnp.int32)
    te = jnp.sum((jnp.minimum(m_ids, n_used - 1)[:, None] >= tile_end[None, :]).astype(jnp.int32), axis=1)
    first = ((m_ids == 0) | (te != jnp.roll(te, 1))) & (m_ids < n_used)
    te_onehot = (te[:, None] == experts[None, :]).astype(jnp.int32)
    rows_before = (m_ids - jnp.sum(te_onehot * tile_start[None, :], axis=1)) * tm
    valid = jnp.clip(jnp.sum(te_onehot * counts[None, :], axis=1) - rows_before, 0, tm)
    valid = jnp.where(m_ids < n_used, valid, 0).astype(jnp.int32)
    nonempty = jnp.where(tiles_e > 0, experts, E)
    after = jnp.min(jnp.where(experts[None, :] > experts[:, None], nonempty[None, :], E), axis=1)
    nxt_e = jnp.where(after < E, after, jnp.min(nonempty))
    nxt = jnp.sum(te_onehot * nxt_e[None, :], axis=1).astype(jnp.int32)
    wrap = jnp.sum(te_onehot * (after == E).astype(jnp.int32)[None, :], axis=1).astype(jnp.int32)
    csum_tile = jnp.sum(csum[:, :, None] * te_onehot.T[None, :, :], axis=1)
    pair_first = jnp.sum((csum_tile <= rows_before[None, :]).astype(jnp.int32), axis=0)
    pair_last = jnp.sum((csum_tile <= (rows_before + valid - 1)[None, :]).astype(jnp.int32), axis=0)
    blk_lo = jnp.where(valid > 0, pair_first // (TOP_K * GATHER_SRC_BLOCK), 0).astype(jnp.int32)
    blk_hi = jnp.where(valid > 0, pair_last // (TOP_K * GATHER_SRC_BLOCK), -1).astype(jnp.int32)
    return (pos[0::TOP_K], pos[1::TOP_K], blk_lo, blk_hi,
            (te.astype(jnp.int32), first.astype(jnp.int32), valid, nxt, wrap))


def moe(x, h, w_router, w_gu, w_down, g_next, *, layer, tm_tok):
    T = x.shape[0]
    E = N_EXPERTS
    w_r = jnp.pad(w_router[layer], ((0, 0), (0, LANES - E)))
    idx, gates = router(h, w_r, tm=tm_tok)
    pos0, pos1, blk_lo, blk_hi, (te, first, valid, nxt, wrap) = _route_plan(idx[:, :TOP_K], MOE_TM)
    plan = (te + layer * E, first, valid, nxt + layer * E, wrap)
    n_rows = te.shape[0] * MOE_TM
    xs = gather_rows(h, pos0, pos1, blk_lo, blk_hi, n_rows, tm=MOE_TM)
    hm = moe_up(xs, w_gu.reshape((-1,) + w_gu.shape[2:]), plan, tm=MOE_TM, tn=1024)
    ys = moe_down(hm, w_down.reshape((-1,) + w_down.shape[2:]), plan, tm=MOE_TM, tn=512)
    return moe_combine_norm(x, gates, ys, pos0, pos1, g_next, tm=tm_tok)


def kernel(x_prompt, x_sample, state_ret, cache_k, cache_v, p_prompt, p_sample, g_mix, g_ffn, g_ple, g_final,
           ret_w_in, ret_w_out, att_w_in, att_w_out, att_rel_bias, gmlp_w_in, gmlp_ln_g, gmlp_w_s, gmlp_b_s,
           gmlp_w_out, ffn_w_gu, ffn_w_down, moe_w_router, moe_w_gu, moe_w_down, ple_w_in, ple_w_gate):
    B, S, D = x_prompt.shape
    BS, L, _ = x_sample.shape
    depth = g_mix.shape[0]
    NP, NS = B * S, BS * L
    T = NP + NS
    assert B == 1 and S % RET_CHUNK == 0 and S % ATT_QBLOCK == 0 and L <= CHUNK and NP % NS == 0
    TM = T // 8
    TMS = T // 16

    x = jnp.concatenate([x_prompt.reshape(NP, D), x_sample.reshape(NS, D)], axis=0)
    p_all = jnp.concatenate([p_prompt.reshape(depth, NP, -1), p_sample.reshape(depth, NS, -1)], axis=1)

    pos = jnp.concatenate([jnp.arange(S, dtype=jnp.int32), jnp.tile(PAST_LEN + jnp.arange(L, dtype=jnp.int32), BS)])
    freqs = ROPE_BASE ** (-jnp.arange(0, RET_DK, 2, dtype=F32) / RET_DK)
    ang = pos.astype(F32)[:, None] * freqs[None, :]
    cos, sin = jnp.cos(ang), jnp.sin(ang)
    log_g = jnp.log1p(-jnp.exp2(-5.0 - jnp.arange(RET_HEADS, dtype=F32)))

    ret_p, ret_s = None, None
    k_p, v_p, k_s, v_s, gv = [], [], [], [], []
    (h,) = rmsnorm(x, g_mix[0], [BF16], TMS)
    for i in range(depth):
        kind, slot = i % 3, i // 3
        if kind == 0:
            qkvg = matmul(h, ret_w_in, layer=slot, tm=TM, tn=1024)
            y_p, ret_p = retention(qkvg, cos, sin, log_g, None, ret_p, batch=B, seq=S, chunk=RET_CHUNK, row0=0,
                                   heads_per_step=RET_HEADS // 2)
            y_s, ret_s = retention(qkvg, cos, sin, log_g, state_ret, ret_s, batch=BS, seq=L, chunk=L, row0=NP,
                                   heads_per_step=RET_HEADS, state_layer=slot)
            x = matmul(y_p, ret_w_out, layer=slot, tm=NS, tn=512, res=x, a_tail=y_s)
        elif kind == 1:
            HD = ATT_HEADS * ATT_DH
            W = cache_k.shape[2]
            past = BAND_PAST_CHUNKS * CHUNK
            qkv = matmul(h, att_w_in, layer=slot, tm=TM, tn=1024)
            bias_p = _rel_bias(att_rel_bias[slot], ATT_QBLOCK, past + ATT_QBLOCK, past)
            bias_s = _rel_bias(att_rel_bias[slot], L, W + L, W)
            o_p = attention_prompt(qkv, bias_p, seq=S)
            o_s = attention_sample(qkv, cache_k, cache_v, bias_s[:, :, :W], bias_s[:, :, W:],
                                   batch=BS, seq=L, row0=NP, layer=slot)
            w = min(past, S)
            k_p.append(qkv[NP - w:NP, HD:2 * HD].reshape(B, w, ATT_HEADS, ATT_DH))
            v_p.append(qkv[NP - w:NP, 2 * HD:].reshape(B, w, ATT_HEADS, ATT_DH))
            k_s.append(qkv[NP:, HD:2 * HD].reshape(BS, L, ATT_HEADS, ATT_DH))
            v_s.append(qkv[NP:, 2 * HD:].reshape(BS, L, ATT_HEADS, ATT_DH))
            x = matmul(o_p, att_w_out, layer=slot, tm=NS, tn=1024, res=x, a_tail=o_s)
        else:
            tile = GMLP_CHUNK
            uv = matmul(h, gmlp_w_in, layer=slot, tm=TM, tn=1024)
            r = jnp.arange(tile)
            ws_p = jnp.where((r[:, None] >= r[None, :])[None], gmlp_w_s[slot][:, :tile, :tile], 0.0)
            same = (r[:, None] // L == r[None, :] // L) & (r[:, None] >= r[None, :])
            ws_s = jnp.where(same[None], jnp.tile(gmlp_w_s[slot][:, :L, :L], (1, tile // L, tile // L)), 0.0)
            bs_p = gmlp_b_s[slot][:, :tile]
            bs_s = jnp.tile(gmlp_b_s[slot][:, :L], (1, tile // L))
            ws = jnp.stack([ws_p, ws_s]).astype(BF16)
            bs = jnp.stack([bs_p, bs_s])[..., None]
            yg, vn = gmlp_gate(uv, gmlp_ln_g[slot], ws, bs, n_prompt_tiles=NP // tile, tile=tile)
            gv.append((vn[:tile].reshape(B, tile, -1), vn[tile:].reshape(BS, L, -1)))
            x = matmul(yg, gmlp_w_out, layer=slot, tm=TMS, tn=512, res=x)

        if i % 2 == 0:
            (h,) = rmsnorm(x, g_ffn[i], [BF16], TMS)
            x = dense_ffn(x, h, ffn_w_gu, ffn_w_down, layer=i // 2, tm_up=TM, tm_down=TMS)
            (hn,) = rmsnorm(x, g_ple[i], [BF16], TMS)
        else:
            (h,) = rmsnorm(x, g_ffn[i], [BF16], TMS)
            x, hn = moe(x, h, moe_w_router, moe_w_gu, moe_w_down, g_ple[i], layer=i // 2, tm_tok=TMS)

        g_next, next_dtype = (g_mix[i + 1], BF16) if i + 1 < depth else (g_final, F32)
        x, h = ple_norm(x, hn, p_all, ple_w_gate, ple_w_in, g_next, next_dtype, layer=i, tm=T // 32)

    y = h
    y_prompt = y[:NP].reshape(B, S, D)
    y_sample = y[NP:].reshape(BS, L, D)
    return (y_prompt, y_sample, ret_p, ret_s, jnp.stack(k_p), jnp.stack(v_p),
            jnp.stack(k_s), jnp.stack(v_s), jnp.stack([a for a, _ in gv]), jnp.stack([b for _, b in gv]))
```

```python
import functools

import jax
import jax.numpy as jnp
from jax import lax
from jax.experimental import pallas as pl
from jax.experimental.pallas import tpu as pltpu

F32 = jnp.float32
BF16 = jnp.bfloat16

EPS = 1e-6
NEG_INF = -1e30
CHUNK = 64
ROPE_BASE = 10000.0
RET_HEADS, RET_DK, RET_DV = 8, 256, 512
ATT_HEADS, ATT_DH = 16, 128
BAND_PAST_CHUNKS = 8
REL_CLIP = 128
GMLP_CHUNK, GMLP_GROUPS = 128, 8
N_EXPERTS, TOP_K = 8, 2
PAST_LEN = 2048

LANES = 128
DMA_PRIORITIES = 2
V7X_VMEM_LIMIT_BYTES = 60 * 1000 * 1024

RET_CHUNK = 256
ATT_QBLOCK = 4 * CHUNK
ATT_UNROLL = 4
MOE_TM = 512
GATHER_SRC_BLOCK = 512


def _params(*sem):
    return pltpu.CompilerParams(dimension_semantics=sem, vmem_limit_bytes=V7X_VMEM_LIMIT_BYTES)


def _silu(x):
    return x * jax.nn.sigmoid(x)


def _rmsnorm_kernel(x_ref, g_ref, *o_refs):
    x = x_ref[...]
    y = x * lax.rsqrt(jnp.mean(x * x, axis=-1, keepdims=True) + EPS) * g_ref[...]
    for o_ref in o_refs:
        o_ref[...] = y.astype(o_ref.dtype)


def rmsnorm(x, g, dtypes, tm):
    T, D = x.shape
    out_specs = [pl.BlockSpec((tm, D), lambda i: (i, 0)) for _ in dtypes]
    out_shape = [jax.ShapeDtypeStruct((T, D), dt) for dt in dtypes]
    return pl.pallas_call(
        _rmsnorm_kernel,
        grid=(T // tm,),
        in_specs=[pl.BlockSpec((tm, D), lambda i: (i, 0)), pl.BlockSpec((1, D), lambda i: (0, 0))],
        out_specs=out_specs,
        out_shape=out_shape,
        compiler_params=_params("arbitrary"),
        name="rmsnorm",
    )(x, g.reshape(1, D))


def _mm_kernel(a_ref, *rest, n_head, has_res, cast_w):
    rest = list(rest)
    t_ref = rest.pop(0) if n_head is not None else None
    w_ref = rest.pop(0)
    r_ref = rest.pop(0) if has_res else None
    o_ref, *scratch = rest
    if cast_w:
        w_bf = scratch[0]

        @pl.when(pl.program_id(1) == 0)
        def _():
            w_bf[...] = w_ref[...].astype(BF16)

        w = w_bf[...]
    else:
        w = w_ref[...]
    a = a_ref[...]
    if n_head is not None:
        a = jnp.where(pl.program_id(1) < n_head, a, t_ref[...])
    acc = jnp.dot(a, w, preferred_element_type=F32)
    if has_res:
        acc = r_ref[...] + acc
    o_ref[...] = acc.astype(o_ref.dtype)


def _weight_spec(w, layer, tn):
    K = w.shape[-2]
    if w.ndim == 2:
        return pl.BlockSpec((K, tn), lambda n, m: (0, n))
    return pl.BlockSpec((None, K, tn), lambda n, m: (layer, 0, n))


def matmul(a, w, *, tm, tn, layer=0, out_dtype=F32, res=None, a_tail=None):
    K = a.shape[1]
    N = w.shape[-1]
    cast_w = w.dtype != BF16
    n_head = None
    M = a.shape[0]
    in_specs = [pl.BlockSpec((tm, K), lambda n, m: (m, 0))]
    args = [a]
    if a_tail is not None:
        assert a.shape[0] % tm == 0 and a_tail.shape[0] % tm == 0
        n_head = a.shape[0] // tm
        M += a_tail.shape[0]
        in_specs = [pl.BlockSpec((tm, K), lambda n, m: (jnp.minimum(m, n_head - 1), 0)),
                    pl.BlockSpec((tm, K), lambda n, m: (jnp.maximum(m - n_head, 0), 0))]
        args.append(a_tail)
    in_specs.append(_weight_spec(w, layer, tn))
    args.append(w)
    if res is not None:
        in_specs.append(pl.BlockSpec((tm, tn), lambda n, m: (m, n)))
        args.append(res)
    return pl.pallas_call(
        functools.partial(_mm_kernel, n_head=n_head, has_res=res is not None, cast_w=cast_w),
        grid=(N // tn, M // tm),
        in_specs=in_specs,
        out_specs=pl.BlockSpec((tm, tn), lambda n, m: (m, n)),
        out_shape=jax.ShapeDtypeStruct((M, N), out_dtype),
        scratch_shapes=[pltpu.VMEM((K, tn), BF16)] if cast_w else [],
        compiler_params=_params("arbitrary", "arbitrary"),
        name="matmul",
    )(*args)


def _swiglu_up_kernel(x_ref, w_hbm, o_ref, stage_a, stage_b, wa_bf, wb_bf, sems, *, layer, tail):
    n = pl.program_id(0)
    nb = pl.num_programs(0)
    tn = wa_bf.shape[1]
    F = w_hbm.shape[2] // 2

    def copies(nn, width):
        col = nn * tn
        a = pltpu.make_async_copy(w_hbm.at[layer, :, pl.ds(pl.multiple_of(col, LANES), width)],
                                  stage_a.at[:, pl.ds(0, width)], sems.at[0])
        b = pltpu.make_async_copy(w_hbm.at[layer, :, pl.ds(pl.multiple_of(F + col, LANES), width)],
                                  stage_b.at[:, pl.ds(0, width)], sems.at[1])
        return a, b

    def for_tile(nn, action):
        if tail == tn:
            for cp in copies(nn, tn):
                action(cp)
        else:
            @pl.when(nn < nb - 1)
            def _():
                for cp in copies(nn, tn):
                    action(cp)

            @pl.when(nn == nb - 1)
            def _():
                for cp in copies(nn, tail):
                    action(cp)

    @pl.when(pl.program_id(1) == 0)
    def _():
        @pl.when(n == 0)
        def _():
            for_tile(n, lambda cp: cp.start())

        for_tile(n, lambda cp: cp.wait())
        wa_bf[...] = stage_a[...].astype(BF16)
        wb_bf[...] = stage_b[...].astype(BF16)

        @pl.when(n + 1 < nb)
        def _():
            for_tile(n + 1, lambda cp: cp.start())

    x = x_ref[...]
    a = jnp.dot(x, wa_bf[...], preferred_element_type=F32)
    b = jnp.dot(x, wb_bf[...], preferred_element_type=F32)
    o_ref[...] = (_silu(a) * b).astype(o_ref.dtype)


def swiglu_up(x, w_gu, *, layer, tm, tn):
    T, D = x.shape
    F = w_gu.shape[2] // 2
    nb = pl.cdiv(F, tn)
    tail = F - (nb - 1) * tn
    assert F % LANES == 0 and (nb > 1 or tail == tn)
    return pl.pallas_call(
        functools.partial(_swiglu_up_kernel, layer=layer, tail=tail),
        grid=(nb, T // tm),
        in_specs=[pl.BlockSpec((tm, D), lambda n, m: (m, 0)), pl.BlockSpec(memory_space=pl.ANY)],
        out_specs=pl.BlockSpec((tm, tn), lambda n, m: (m, n)),
        out_shape=jax.ShapeDtypeStruct((T, F), BF16),
        scratch_shapes=[pltpu.VMEM((D, tn), w_gu.dtype), pltpu.VMEM((D, tn), w_gu.dtype),
                        pltpu.VMEM((D, tn), BF16), pltpu.VMEM((D, tn), BF16), pltpu.SemaphoreType.DMA((2,))],
        compiler_params=_params("arbitrary", "arbitrary"),
        name="swiglu_up",
    )(x, w_gu)


def dense_ffn(x, h, w_gu, w_down, *, layer, tm_up, tm_down):
    hm = swiglu_up(h, w_gu, layer=layer, tm=tm_up, tn=512)
    return matmul(hm, w_down, layer=layer, tm=tm_down, tn=512, res=x)


def _ple_norm_kernel(hn_ref, wg_ref, p_ref, wi_ref, x_ref, gn_ref, o_ref, hnext_ref, wg_bf, wi_bf):
    @pl.when(pl.program_id(0) == 0)
    def _():
        wg_bf[...] = wg_ref[...].astype(BF16)
        wi_bf[...] = wi_ref[...].astype(BF16)

    gate = jax.nn.sigmoid(jnp.dot(hn_ref[...], wg_bf[...], preferred_element_type=F32))
    emb = jnp.dot(p_ref[...].astype(BF16), wi_bf[...], preferred_element_type=F32)
    xn = x_ref[...] + gate * emb
    o_ref[...] = xn
    hnext = xn * lax.rsqrt(jnp.mean(xn * xn, axis=-1, keepdims=True) + EPS) * gn_ref[...]
    hnext_ref[...] = hnext.astype(hnext_ref.dtype)


def ple_norm(x, hn, p, w_gate, w_in, g_next, next_dtype, *, layer, tm):
    T, D = x.shape
    DP = p.shape[2]
    resident = pl.Buffered(1)
    return pl.pallas_call(
        _ple_norm_kernel,
        grid=(T // tm,),
        in_specs=[pl.BlockSpec((tm, D), lambda m: (m, 0)),
                  pl.BlockSpec((None, D, D), lambda m: (layer, 0, 0), pipeline_mode=resident),
                  pl.BlockSpec((None, tm, DP), lambda m: (layer, m, 0)),
                  pl.BlockSpec((None, DP, D), lambda m: (layer, 0, 0), pipeline_mode=resident),
                  pl.BlockSpec((tm, D), lambda m: (m, 0)),
                  pl.BlockSpec((1, D), lambda m: (0, 0))],
        out_specs=[pl.BlockSpec((tm, D), lambda m: (m, 0)), pl.BlockSpec((tm, D), lambda m: (m, 0))],
        out_shape=[jax.ShapeDtypeStruct((T, D), F32), jax.ShapeDtypeStruct((T, D), next_dtype)],
        scratch_shapes=[pltpu.VMEM((D, D), BF16), pltpu.VMEM((DP, D), BF16)],
        compiler_params=_params("arbitrary"),
        name="ple_norm",
    )(hn, w_gate, p, w_in, x, g_next.reshape(1, D))


def _retention_kernel(lg_ref, q_ref, k_ref, v_ref, g_ref, cos_ref, sin_ref, *rest, has_state0, has_prev):
    rest = list(rest)
    s0_ref = rest.pop(0) if has_state0 else None
    prev_ref = rest.pop(0) if has_prev else None
    y_ref, sout_ref, state = rest
    n_prev = sout_ref.shape[0] - 1
    hb = pl.program_id(1)
    c = pl.program_id(2)
    C = q_ref.shape[0]
    hp = state.shape[0]
    dk, dv = RET_DK, RET_DV
    half = dk // 2

    @pl.when(c == 0)
    def _():
        if has_state0:
            state[...] = s0_ref[...]
        else:
            state[...] = jnp.zeros_like(state)

    cos = cos_ref[...]
    sin = sin_ref[...]

    def rotary(x):
        x1, x2 = x[:, :half], x[:, half:]
        return jnp.concatenate([x1 * cos - x2 * sin, x1 * sin + x2 * cos], axis=-1)

    row = lax.broadcasted_iota(jnp.int32, (C, C), 0)
    col = lax.broadcasted_iota(jnp.int32, (C, C), 1)
    rel = (row - col).astype(F32)
    rel_pos = jnp.maximum(rel, 0.0)
    idx = lax.broadcasted_iota(jnp.int32, (C, 1), 0).astype(F32)

    for j in range(hp):
        lg = lg_ref[hb * hp + j]
        q = rotary(q_ref[:, j * dk:(j + 1) * dk])
        k = rotary(k_ref[:, j * dk:(j + 1) * dk]) * dk ** -0.5
        vb = v_ref[:, j * dv:(j + 1) * dv].astype(BF16)
        decay = jnp.where(rel >= 0, jnp.exp(rel_pos * lg), 0.0)
        q_decay = jnp.exp((idx + 1.0) * lg)
        k_decay = jnp.exp((C - 1.0 - idx) * lg)
        chunk_decay = jnp.exp(jnp.zeros((1, 1), F32) + C * lg)

        qb = q.astype(BF16)
        scores = lax.dot_general(qb, k.astype(BF16), (((1,), (1,)), ((), ())), preferred_element_type=F32) * decay
        o = jnp.dot(scores.astype(BF16), vb, preferred_element_type=F32)
        st = state[j]
        o = o + jnp.dot(qb, st.astype(BF16), preferred_element_type=F32) * q_decay
        kv = lax.dot_general((k * k_decay).astype(BF16), vb, (((0,), (0,)), ((), ())), preferred_element_type=F32)
        st = st * chunk_decay + kv
        state[j] = st
        sout_ref[n_prev, j] = st

        o = o * lax.rsqrt(jnp.mean(o * o, axis=-1, keepdims=True) + EPS)
        y_ref[:, j * dv:(j + 1) * dv] = (o * _silu(g_ref[:, j * dv:(j + 1) * dv])).astype(y_ref.dtype)

    if has_prev:
        @pl.when(c == pl.num_programs(2) - 1)
        def _():
            sout_ref[:n_prev] = prev_ref[...]


def retention(qkvg, cos, sin, log_g, state0, prev_states, *, batch, seq, chunk, row0, heads_per_step, state_layer=0):
    H, dk, dv = RET_HEADS, RET_DK, RET_DV
    hp = heads_per_step
    nhb = H // hp
    nc = seq // chunk
    rb0 = row0 // chunk
    kq = nhb
    kv = (2 * H * dk) // (hp * dv)
    kg = kv + nhb

    def rows(b, c):
        return rb0 + b * nc + c

    in_specs = [
        pl.BlockSpec((chunk, hp * dk), lambda b, h, c, lg: (rows(b, c), h)),
        pl.BlockSpec((chunk, hp * dk), lambda b, h, c, lg: (rows(b, c), kq + h)),
        pl.BlockSpec((chunk, hp * dv), lambda b, h, c, lg: (rows(b, c), kv + h)),
        pl.BlockSpec((chunk, hp * dv), lambda b, h, c, lg: (rows(b, c), kg + h)),
        pl.BlockSpec((chunk, dk // 2), lambda b, h, c, lg: (rows(b, c), 0)),
        pl.BlockSpec((chunk, dk // 2), lambda b, h, c, lg: (rows(b, c), 0)),
    ]
    args = [qkvg, qkvg, qkvg, qkvg, cos, sin]
    if state0 is not None:
        in_specs.append(pl.BlockSpec((None, None, hp, dk, dv), lambda b, h, c, lg: (state_layer, b, h, 0, 0)))
        args.append(state0)
    n_prev = 0 if prev_states is None else prev_states.shape[0]
    if prev_states is not None:
        in_specs.append(pl.BlockSpec((n_prev, None, hp, dk, dv), lambda b, h, c, lg: (0, b, h, 0, 0)))
        args.append(prev_states)
    grid_spec = pltpu.PrefetchScalarGridSpec(
        num_scalar_prefetch=1,
        grid=(batch, nhb, nc),
        in_specs=in_specs,
        out_specs=[pl.BlockSpec((chunk, hp * dv), lambda b, h, c, lg: (b * nc + c, h)),
                   pl.BlockSpec((n_prev + 1, None, hp, dk, dv), lambda b, h, c, lg: (0, b, h, 0, 0))],
        scratch_shapes=[pltpu.VMEM((hp, dk, dv), F32)],
    )
    return pl.pallas_call(
        functools.partial(_retention_kernel, has_state0=state0 is not None, has_prev=prev_states is not None),
        grid_spec=grid_spec,
        out_shape=[jax.ShapeDtypeStruct((batch * seq, H * dv), BF16),
                   jax.ShapeDtypeStruct((n_prev + 1, batch, H, dk, dv), F32)],
        compiler_params=_params("arbitrary", "arbitrary", "arbitrary"),
        name="retention",
    )(log_g, *args)


def _att_prompt_kernel(q_ref, k_ref, v_ref, bias_ref, o_ref, kpad, vpad):
    S = q_ref.shape[0]
    QB = ATT_QBLOCK
    past = BAND_PAST_CHUNKS * CHUNK
    win = past + QB
    kpad[:past, :] = jnp.zeros((past, ATT_DH), BF16)
    vpad[:past, :] = jnp.zeros((past, ATT_DH), BF16)
    kpad[past:, :] = k_ref[...].astype(BF16)
    vpad[past:, :] = v_ref[...].astype(BF16)
    bias = bias_ref[...]
    qi = lax.broadcasted_iota(jnp.int32, (QB, win), 0)
    kj = lax.broadcasted_iota(jnp.int32, (QB, win), 1)
    chunk_shift = CHUNK.bit_length() - 1
    qc = jnp.right_shift(qi, chunk_shift)
    kc = jnp.right_shift(kj, chunk_shift)
    band = (kc >= qc) & (kc <= qc + BAND_PAST_CHUNKS)

    def body(i, carry):
        q0 = pl.multiple_of(i * QB, QB)
        qb = q_ref[pl.ds(q0, QB), :].astype(BF16)
        kw = kpad[pl.ds(q0, win), :]
        vw = vpad[pl.ds(q0, win), :]
        s = lax.dot_general(qb, kw, (((1,), (1,)), ((), ())), preferred_element_type=F32)
        s = s * ATT_DH ** -0.5 + bias
        valid = band & (kj >= past - q0)
        s = jnp.where(valid, s, NEG_INF)
        m = jnp.max(s, axis=-1, keepdims=True)
        e = jnp.exp(s - m)
        pr = (e / jnp.sum(e, axis=-1, keepdims=True)).astype(BF16)
        o_ref[pl.ds(q0, QB), :] = jnp.dot(pr, vw, preferred_element_type=F32).astype(o_ref.dtype)
        return carry

    lax.fori_loop(0, S // QB, body, 0, unroll=ATT_UNROLL)


def attention_prompt(qkv, bias, *, seq):
    H, dh = ATT_HEADS, ATT_DH
    past = BAND_PAST_CHUNKS * CHUNK
    return pl.pallas_call(
        _att_prompt_kernel,
        grid=(H,),
        in_specs=[pl.BlockSpec((seq, dh), lambda h: (0, h)),
                  pl.BlockSpec((seq, dh), lambda h: (0, H + h)),
                  pl.BlockSpec((seq, dh), lambda h: (0, 2 * H + h)),
                  pl.BlockSpec((None, ATT_QBLOCK, past + ATT_QBLOCK), lambda h: (h, 0, 0))],
        out_specs=pl.BlockSpec((seq, dh), lambda h: (0, h)),
        out_shape=jax.ShapeDtypeStruct((seq, H * dh), BF16),
        scratch_shapes=[pltpu.VMEM((past + seq, dh), BF16), pltpu.VMEM((past + seq, dh), BF16)],
        compiler_params=_params("arbitrary"),
        name="attention_prompt",
    )(qkv, qkv, qkv, bias)


def _att_sample_kernel(q_ref, kn_ref, vn_ref, ck_ref, cv_ref, bp_ref, bn_ref, o_ref):
    dh = ATT_DH
    H = ATT_HEADS
    W = ck_ref.shape[0] // H
    for h in range(H):
        sl = slice(h * dh, (h + 1) * dh)
        qb = q_ref[:, sl].astype(BF16)
        kn = kn_ref[:, sl].astype(BF16)
        vn = vn_ref[:, sl].astype(BF16)
        kp = ck_ref[pl.ds(h, W, stride=H), :].astype(BF16)
        vp = cv_ref[pl.ds(h, W, stride=H), :].astype(BF16)
        sp = lax.dot_general(qb, kp, (((1,), (1,)), ((), ())), preferred_element_type=F32)
        sn = lax.dot_general(qb, kn, (((1,), (1,)), ((), ())), preferred_element_type=F32)
        sp = sp * dh ** -0.5 + bp_ref[h]
        sn = sn * dh ** -0.5 + bn_ref[h]
        m = jnp.maximum(jnp.max(sp, axis=-1, keepdims=True), jnp.max(sn, axis=-1, keepdims=True))
        ep = jnp.exp(sp - m)
        en = jnp.exp(sn - m)
        den = jnp.sum(ep, axis=-1, keepdims=True) + jnp.sum(en, axis=-1, keepdims=True)
        o = jnp.dot((ep / den).astype(BF16), vp, preferred_element_type=F32)
        o = o + jnp.dot((en / den).astype(BF16), vn, preferred_element_type=F32)
        o_ref[:, sl] = o.astype(o_ref.dtype)


def attention_sample(qkv, cache_k, cache_v, bias_past, bias_new, *, batch, seq, row0, layer):
    H, dh = ATT_HEADS, ATT_DH
    W = cache_k.shape[2]
    rb0 = row0 // seq
    cache_k = cache_k.reshape(-1, dh)
    cache_v = cache_v.reshape(-1, dh)
    return pl.pallas_call(
        _att_sample_kernel,
        grid=(batch,),
        in_specs=[pl.BlockSpec((seq, H * dh), lambda b: (rb0 + b, 0)),
                  pl.BlockSpec((seq, H * dh), lambda b: (rb0 + b, 1)),
                  pl.BlockSpec((seq, H * dh), lambda b: (rb0 + b, 2)),
                  pl.BlockSpec((W * H, dh), lambda b: (layer * batch + b, 0)),
                  pl.BlockSpec((W * H, dh), lambda b: (layer * batch + b, 0)),
                  pl.BlockSpec((H, seq, W), lambda b: (0, 0, 0)),
                  pl.BlockSpec((H, seq, seq), lambda b: (0, 0, 0))],
        out_specs=pl.BlockSpec((seq, H * dh), lambda b: (b, 0)),
        out_shape=jax.ShapeDtypeStruct((batch * seq, H * dh), BF16),
        compiler_params=_params("arbitrary"),
        name="attention_sample",
    )(qkv, qkv, qkv, cache_k, cache_v, bias_past, bias_new)


def _rel_bias(table, n_q, n_k, offset):
    period = n_q + n_k
    d = jnp.arange(period)
    d = jnp.where(d < n_k, d, d - period)
    u = jnp.take(table, jnp.clip(offset - d, -REL_CLIP, REL_CLIP) + REL_CLIP, axis=0).T.astype(F32)
    H = u.shape[0]
    skew = jnp.tile(u, (1, n_q))[:, :n_q * (period - 1)].reshape(H, n_q, period - 1)
    return skew[:, :, :n_k]


def _gmlp_kernel(uv_ref, lng_ref, ws_ref, bs_ref, y_ref, vn_ref):
    DV = y_ref.shape[1]
    cg = DV // GMLP_GROUPS
    v = jax.nn.gelu(uv_ref[:, DV:])
    mu = jnp.mean(v, axis=-1, keepdims=True)
    d = v - mu
    var = jnp.mean(d * d, axis=-1, keepdims=True)
    vn = d * lax.rsqrt(var + EPS) * lng_ref[...]
    vn_ref[...] = vn
    for g in range(GMLP_GROUPS):
        sl = slice(g * cg, (g + 1) * cg)
        mixed = jnp.dot(ws_ref[g], vn[:, sl].astype(BF16), preferred_element_type=F32) + bs_ref[g]
        y_ref[:, sl] = (jax.nn.gelu(uv_ref[:, sl]) * mixed).astype(y_ref.dtype)


def gmlp_gate(uv, ln_g, ws, bs, *, n_prompt_tiles, tile):
    T = uv.shape[0]
    DV = uv.shape[1] // 2
    G = GMLP_GROUPS
    nt = T // tile
    last = n_prompt_tiles - 1
    return pl.pallas_call(
        _gmlp_kernel,
        grid=(nt,),
        in_specs=[pl.BlockSpec((tile, 2 * DV), lambda i: (i, 0)),
                  pl.BlockSpec((1, DV), lambda i: (0, 0)),
                  pl.BlockSpec((None, G, tile, tile), lambda i: (i // n_prompt_tiles, 0, 0, 0)),
                  pl.BlockSpec((None, G, tile, 1), lambda i: (i // n_prompt_tiles, 0, 0, 0))],
        out_specs=[pl.BlockSpec((tile, DV), lambda i: (i, 0)),
                   pl.BlockSpec((tile, DV), lambda i: (jnp.maximum(i - last, 0), 0))],
        out_shape=[jax.ShapeDtypeStruct((T, DV), BF16),
                   jax.ShapeDtypeStruct(((nt - last) * tile, DV), F32)],
        compiler_params=_params("arbitrary"),
        name="gmlp_gate",
    )(uv, ln_g.reshape(1, DV), ws, bs)


def _router_kernel(h_ref, w_ref, idx_ref, gate_ref):
    logits = jnp.dot(h_ref[...], w_ref[...].astype(BF16), preferred_element_type=F32)
    lane = lax.broadcasted_iota(jnp.int32, logits.shape, 1)
    lane_f = lane.astype(F32)
    l1 = jnp.where(lane < N_EXPERTS, logits, -jnp.inf)
    v1 = jnp.max(l1, axis=-1, keepdims=True)
    i1 = jnp.min(jnp.where(l1 == v1, lane_f, float(LANES)), axis=-1, keepdims=True)
    l2 = jnp.where(lane_f == i1, -jnp.inf, l1)
    v2 = jnp.max(l2, axis=-1, keepdims=True)
    i2 = jnp.min(jnp.where(l2 == v2, lane_f, float(LANES)), axis=-1, keepdims=True)
    e2 = jnp.exp(v2 - v1)
    den = 1.0 + e2
    idx_ref[...] = jnp.where(lane == 0, i1, jnp.where(lane == 1, i2, 0.0)).astype(jnp.int32)
    gate_ref[...] = jnp.where(lane == 0, 1.0 / den, jnp.where(lane == 1, e2 / den, 0.0))


def router(h, w_router_padded, *, tm):
    T, D = h.shape
    return pl.pallas_call(
        _router_kernel,
        grid=(T // tm,),
        in_specs=[pl.BlockSpec((tm, D), lambda i: (i, 0)), pl.BlockSpec((D, LANES), lambda i: (0, 0))],
        out_specs=[pl.BlockSpec((tm, LANES), lambda i: (i, 0)), pl.BlockSpec((tm, LANES), lambda i: (i, 0))],
        out_shape=[jax.ShapeDtypeStruct((T, LANES), jnp.int32), jax.ShapeDtypeStruct((T, LANES), F32)],
        compiler_params=_params("arbitrary"),
        name="router",
    )(h, w_router_padded)


def _gather_kernel(lo_ref, hi_ref, p0_ref, p1_ref, h_ref, o_ref, acc):
    m = pl.program_id(0)
    tm = o_ref.shape[0]
    sb = GATHER_SRC_BLOCK
    row = m * tm + lax.broadcasted_iota(jnp.int32, (tm, sb), 0)
    acc[...] = jnp.zeros_like(acc)

    def body(b, carry):
        select = jnp.where(p0_ref[b] == row, 1.0, jnp.where(p1_ref[b] == row, 1.0, 0.0)).astype(h_ref.dtype)
        acc[...] += jnp.dot(select, h_ref[pl.ds(pl.multiple_of(b * sb, sb), sb), :], preferred_element_type=F32)
        return carry

    lax.fori_loop(lo_ref[m], hi_ref[m] + 1, body, 0)
    o_ref[...] = acc[...].astype(o_ref.dtype)


def gather_rows(h, pos0, pos1, blk_lo, blk_hi, n_rows, *, tm):
    T, D = h.shape
    nsb = T // GATHER_SRC_BLOCK
    grid_spec = pltpu.PrefetchScalarGridSpec(
        num_scalar_prefetch=2,
        grid=(n_rows // tm,),
        in_specs=[pl.BlockSpec((nsb, 1, GATHER_SRC_BLOCK), lambda i, lo, hi: (0, 0, 0)),
                  pl.BlockSpec((nsb, 1, GATHER_SRC_BLOCK), lambda i, lo, hi: (0, 0, 0)),
                  pl.BlockSpec((T, D), lambda i, lo, hi: (0, 0), pipeline_mode=pl.Buffered(1))],
        out_specs=pl.BlockSpec((tm, D), lambda i, lo, hi: (i, 0)),
        scratch_shapes=[pltpu.VMEM((tm, D), F32)],
    )
    return pl.pallas_call(
        _gather_kernel,
        grid_spec=grid_spec,
        out_shape=jax.ShapeDtypeStruct((n_rows, D), h.dtype),
        compiler_params=_params("arbitrary"),
        name="moe_gather",
    )(blk_lo, blk_hi, pos0.reshape(nsb, 1, GATHER_SRC_BLOCK), pos1.reshape(nsb, 1, GATHER_SRC_BLOCK), h)


def _on_valid_rows(valid, o_ref, compute):
    tm = o_ref.shape[0]
    half = tm // 2

    @pl.when(valid > half)
    def _():
        compute(slice(0, tm))

    @pl.when((valid > 0) & (valid <= half))
    def _():
        compute(slice(0, half))
        o_ref[half:, :] = jnp.zeros((tm - half, o_ref.shape[1]), o_ref.dtype)

    @pl.when(valid == 0)
    def _():
        o_ref[...] = jnp.zeros_like(o_ref)


def _group_weights(te_ref, first_ref, nxt_ref, wrap_ref, copies, cast):
    n = pl.program_id(0)
    m = pl.program_id(1)

    @pl.when(first_ref[m] == 1)
    def _():
        @pl.when((n == 0) & (m == 0))
        def _():
            for cp in copies(te_ref[0], 0):
                cp.start()

        for cp in copies(te_ref[m], n):
            cp.wait()
        cast()
        n_next = n + wrap_ref[m]

        @pl.when(n_next < pl.num_programs(0))
        def _():
            for cp in copies(nxt_ref[m], n_next):
                cp.start()


def _moe_up_kernel(te_ref, first_ref, valid_ref, nxt_ref, wrap_ref, x_ref, w_hbm, o_ref,
                   stage_a, stage_b, wa_bf, wb_bf, sems):
    m = pl.program_id(1)
    tn = wa_bf.shape[1]
    F = w_hbm.shape[2] // 2

    def copies(e, n):
        col = pl.multiple_of(n * tn, tn)
        return (pltpu.make_async_copy(w_hbm.at[e, :, pl.ds(col, tn)], stage_a, sems.at[0]),
                pltpu.make_async_copy(w_hbm.at[e, :, pl.ds(pl.multiple_of(F + col, tn), tn)], stage_b, sems.at[1]))

    def cast():
        wa_bf[...] = stage_a[...].astype(BF16)
        wb_bf[...] = stage_b[...].astype(BF16)

    _group_weights(te_ref, first_ref, nxt_ref, wrap_ref, copies, cast)

    def compute(rows):
        x = x_ref[rows, :]
        a = jnp.dot(x, wa_bf[...], preferred_element_type=F32)
        b = jnp.dot(x, wb_bf[...], preferred_element_type=F32)
        o_ref[rows, :] = (_silu(a) * b).astype(o_ref.dtype)

    _on_valid_rows(valid_ref[m], o_ref, compute)


def moe_up(xs, w_gu, plan, *, tm, tn):
    P, D = xs.shape
    F = w_gu.shape[2] // 2
    grid_spec = pltpu.PrefetchScalarGridSpec(
        num_scalar_prefetch=len(plan),
        grid=(F // tn, P // tm),
        in_specs=[pl.BlockSpec((tm, D), lambda n, m, *_: (m, 0)),
                  pl.BlockSpec(memory_space=pl.ANY)],
        out_specs=pl.BlockSpec((tm, tn), lambda n, m, *_: (m, n)),
        scratch_shapes=[pltpu.VMEM((D, tn), w_gu.dtype), pltpu.VMEM((D, tn), w_gu.dtype),
                        pltpu.VMEM((D, tn), BF16), pltpu.VMEM((D, tn), BF16), pltpu.SemaphoreType.DMA((2,))],
    )
    return pl.pallas_call(
        _moe_up_kernel,
        grid_spec=grid_spec,
        out_shape=jax.ShapeDtypeStruct((P, F), BF16),
        compiler_params=_params("arbitrary", "arbitrary"),
        name="moe_up",
    )(*plan, xs, w_gu)


def _moe_down_kernel(te_ref, first_ref, valid_ref, nxt_ref, wrap_ref, x_ref, w_hbm, o_ref, stage, w_bf, sem):
    m = pl.program_id(1)
    tn = w_bf.shape[1]

    def copies(e, n):
        return (pltpu.make_async_copy(w_hbm.at[e, :, pl.ds(pl.multiple_of(n * tn, tn), tn)], stage, sem),)

    def cast():
        w_bf[...] = stage[...].astype(BF16)

    _group_weights(te_ref, first_ref, nxt_ref, wrap_ref, copies, cast)

    def compute(rows):
        o_ref[rows, :] = jnp.dot(x_ref[rows, :], w_bf[...], preferred_element_type=F32)

    _on_valid_rows(valid_ref[m], o_ref, compute)


def moe_down(hm, w_down, plan, *, tm, tn):
    P, F = hm.shape
    D = w_down.shape[2]
    grid_spec = pltpu.PrefetchScalarGridSpec(
        num_scalar_prefetch=len(plan),
        grid=(D // tn, P // tm),
        in_specs=[pl.BlockSpec((tm, F), lambda n, m, *_: (m, 0)),
                  pl.BlockSpec(memory_space=pl.ANY)],
        out_specs=pl.BlockSpec((tm, tn), lambda n, m, *_: (m, n)),
        scratch_shapes=[pltpu.VMEM((F, tn), w_down.dtype), pltpu.VMEM((F, tn), BF16), pltpu.SemaphoreType.DMA(())],
    )
    return pl.pallas_call(
        _moe_down_kernel,
        grid_spec=grid_spec,
        out_shape=jax.ShapeDtypeStruct((P, D), F32),
        compiler_params=_params("arbitrary", "arbitrary"),
        name="moe_down",
    )(*plan, hm, w_down)


def _row_copy(y_hbm, buf, src_row, dst_row, sem):
    return pltpu.make_async_copy(y_hbm.at[pl.ds(src_row, 1), :], buf.at[pl.ds(dst_row, 1), :], sem)


def _combine_norm_kernel(p0_ref, p1_ref, x_ref, gate_ref, g_ref, y_hbm, xo_ref, hn_ref, buf0, buf1, sems):
    tm = x_ref.shape[0]
    base = pl.program_id(0) * tm

    def start(r, carry):
        _row_copy(y_hbm, buf0, p0_ref[base + r], r, sems.at[0]).start(priority=0)
        _row_copy(y_hbm, buf1, p1_ref[base + r], r, sems.at[1]).start(priority=DMA_PRIORITIES - 1)
        return carry

    def wait(r, carry):
        _row_copy(y_hbm, buf0, 0, r, sems.at[0]).wait()
        _row_copy(y_hbm, buf1, 0, r, sems.at[1]).wait()
        return carry

    lax.fori_loop(0, tm, start, 0)
    lax.fori_loop(0, tm, wait, 0)
    gate = gate_ref[...]
    xn = x_ref[...] + (gate[:, 0:1] * buf0[...] + gate[:, 1:2] * buf1[...])
    xo_ref[...] = xn
    hn = xn * lax.rsqrt(jnp.mean(xn * xn, axis=-1, keepdims=True) + EPS) * g_ref[...]
    hn_ref[...] = hn.astype(hn_ref.dtype)


def moe_combine_norm(x, gates, ys, pos0, pos1, g, *, tm):
    T, D = x.shape
    grid_spec = pltpu.PrefetchScalarGridSpec(
        num_scalar_prefetch=2,
        grid=(T // tm,),
        in_specs=[pl.BlockSpec((tm, D), lambda i, a, b: (i, 0)),
                  pl.BlockSpec((tm, LANES), lambda i, a, b: (i, 0)),
                  pl.BlockSpec((1, D), lambda i, a, b: (0, 0)),
                  pl.BlockSpec(memory_space=pl.ANY)],
        out_specs=[pl.BlockSpec((tm, D), lambda i, a, b: (i, 0)), pl.BlockSpec((tm, D), lambda i, a, b: (i, 0))],
        scratch_shapes=[pltpu.VMEM((tm, D), F32), pltpu.VMEM((tm, D), F32), pltpu.SemaphoreType.DMA((2,))],
    )
    return pl.pallas_call(
        _combine_norm_kernel,
        grid_spec=grid_spec,
        out_shape=[jax.ShapeDtypeStruct((T, D), F32), jax.ShapeDtypeStruct((T, D), BF16)],
        compiler_params=_params("arbitrary"),
        name="moe_combine_norm",
    )(pos0, pos1, x, gates, g.reshape(1, D), ys)


def _route_plan(idx, tm):
    T = idx.shape[0]
    E = N_EXPERTS
    n_tiles = (TOP_K * T) // tm + E
    experts = jnp.arange(E, dtype=jnp.int32)
    flat_e = idx.reshape(-1)
    onehot = (flat_e[:, None] == experts[None, :]).astype(jnp.int32)
    csum = jnp.cumsum(onehot, axis=0)
    rank = jnp.sum((csum - onehot) * onehot, axis=1)
    counts = csum[-1]
    tiles_e = (counts + tm - 1) // tm
    tile_end = jnp.cumsum(tiles_e)
    tile_start = tile_end - tiles_e
    pos = (jnp.sum(onehot * tile_start[None, :], axis=1) * tm + rank).astype(jnp.int32)
    n_used = tile_end[-1]
    m_ids = jnp.arange(n_tiles, dtype=jnp.int32)
    te = jnp.sum((jnp.minimum(m_ids, n_used - 1)[:, None] >= tile_end[None, :]).astype(jnp.int32), axis=1)
    first = ((m_ids == 0) | (te != jnp.roll(te, 1))) & (m_ids < n_used)
    te_onehot = (te[:, None] == experts[None, :]).astype(jnp.int32)
    rows_before = (m_ids - jnp.sum(te_onehot * tile_start[None, :], axis=1)) * tm
    valid = jnp.clip(jnp.sum(te_onehot * counts[None, :], axis=1) - rows_before, 0, tm)
    valid = jnp.where(m_ids < n_used, valid, 0).astype(jnp.int32)
    nonempty = jnp.where(tiles_e > 0, experts, E)
    after = jnp.min(jnp.where(experts[None, :] > experts[:, None], nonempty[None, :], E), axis=1)
    nxt_e = jnp.where(after < E, after, jnp.min(nonempty))
    nxt = jnp.sum(te_onehot * nxt_e[None, :], axis=1).astype(jnp.int32)
    wrap = jnp.sum(te_onehot * (after == E).astype(jnp.int32)[None, :], axis=1).astype(jnp.int32)
    csum_tile = jnp.sum(csum[:, :, None] * te_onehot.T[None, :, :], axis=1)
    pair_first = jnp.sum((csum_tile <= rows_before[None, :]).astype(jnp.int32), axis=0)
    pair_last = jnp.sum((csum_tile <= (rows_before + valid - 1)[None, :]).astype(jnp.int32), axis=0)
    blk_lo = jnp.where(valid > 0, pair_first // (TOP_K * GATHER_SRC_BLOCK), 0).astype(jnp.int32)
    blk_hi = jnp.where(valid > 0, pair_last // (TOP_K * GATHER_SRC_BLOCK), -1).astype(jnp.int32)
    return (pos[0::TOP_K], pos[1::TOP_K], blk_lo, blk_hi,
            (te.astype(jnp.int32), first.astype(jnp.int32), valid, nxt, wrap))


def moe(x, h, w_router, w_gu, w_down, g_next, *, layer, tm_tok):
    T = x.shape[0]
    E = N_EXPERTS
    w_r = jnp.pad(w_router[layer], ((0, 0), (0, LANES - E)))
    idx, gates = router(h, w_r, tm=tm_tok)
    pos0, pos1, blk_lo, blk_hi, (te, first, valid, nxt, wrap) = _route_plan(idx[:, :TOP_K], MOE_TM)
    plan = (te + layer * E, first, valid, nxt + layer * E, wrap)
    n_rows = te.shape[0] * MOE_TM
    xs = gather_rows(h, pos0, pos1, blk_lo, blk_hi, n_rows, tm=MOE_TM)
    hm = moe_up(xs, w_gu.reshape((-1,) + w_gu.shape[2:]), plan, tm=MOE_TM, tn=1024)
    ys = moe_down(hm, w_down.reshape((-1,) + w_down.shape[2:]), plan, tm=MOE_TM, tn=512)
    return moe_combine_norm(x, gates, ys, pos0, pos1, g_next, tm=tm_tok)


def kernel(x_prompt, x_sample, state_ret, cache_k, cache_v, p_prompt, p_sample, g_mix, g_ffn, g_ple, g_final,
           ret_w_in, ret_w_out, att_w_in, att_w_out, att_rel_bias, gmlp_w_in, gmlp_ln_g, gmlp_w_s, gmlp_b_s,
           gmlp_w_out, ffn_w_gu, ffn_w_down, moe_w_router, moe_w_gu, moe_w_down, ple_w_in, ple_w_gate):
    B, S, D = x_prompt.shape
    BS, L, _ = x_sample.shape
    depth = g_mix.shape[0]
    NP, NS = B * S, BS * L
    T = NP + NS
    assert B == 1 and S % RET_CHUNK == 0 and S % ATT_QBLOCK == 0 and L <= CHUNK and NP % NS == 0
    TM = T // 8
    TMS = T // 16

    x = jnp.concatenate([x_prompt.reshape(NP, D), x_sample.reshape(NS, D)], axis=0)
    p_all = jnp.concatenate([p_prompt.reshape(depth, NP, -1), p_sample.reshape(depth, NS, -1)], axis=1)

    pos = jnp.concatenate([jnp.arange(S, dtype=jnp.int32), jnp.tile(PAST_LEN + jnp.arange(L, dtype=jnp.int32), BS)])
    freqs = ROPE_BASE ** (-jnp.arange(0, RET_DK, 2, dtype=F32) / RET_DK)
    ang = pos.astype(F32)[:, None] * freqs[None, :]
    cos, sin = jnp.cos(ang), jnp.sin(ang)
    log_g = jnp.log1p(-jnp.exp2(-5.0 - jnp.arange(RET_HEADS, dtype=F32)))

    ret_p, ret_s = None, None
    k_p, v_p, k_s, v_s, gv = [], [], [], [], []
    (h,) = rmsnorm(x, g_mix[0], [BF16], TMS)
    for i in range(depth):
        kind, slot = i % 3, i // 3
        if kind == 0:
            qkvg = matmul(h, ret_w_in, layer=slot, tm=TM, tn=1024)
            y_p, ret_p = retention(qkvg, cos, sin, log_g, None, ret_p, batch=B, seq=S, chunk=RET_CHUNK, row0=0,
                                   heads_per_step=RET_HEADS)
            y_s, ret_s = retention(qkvg, cos, sin, log_g, state_ret, ret_s, batch=BS, seq=L, chunk=L, row0=NP,
                                   heads_per_step=RET_HEADS, state_layer=slot)
            x = matmul(y_p, ret_w_out, layer=slot, tm=NS, tn=512, res=x, a_tail=y_s)
        elif kind == 1:
            HD = ATT_HEADS * ATT_DH
            W = cache_k.shape[2]
            past = BAND_PAST_CHUNKS * CHUNK
            qkv = matmul(h, att_w_in, layer=slot, tm=TM, tn=1024)
            bias_p = _rel_bias(att_rel_bias[slot], ATT_QBLOCK, past + ATT_QBLOCK, past)
            bias_s = _rel_bias(att_rel_bias[slot], L, W + L, W)
            o_p = attention_prompt(qkv, bias_p, seq=S)
            o_s = attention_sample(qkv, cache_k, cache_v, bias_s[:, :, :W], bias_s[:, :, W:],
                                   batch=BS, seq=L, row0=NP, layer=slot)
            w = min(past, S)
            k_p.append(qkv[NP - w:NP, HD:2 * HD].reshape(B, w, ATT_HEADS, ATT_DH))
            v_p.append(qkv[NP - w:NP, 2 * HD:].reshape(B, w, ATT_HEADS, ATT_DH))
            k_s.append(qkv[NP:, HD:2 * HD].reshape(BS, L, ATT_HEADS, ATT_DH))
            v_s.append(qkv[NP:, 2 * HD:].reshape(BS, L, ATT_HEADS, ATT_DH))
            x = matmul(o_p, att_w_out, layer=slot, tm=NS, tn=1024, res=x, a_tail=o_s)
        else:
            tile = GMLP_CHUNK
            uv = matmul(h, gmlp_w_in, layer=slot, tm=TM, tn=1024)
            r = jnp.arange(tile)
            ws_p = jnp.where((r[:, None] >= r[None, :])[None], gmlp_w_s[slot][:, :tile, :tile], 0.0)
            same = (r[:, None] // L == r[None, :] // L) & (r[:, None] >= r[None, :])
            ws_s = jnp.where(same[None], jnp.tile(gmlp_w_s[slot][:, :L, :L], (1, tile // L, tile // L)), 0.0)
            bs_p = gmlp_b_s[slot][:, :tile]
            bs_s = jnp.tile(gmlp_b_s[slot][:, :L], (1, tile // L))
            ws = jnp.stack([ws_p, ws_s]).astype(BF16)
            bs = jnp.stack([bs_p, bs_s])[..., None]
            yg, vn = gmlp_gate(uv, gmlp_ln_g[slot], ws, bs, n_prompt_tiles=NP // tile, tile=tile)
            gv.append((vn[:tile].reshape(B, tile, -1), vn[tile:].reshape(BS, L, -1)))
            x = matmul(yg, gmlp_w_out, layer=slot, tm=TMS, tn=512, res=x)

        if i % 2 == 0:
            (h,) = rmsnorm(x, g_ffn[i], [BF16], TMS)
            x = dense_ffn(x, h, ffn_w_gu, ffn_w_down, layer=i // 2, tm_up=TM, tm_down=TMS)
            (hn,) = rmsnorm(x, g_ple[i], [BF16], TMS)
        else:
            (h,) = rmsnorm(x, g_ffn[i], [BF16], TMS)
            x, hn = moe(x, h, moe_w_router, moe_w_gu, moe_w_down, g_ple[i], layer=i // 2, tm_tok=TMS)

        g_next, next_dtype = (g_mix[i + 1], BF16) if i + 1 < depth else (g_final, F32)
        x, h = ple_norm(x, hn, p_all, ple_w_gate, ple_w_in, g_next, next_dtype, layer=i, tm=T // 32)

    y = h
    y_prompt = y[:NP].reshape(B, S, D)
    y_sample = y[NP:].reshape(BS, L, D)
    return (y_prompt, y_sample, ret_p, ret_s, jnp.stack(k_p), jnp.stack(v_p),
            jnp.stack(k_s), jnp.stack(v_s), jnp.stack([a for a, _ in gv]), jnp.stack([b for _, b in gv]))
```

```python
import functools

import jax
import jax.numpy as jnp
from jax import lax
from jax.experimental import pallas as pl
from jax.experimental.pallas import tpu as pltpu

F32 = jnp.float32
BF16 = jnp.bfloat16

EPS = 1e-6
NEG_INF = -1e30
CHUNK = 64
ROPE_BASE = 10000.0
RET_HEADS, RET_DK, RET_DV = 8, 256, 512
ATT_HEADS, ATT_DH = 16, 128
BAND_PAST_CHUNKS = 8
REL_CLIP = 128
GMLP_CHUNK, GMLP_GROUPS = 128, 8
N_EXPERTS, TOP_K = 8, 2
PAST_LEN = 2048

LANES = 128
DMA_PRIORITIES = 2
V7X_VMEM_LIMIT_BYTES = 60 * 1000 * 1024

RET_CHUNK = 256
ATT_QBLOCK = 4 * CHUNK
ATT_UNROLL = 4
MOE_TM = 512
GATHER_SRC_BLOCK = 512


def _params(*sem):
    return pltpu.CompilerParams(dimension_semantics=sem, vmem_limit_bytes=V7X_VMEM_LIMIT_BYTES)


def _silu(x):
    return x * jax.nn.sigmoid(x)


def _rmsnorm_kernel(x_ref, g_ref, *o_refs):
    x = x_ref[...]
    y = x * lax.rsqrt(jnp.mean(x * x, axis=-1, keepdims=True) + EPS) * g_ref[...]
    for o_ref in o_refs:
        o_ref[...] = y.astype(o_ref.dtype)


def rmsnorm(x, g, dtypes, tm):
    T, D = x.shape
    out_specs = [pl.BlockSpec((tm, D), lambda i: (i, 0)) for _ in dtypes]
    out_shape = [jax.ShapeDtypeStruct((T, D), dt) for dt in dtypes]
    return pl.pallas_call(
        _rmsnorm_kernel,
        grid=(T // tm,),
        in_specs=[pl.BlockSpec((tm, D), lambda i: (i, 0)), pl.BlockSpec((1, D), lambda i: (0, 0))],
        out_specs=out_specs,
        out_shape=out_shape,
        compiler_params=_params("arbitrary"),
        name="rmsnorm",
    )(x, g.reshape(1, D))


def _mm_kernel(a_ref, *rest, n_head, has_res, cast_w):
    rest = list(rest)
    t_ref = rest.pop(0) if n_head is not None else None
    w_ref = rest.pop(0)
    r_ref = rest.pop(0) if has_res else None
    o_ref, *scratch = rest
    if cast_w:
        w_bf = scratch[0]

        @pl.when(pl.program_id(1) == 0)
        def _():
            w_bf[...] = w_ref[...].astype(BF16)

        w = w_bf[...]
    else:
        w = w_ref[...]
    a = a_ref[...]
    if n_head is not None:
        a = jnp.where(pl.program_id(1) < n_head, a, t_ref[...])
    acc = jnp.dot(a, w, preferred_element_type=F32)
    if has_res:
        acc = r_ref[...] + acc
    o_ref[...] = acc.astype(o_ref.dtype)


def _weight_spec(w, layer, tn):
    K = w.shape[-2]
    if w.ndim == 2:
        return pl.BlockSpec((K, tn), lambda n, m: (0, n))
    return pl.BlockSpec((None, K, tn), lambda n, m: (layer, 0, n))


def matmul(a, w, *, tm, tn, layer=0, out_dtype=F32, res=None, a_tail=None):
    K = a.shape[1]
    N = w.shape[-1]
    cast_w = w.dtype != BF16
    n_head = None
    M = a.shape[0]
    in_specs = [pl.BlockSpec((tm, K), lambda n, m: (m, 0))]
    args = [a]
    if a_tail is not None:
        assert a.shape[0] % tm == 0 and a_tail.shape[0] % tm == 0
        n_head = a.shape[0] // tm
        M += a_tail.shape[0]
        in_specs = [pl.BlockSpec((tm, K), lambda n, m: (jnp.minimum(m, n_head - 1), 0)),
                    pl.BlockSpec((tm, K), lambda n, m: (jnp.maximum(m - n_head, 0), 0))]
        args.append(a_tail)
    in_specs.append(_weight_spec(w, layer, tn))
    args.append(w)
    if res is not None:
        in_specs.append(pl.BlockSpec((tm, tn), lambda n, m: (m, n)))
        args.append(res)
    return pl.pallas_call(
        functools.partial(_mm_kernel, n_head=n_head, has_res=res is not None, cast_w=cast_w),
        grid=(N // tn, M // tm),
        in_specs=in_specs,
        out_specs=pl.BlockSpec((tm, tn), lambda n, m: (m, n)),
        out_shape=jax.ShapeDtypeStruct((M, N), out_dtype),
        scratch_shapes=[pltpu.VMEM((K, tn), BF16)] if cast_w else [],
        compiler_params=_params("arbitrary", "arbitrary"),
        name="matmul",
    )(*args)


def _swiglu_up_kernel(x_ref, w_hbm, o_ref, stage_a, stage_b, wa_bf, wb_bf, sems, *, layer, tail):
    n = pl.program_id(0)
    nb = pl.num_programs(0)
    tn = wa_bf.shape[1]
    F = w_hbm.shape[2] // 2

    def copies(nn, width):
        col = nn * tn
        a = pltpu.make_async_copy(w_hbm.at[layer, :, pl.ds(pl.multiple_of(col, LANES), width)],
                                  stage_a.at[:, pl.ds(0, width)], sems.at[0])
        b = pltpu.make_async_copy(w_hbm.at[layer, :, pl.ds(pl.multiple_of(F + col, LANES), width)],
                                  stage_b.at[:, pl.ds(0, width)], sems.at[1])
        return a, b

    def for_tile(nn, action):
        if tail == tn:
            for cp in copies(nn, tn):
                action(cp)
        else:
            @pl.when(nn < nb - 1)
            def _():
                for cp in copies(nn, tn):
                    action(cp)

            @pl.when(nn == nb - 1)
            def _():
                for cp in copies(nn, tail):
                    action(cp)

    @pl.when(pl.program_id(1) == 0)
    def _():
        @pl.when(n == 0)
        def _():
            for_tile(n, lambda cp: cp.start())

        for_tile(n, lambda cp: cp.wait())
        wa_bf[...] = stage_a[...].astype(BF16)
        wb_bf[...] = stage_b[...].astype(BF16)

        @pl.when(n + 1 < nb)
        def _():
            for_tile(n + 1, lambda cp: cp.start())

    x = x_ref[...]
    a = jnp.dot(x, wa_bf[...], preferred_element_type=F32)
    b = jnp.dot(x, wb_bf[...], preferred_element_type=F32)
    o_ref[...] = (_silu(a) * b).astype(o_ref.dtype)


def swiglu_up(x, w_gu, *, layer, tm, tn):
    T, D = x.shape
    F = w_gu.shape[2] // 2
    nb = pl.cdiv(F, tn)
    tail = F - (nb - 1) * tn
    assert F % LANES == 0 and (nb > 1 or tail == tn)
    return pl.pallas_call(
        functools.partial(_swiglu_up_kernel, layer=layer, tail=tail),
        grid=(nb, T // tm),
        in_specs=[pl.BlockSpec((tm, D), lambda n, m: (m, 0)), pl.BlockSpec(memory_space=pl.ANY)],
        out_specs=pl.BlockSpec((tm, tn), lambda n, m: (m, n)),
        out_shape=jax.ShapeDtypeStruct((T, F), BF16),
        scratch_shapes=[pltpu.VMEM((D, tn), w_gu.dtype), pltpu.VMEM((D, tn), w_gu.dtype),
                        pltpu.VMEM((D, tn), BF16), pltpu.VMEM((D, tn), BF16), pltpu.SemaphoreType.DMA((2,))],
        compiler_params=_params("arbitrary", "arbitrary"),
        name="swiglu_up",
    )(x, w_gu)


def dense_ffn(x, h, w_gu, w_down, *, layer, tm_up, tm_down):
    hm = swiglu_up(h, w_gu, layer=layer, tm=tm_up, tn=512)
    return matmul(hm, w_down, layer=layer, tm=tm_down, tn=512, res=x)


def _ple_norm_kernel(hn_ref, wg_ref, p_ref, wi_ref, x_ref, gn_ref, o_ref, hnext_ref, wg_bf, wi_bf):
    @pl.when(pl.program_id(0) == 0)
    def _():
        wg_bf[...] = wg_ref[...].astype(BF16)
        wi_bf[...] = wi_ref[...].astype(BF16)

    gate = jax.nn.sigmoid(jnp.dot(hn_ref[...], wg_bf[...], preferred_element_type=F32))
    emb = jnp.dot(p_ref[...].astype(BF16), wi_bf[...], preferred_element_type=F32)
    xn = x_ref[...] + gate * emb
    o_ref[...] = xn
    hnext = xn * lax.rsqrt(jnp.mean(xn * xn, axis=-1, keepdims=True) + EPS) * gn_ref[...]
    hnext_ref[...] = hnext.astype(hnext_ref.dtype)


def ple_norm(x, hn, p, w_gate, w_in, g_next, next_dtype, *, layer, tm):
    T, D = x.shape
    DP = p.shape[2]
    resident = pl.Buffered(1)
    return pl.pallas_call(
        _ple_norm_kernel,
        grid=(T // tm,),
        in_specs=[pl.BlockSpec((tm, D), lambda m: (m, 0)),
                  pl.BlockSpec((None, D, D), lambda m: (layer, 0, 0), pipeline_mode=resident),
                  pl.BlockSpec((None, tm, DP), lambda m: (layer, m, 0)),
                  pl.BlockSpec((None, DP, D), lambda m: (layer, 0, 0), pipeline_mode=resident),
                  pl.BlockSpec((tm, D), lambda m: (m, 0)),
                  pl.BlockSpec((1, D), lambda m: (0, 0))],
        out_specs=[pl.BlockSpec((tm, D), lambda m: (m, 0)), pl.BlockSpec((tm, D), lambda m: (m, 0))],
        out_shape=[jax.ShapeDtypeStruct((T, D), F32), jax.ShapeDtypeStruct((T, D), next_dtype)],
        scratch_shapes=[pltpu.VMEM((D, D), BF16), pltpu.VMEM((DP, D), BF16)],
        compiler_params=_params("arbitrary"),
        name="ple_norm",
    )(hn, w_gate, p, w_in, x, g_next.reshape(1, D))


def _retention_kernel(lg_ref, q_ref, k_ref, v_ref, g_ref, cos_ref, sin_ref, *rest, has_state0, has_prev):
    rest = list(rest)
    s0_ref = rest.pop(0) if has_state0 else None
    prev_ref = rest.pop(0) if has_prev else None
    y_ref, sout_ref, state = rest
    n_prev = sout_ref.shape[0] - 1
    hb = pl.program_id(1)
    c = pl.program_id(2)
    C = q_ref.shape[0]
    hp = state.shape[0]
    dk, dv = RET_DK, RET_DV
    half = dk // 2

    @pl.when(c == 0)
    def _():
        if has_state0:
            state[...] = s0_ref[...]
        else:
            state[...] = jnp.zeros_like(state)

    cos = cos_ref[...]
    sin = sin_ref[...]

    def rotary(x):
        x1, x2 = x[:, :half], x[:, half:]
        return jnp.concatenate([x1 * cos - x2 * sin, x1 * sin + x2 * cos], axis=-1)

    row = lax.broadcasted_iota(jnp.int32, (C, C), 0)
    col = lax.broadcasted_iota(jnp.int32, (C, C), 1)
    rel = (row - col).astype(F32)
    rel_pos = jnp.maximum(rel, 0.0)
    idx = lax.broadcasted_iota(jnp.int32, (C, 1), 0).astype(F32)

    for j in range(hp):
        lg = lg_ref[hb * hp + j]
        q = rotary(q_ref[:, j * dk:(j + 1) * dk])
        k = rotary(k_ref[:, j * dk:(j + 1) * dk]) * dk ** -0.5
        vb = v_ref[:, j * dv:(j + 1) * dv].astype(BF16)
        decay = jnp.where(rel >= 0, jnp.exp(rel_pos * lg), 0.0)
        q_decay = jnp.exp((idx + 1.0) * lg)
        k_decay = jnp.exp((C - 1.0 - idx) * lg)
        chunk_decay = jnp.exp(jnp.zeros((1, 1), F32) + C * lg)

        qb = q.astype(BF16)
        scores = lax.dot_general(qb, k.astype(BF16), (((1,), (1,)), ((), ())), preferred_element_type=F32) * decay
        o = jnp.dot(scores.astype(BF16), vb, preferred_element_type=F32)
        st = state[j]
        o = o + jnp.dot(qb, st.astype(BF16), preferred_element_type=F32) * q_decay
        kv = lax.dot_general((k * k_decay).astype(BF16), vb, (((0,), (0,)), ((), ())), preferred_element_type=F32)
        st = st * chunk_decay + kv
        state[j] = st
        sout_ref[n_prev, j] = st

        o = o * lax.rsqrt(jnp.mean(o * o, axis=-1, keepdims=True) + EPS)
        y_ref[:, j * dv:(j + 1) * dv] = (o * _silu(g_ref[:, j * dv:(j + 1) * dv])).astype(y_ref.dtype)

    if has_prev:
        @pl.when(c == pl.num_programs(2) - 1)
        def _():
            sout_ref[:n_prev] = prev_ref[...]


def retention(qkvg, cos, sin, log_g, state0, prev_states, *, batch, seq, chunk, row0, heads_per_step, state_layer=0):
    H, dk, dv = RET_HEADS, RET_DK, RET_DV
    hp = heads_per_step
    nhb = H // hp
    nc = seq // chunk
    rb0 = row0 // chunk
    kq = nhb
    kv = (2 * H * dk) // (hp * dv)
    kg = kv + nhb

    def rows(b, c):
        return rb0 + b * nc + c

    in_specs = [
        pl.BlockSpec((chunk, hp * dk), lambda b, h, c, lg: (rows(b, c), h)),
        pl.BlockSpec((chunk, hp * dk), lambda b, h, c, lg: (rows(b, c), kq + h)),
        pl.BlockSpec((chunk, hp * dv), lambda b, h, c, lg: (rows(b, c), kv + h)),
        pl.BlockSpec((chunk, hp * dv), lambda b, h, c, lg: (rows(b, c), kg + h)),
        pl.BlockSpec((chunk, dk // 2), lambda b, h, c, lg: (rows(b, c), 0)),
        pl.BlockSpec((chunk, dk // 2), lambda b, h, c, lg: (rows(b, c), 0)),
    ]
    args = [qkvg, qkvg, qkvg, qkvg, cos, sin]
    if state0 is not None:
        in_specs.append(pl.BlockSpec((None, None, hp, dk, dv), lambda b, h, c, lg: (state_layer, b, h, 0, 0)))
        args.append(state0)
    n_prev = 0 if prev_states is None else prev_states.shape[0]
    if prev_states is not None:
        in_specs.append(pl.BlockSpec((n_prev, None, hp, dk, dv), lambda b, h, c, lg: (0, b, h, 0, 0)))
        args.append(prev_states)
    grid_spec = pltpu.PrefetchScalarGridSpec(
        num_scalar_prefetch=1,
        grid=(batch, nhb, nc),
        in_specs=in_specs,
        out_specs=[pl.BlockSpec((chunk, hp * dv), lambda b, h, c, lg: (b * nc + c, h)),
                   pl.BlockSpec((n_prev + 1, None, hp, dk, dv), lambda b, h, c, lg: (0, b, h, 0, 0))],
        scratch_shapes=[pltpu.VMEM((hp, dk, dv), F32)],
    )
    return pl.pallas_call(
        functools.partial(_retention_kernel, has_state0=state0 is not None, has_prev=prev_states is not None),
        grid_spec=grid_spec,
        out_shape=[jax.ShapeDtypeStruct((batch * seq, H * dv), BF16),
                   jax.ShapeDtypeStruct((n_prev + 1, batch, H, dk, dv), F32)],
        compiler_params=_params("arbitrary", "arbitrary", "arbitrary"),
        name="retention",
    )(log_g, *args)


def _att_prompt_kernel(q_ref, k_ref, v_ref, bias_ref, o_ref, kpad, vpad):
    S = q_ref.shape[0]
    QB = ATT_QBLOCK
    past = BAND_PAST_CHUNKS * CHUNK
    win = past + QB
    kpad[:past, :] = jnp.zeros((past, ATT_DH), BF16)
    vpad[:past, :ATT_DH] = jnp.zeros((past, ATT_DH), BF16)
    kpad[past:, :] = k_ref[...].astype(BF16)
    vpad[past:, :ATT_DH] = v_ref[...].astype(BF16)
    vpad[:, ATT_DH:] = jnp.ones((past + S, ATT_DH), BF16)
    bias = bias_ref[...]
    qi = lax.broadcasted_iota(jnp.int32, (QB, win), 0)
    kj = lax.broadcasted_iota(jnp.int32, (QB, win), 1)
    chunk_shift = CHUNK.bit_length() - 1
    qc = jnp.right_shift(qi, chunk_shift)
    kc = jnp.right_shift(kj, chunk_shift)
    band = (kc >= qc) & (kc <= qc + BAND_PAST_CHUNKS)

    def body(i, carry):
        q0 = pl.multiple_of(i * QB, QB)
        qb = q_ref[pl.ds(q0, QB), :].astype(BF16)
        kw = kpad[pl.ds(q0, win), :]
        vw = vpad[pl.ds(q0, win), :]
        s = lax.dot_general(qb, kw, (((1,), (1,)), ((), ())), preferred_element_type=F32)
        s = s * ATT_DH ** -0.5 + bias
        valid = band & (kj >= past - q0)
        s = jnp.where(valid, s, NEG_INF)
        m = jnp.max(s, axis=-1, keepdims=True)
        e = jnp.exp(s - m).astype(BF16)
        ov = jnp.dot(e, vw, preferred_element_type=F32)
        o_ref[pl.ds(q0, QB), :] = (ov[:, :ATT_DH] / ov[:, ATT_DH:ATT_DH + 1]).astype(o_ref.dtype)
        return carry

    lax.fori_loop(0, S // QB, body, 0, unroll=ATT_UNROLL)


def attention_prompt(qkv, bias, *, seq):
    H, dh = ATT_HEADS, ATT_DH
    past = BAND_PAST_CHUNKS * CHUNK
    return pl.pallas_call(
        _att_prompt_kernel,
        grid=(H,),
        in_specs=[pl.BlockSpec((seq, dh), lambda h: (0, h)),
                  pl.BlockSpec((seq, dh), lambda h: (0, H + h)),
                  pl.BlockSpec((seq, dh), lambda h: (0, 2 * H + h)),
                  pl.BlockSpec((None, ATT_QBLOCK, past + ATT_QBLOCK), lambda h: (h, 0, 0))],
        out_specs=pl.BlockSpec((seq, dh), lambda h: (0, h)),
        out_shape=jax.ShapeDtypeStruct((seq, H * dh), BF16),
        scratch_shapes=[pltpu.VMEM((past + seq, dh), BF16), pltpu.VMEM((past + seq, 2 * dh), BF16)],
        compiler_params=_params("arbitrary"),
        name="attention_prompt",
    )(qkv, qkv, qkv, bias)


def _att_sample_kernel(q_ref, kn_ref, vn_ref, ck_ref, cv_ref, bp_ref, bn_ref, o_ref):
    dh = ATT_DH
    H = ATT_HEADS
    W = ck_ref.shape[0] // H
    for h in range(H):
        sl = slice(h * dh, (h + 1) * dh)
        qb = q_ref[:, sl].astype(BF16)
        kn = kn_ref[:, sl].astype(BF16)
        vn = vn_ref[:, sl].astype(BF16)
        kp = ck_ref[pl.ds(h, W, stride=H), :].astype(BF16)
        vp = cv_ref[pl.ds(h, W, stride=H), :].astype(BF16)
        sp = lax.dot_general(qb, kp, (((1,), (1,)), ((), ())), preferred_element_type=F32)
        sn = lax.dot_general(qb, kn, (((1,), (1,)), ((), ())), preferred_element_type=F32)
        sp = sp * dh ** -0.5 + bp_ref[h]
        sn = sn * dh ** -0.5 + bn_ref[h]
        m = jnp.maximum(jnp.max(sp, axis=-1, keepdims=True), jnp.max(sn, axis=-1, keepdims=True))
        ep = jnp.exp(sp - m)
        en = jnp.exp(sn - m)
        den = jnp.sum(ep, axis=-1, keepdims=True) + jnp.sum(en, axis=-1, keepdims=True)
        o = jnp.dot((ep / den).astype(BF16), vp, preferred_element_type=F32)
        o = o + jnp.dot((en / den).astype(BF16), vn, preferred_element_type=F32)
        o_ref[:, sl] = o.astype(o_ref.dtype)


def attention_sample(qkv, cache_k, cache_v, bias_past, bias_new, *, batch, seq, row0, layer):
    H, dh = ATT_HEADS, ATT_DH
    W = cache_k.shape[2]
    rb0 = row0 // seq
    cache_k = cache_k.reshape(-1, dh)
    cache_v = cache_v.reshape(-1, dh)
    return pl.pallas_call(
        _att_sample_kernel,
        grid=(batch,),
        in_specs=[pl.BlockSpec((seq, H * dh), lambda b: (rb0 + b, 0)),
                  pl.BlockSpec((seq, H * dh), lambda b: (rb0 + b, 1)),
                  pl.BlockSpec((seq, H * dh), lambda b: (rb0 + b, 2)),
                  pl.BlockSpec((W * H, dh), lambda b: (layer * batch + b, 0)),
                  pl.BlockSpec((W * H, dh), lambda b: (layer * batch + b, 0)),
                  pl.BlockSpec((H, seq, W), lambda b: (0, 0, 0)),
                  pl.BlockSpec((H, seq, seq), lambda b: (0, 0, 0))],
        out_specs=pl.BlockSpec((seq, H * dh), lambda b: (b, 0)),
        out_shape=jax.ShapeDtypeStruct((batch * seq, H * dh), BF16),
        compiler_params=_params("arbitrary"),
        name="attention_sample",
    )(qkv, qkv, qkv, cache_k, cache_v, bias_past, bias_new)


def _rel_bias(table, n_q, n_k, offset):
    period = n_q + n_k
    d = jnp.arange(period)
    d = jnp.where(d < n_k, d, d - period)
    u = jnp.take(table, jnp.clip(offset - d, -REL_CLIP, REL_CLIP) + REL_CLIP, axis=0).T.astype(F32)
    H = u.shape[0]
    skew = jnp.tile(u, (1, n_q))[:, :n_q * (period - 1)].reshape(H, n_q, period - 1)
    return skew[:, :, :n_k]


def _gmlp_kernel(uv_ref, lng_ref, ws_ref, bs_ref, y_ref, vn_ref):
    DV = y_ref.shape[1]
    cg = DV // GMLP_GROUPS
    v = jax.nn.gelu(uv_ref[:, DV:])
    mu = jnp.mean(v, axis=-1, keepdims=True)
    d = v - mu
    var = jnp.mean(d * d, axis=-1, keepdims=True)
    vn = d * lax.rsqrt(var + EPS) * lng_ref[...]
    vn_ref[...] = vn
    for g in range(GMLP_GROUPS):
        sl = slice(g * cg, (g + 1) * cg)
        mixed = jnp.dot(ws_ref[g], vn[:, sl].astype(BF16), preferred_element_type=F32) + bs_ref[g]
        y_ref[:, sl] = (jax.nn.gelu(uv_ref[:, sl]) * mixed).astype(y_ref.dtype)


def gmlp_gate(uv, ln_g, ws, bs, *, n_prompt_tiles, tile):
    T = uv.shape[0]
    DV = uv.shape[1] // 2
    G = GMLP_GROUPS
    nt = T // tile
    last = n_prompt_tiles - 1
    return pl.pallas_call(
        _gmlp_kernel,
        grid=(nt,),
        in_specs=[pl.BlockSpec((tile, 2 * DV), lambda i: (i, 0)),
                  pl.BlockSpec((1, DV), lambda i: (0, 0)),
                  pl.BlockSpec((None, G, tile, tile), lambda i: (i // n_prompt_tiles, 0, 0, 0)),
                  pl.BlockSpec((None, G, tile, 1), lambda i: (i // n_prompt_tiles, 0, 0, 0))],
        out_specs=[pl.BlockSpec((tile, DV), lambda i: (i, 0)),
                   pl.BlockSpec((tile, DV), lambda i: (jnp.maximum(i - last, 0), 0))],
        out_shape=[jax.ShapeDtypeStruct((T, DV), BF16),
                   jax.ShapeDtypeStruct(((nt - last) * tile, DV), F32)],
        compiler_params=_params("arbitrary"),
        name="gmlp_gate",
    )(uv, ln_g.reshape(1, DV), ws, bs)


def _router_kernel(h_ref, w_ref, idx_ref, gate_ref):
    logits = jnp.dot(h_ref[...], w_ref[...].astype(BF16), preferred_element_type=F32)
    lane = lax.broadcasted_iota(jnp.int32, logits.shape, 1)
    lane_f = lane.astype(F32)
    l1 = jnp.where(lane < N_EXPERTS, logits, -jnp.inf)
    v1 = jnp.max(l1, axis=-1, keepdims=True)
    i1 = jnp.min(jnp.where(l1 == v1, lane_f, float(LANES)), axis=-1, keepdims=True)
    l2 = jnp.where(lane_f == i1, -jnp.inf, l1)
    v2 = jnp.max(l2, axis=-1, keepdims=True)
    i2 = jnp.min(jnp.where(l2 == v2, lane_f, float(LANES)), axis=-1, keepdims=True)
    e2 = jnp.exp(v2 - v1)
    den = 1.0 + e2
    idx_ref[...] = jnp.where(lane == 0, i1, jnp.where(lane == 1, i2, 0.0)).astype(jnp.int32)
    gate_ref[...] = jnp.where(lane == 0, 1.0 / den, jnp.where(lane == 1, e2 / den, 0.0))


def router(h, w_router_padded, *, tm):
    T, D = h.shape
    return pl.pallas_call(
        _router_kernel,
        grid=(T // tm,),
        in_specs=[pl.BlockSpec((tm, D), lambda i: (i, 0)), pl.BlockSpec((D, LANES), lambda i: (0, 0))],
        out_specs=[pl.BlockSpec((tm, LANES), lambda i: (i, 0)), pl.BlockSpec((tm, LANES), lambda i: (i, 0))],
        out_shape=[jax.ShapeDtypeStruct((T, LANES), jnp.int32), jax.ShapeDtypeStruct((T, LANES), F32)],
        compiler_params=_params("arbitrary"),
        name="router",
    )(h, w_router_padded)


def _gather_kernel(lo_ref, hi_ref, p0_ref, p1_ref, h_ref, o_ref, acc):
    m = pl.program_id(0)
    tm = o_ref.shape[0]
    sb = GATHER_SRC_BLOCK
    row = m * tm + lax.broadcasted_iota(jnp.int32, (tm, sb), 0)
    acc[...] = jnp.zeros_like(acc)

    def body(b, carry):
        select = jnp.where(p0_ref[b] == row, 1.0, jnp.where(p1_ref[b] == row, 1.0, 0.0)).astype(h_ref.dtype)
        acc[...] += jnp.dot(select, h_ref[pl.ds(pl.multiple_of(b * sb, sb), sb), :], preferred_element_type=F32)
        return carry

    lax.fori_loop(lo_ref[m], hi_ref[m] + 1, body, 0)
    o_ref[...] = acc[...].astype(o_ref.dtype)


def gather_rows(h, pos0, pos1, blk_lo, blk_hi, n_rows, *, tm):
    T, D = h.shape
    nsb = T // GATHER_SRC_BLOCK
    grid_spec = pltpu.PrefetchScalarGridSpec(
        num_scalar_prefetch=2,
        grid=(n_rows // tm,),
        in_specs=[pl.BlockSpec((nsb, 1, GATHER_SRC_BLOCK), lambda i, lo, hi: (0, 0, 0)),
                  pl.BlockSpec((nsb, 1, GATHER_SRC_BLOCK), lambda i, lo, hi: (0, 0, 0)),
                  pl.BlockSpec((T, D), lambda i, lo, hi: (0, 0), pipeline_mode=pl.Buffered(1))],
        out_specs=pl.BlockSpec((tm, D), lambda i, lo, hi: (i, 0)),
        scratch_shapes=[pltpu.VMEM((tm, D), F32)],
    )
    return pl.pallas_call(
        _gather_kernel,
        grid_spec=grid_spec,
        out_shape=jax.ShapeDtypeStruct((n_rows, D), h.dtype),
        compiler_params=_params("arbitrary"),
        name="moe_gather",
    )(blk_lo, blk_hi, pos0.reshape(nsb, 1, GATHER_SRC_BLOCK), pos1.reshape(nsb, 1, GATHER_SRC_BLOCK), h)


def _on_valid_rows(valid, o_ref, compute):
    tm = o_ref.shape[0]
    half = tm // 2

    @pl.when(valid > half)
    def _():
        compute(slice(0, tm))

    @pl.when((valid > 0) & (valid <= half))
    def _():
        compute(slice(0, half))
        o_ref[half:, :] = jnp.zeros((tm - half, o_ref.shape[1]), o_ref.dtype)

    @pl.when(valid == 0)
    def _():
        o_ref[...] = jnp.zeros_like(o_ref)


def _group_weights(te_ref, first_ref, nxt_ref, wrap_ref, copies, cast):
    n = pl.program_id(0)
    m = pl.program_id(1)

    @pl.when(first_ref[m] == 1)
    def _():
        @pl.when((n == 0) & (m == 0))
        def _():
            for cp in copies(te_ref[0], 0):
                cp.start()

        for cp in copies(te_ref[m], n):
            cp.wait()
        cast()
        n_next = n + wrap_ref[m]

        @pl.when(n_next < pl.num_programs(0))
        def _():
            for cp in copies(nxt_ref[m], n_next):
                cp.start()


def _moe_up_kernel(te_ref, first_ref, valid_ref, nxt_ref, wrap_ref, x_ref, w_hbm, o_ref,
                   stage_a, stage_b, wa_bf, wb_bf, sems):
    m = pl.program_id(1)
    tn = wa_bf.shape[1]
    F = w_hbm.shape[2] // 2

    def copies(e, n):
        col = pl.multiple_of(n * tn, tn)
        return (pltpu.make_async_copy(w_hbm.at[e, :, pl.ds(col, tn)], stage_a, sems.at[0]),
                pltpu.make_async_copy(w_hbm.at[e, :, pl.ds(pl.multiple_of(F + col, tn), tn)], stage_b, sems.at[1]))

    def cast():
        wa_bf[...] = stage_a[...].astype(BF16)
        wb_bf[...] = stage_b[...].astype(BF16)

    _group_weights(te_ref, first_ref, nxt_ref, wrap_ref, copies, cast)

    def compute(rows):
        x = x_ref[rows, :]
        a = jnp.dot(x, wa_bf[...], preferred_element_type=F32)
        b = jnp.dot(x, wb_bf[...], preferred_element_type=F32)
        o_ref[rows, :] = (_silu(a) * b).astype(o_ref.dtype)

    _on_valid_rows(valid_ref[m], o_ref, compute)


def moe_up(xs, w_gu, plan, *, tm, tn):
    P, D = xs.shape
    F = w_gu.shape[2] // 2
    grid_spec = pltpu.PrefetchScalarGridSpec(
        num_scalar_prefetch=len(plan),
        grid=(F // tn, P // tm),
        in_specs=[pl.BlockSpec((tm, D), lambda n, m, *_: (m, 0)),
                  pl.BlockSpec(memory_space=pl.ANY)],
        out_specs=pl.BlockSpec((tm, tn), lambda n, m, *_: (m, n)),
        scratch_shapes=[pltpu.VMEM((D, tn), w_gu.dtype), pltpu.VMEM((D, tn), w_gu.dtype),
                        pltpu.VMEM((D, tn), BF16), pltpu.VMEM((D, tn), BF16), pltpu.SemaphoreType.DMA((2,))],
    )
    return pl.pallas_call(
        _moe_up_kernel,
        grid_spec=grid_spec,
        out_shape=jax.ShapeDtypeStruct((P, F), BF16),
        compiler_params=_params("arbitrary", "arbitrary"),
        name="moe_up",
    )(*plan, xs, w_gu)


def _moe_down_kernel(te_ref, first_ref, valid_ref, nxt_ref, wrap_ref, x_ref, w_hbm, o_ref, stage, w_bf, sem):
    m = pl.program_id(1)
    tn = w_bf.shape[1]

    def copies(e, n):
        return (pltpu.make_async_copy(w_hbm.at[e, :, pl.ds(pl.multiple_of(n * tn, tn), tn)], stage, sem),)

    def cast():
        w_bf[...] = stage[...].astype(BF16)

    _group_weights(te_ref, first_ref, nxt_ref, wrap_ref, copies, cast)

    def compute(rows):
        o_ref[rows, :] = jnp.dot(x_ref[rows, :], w_bf[...], preferred_element_type=F32)

    _on_valid_rows(valid_ref[m], o_ref, compute)


def moe_down(hm, w_down, plan, *, tm, tn):
    P, F = hm.shape
    D = w_down.shape[2]
    grid_spec = pltpu.PrefetchScalarGridSpec(
        num_scalar_prefetch=len(plan),
        grid=(D // tn, P // tm),
        in_specs=[pl.BlockSpec((tm, F), lambda n, m, *_: (m, 0)),
                  pl.BlockSpec(memory_space=pl.ANY)],
        out_specs=pl.BlockSpec((tm, tn), lambda n, m, *_: (m, n)),
        scratch_shapes=[pltpu.VMEM((F, tn), w_down.dtype), pltpu.VMEM((F, tn), BF16), pltpu.SemaphoreType.DMA(())],
    )
    return pl.pallas_call(
        _moe_down_kernel,
        grid_spec=grid_spec,
        out_shape=jax.ShapeDtypeStruct((P, D), F32),
        compiler_params=_params("arbitrary", "arbitrary"),
        name="moe_down",
    )(*plan, hm, w_down)


def _row_copy(y_hbm, buf, src_row, dst_row, sem):
    return pltpu.make_async_copy(y_hbm.at[pl.ds(src_row, 1), :], buf.at[pl.ds(dst_row, 1), :], sem)


def _combine_norm_kernel(p0_ref, p1_ref, x_ref, gate_ref, g_ref, y_hbm, xo_ref, hn_ref, buf0, buf1, sems):
    tm = x_ref.shape[0]
    base = pl.program_id(0) * tm

    def start(r, carry):
        _row_copy(y_hbm, buf0, p0_ref[base + r], r, sems.at[0]).start(priority=0)
        _row_copy(y_hbm, buf1, p1_ref[base + r], r, sems.at[1]).start(priority=DMA_PRIORITIES - 1)
        return carry

    def wait(r, carry):
        _row_copy(y_hbm, buf0, 0, r, sems.at[0]).wait()
        _row_copy(y_hbm, buf1, 0, r, sems.at[1]).wait()
        return carry

    lax.fori_loop(0, tm, start, 0)
    lax.fori_loop(0, tm, wait, 0)
    gate = gate_ref[...]
    xn = x_ref[...] + (gate[:, 0:1] * buf0[...] + gate[:, 1:2] * buf1[...])
    xo_ref[...] = xn
    hn = xn * lax.rsqrt(jnp.mean(xn * xn, axis=-1, keepdims=True) + EPS) * g_ref[...]
    hn_ref[...] = hn.astype(hn_ref.dtype)


def moe_combine_norm(x, gates, ys, pos0, pos1, g, *, tm):
    T, D = x.shape
    grid_spec = pltpu.PrefetchScalarGridSpec(
        num_scalar_prefetch=2,
        grid=(T // tm,),
        in_specs=[pl.BlockSpec((tm, D), lambda i, a, b: (i, 0)),
                  pl.BlockSpec((tm, LANES), lambda i, a, b: (i, 0)),
                  pl.BlockSpec((1, D), lambda i, a, b: (0, 0)),
                  pl.BlockSpec(memory_space=pl.ANY)],
        out_specs=[pl.BlockSpec((tm, D), lambda i, a, b: (i, 0)), pl.BlockSpec((tm, D), lambda i, a, b: (i, 0))],
        scratch_shapes=[pltpu.VMEM((tm, D), F32), pltpu.VMEM((tm, D), F32), pltpu.SemaphoreType.DMA((2,))],
    )
    return pl.pallas_call(
        _combine_norm_kernel,
        grid_spec=grid_spec,
        out_shape=[jax.ShapeDtypeStruct((T, D), F32), jax.ShapeDtypeStruct((T, D), BF16)],
        compiler_params=_params("arbitrary"),
        name="moe_combine_norm",
    )(pos0, pos1, x, gates, g.reshape(1, D), ys)


def _route_plan(idx, tm):
    T = idx.shape[0]
    E = N_EXPERTS
    n_tiles = (TOP_K * T) // tm + E
    experts = jnp.arange(E, dtype=jnp.int32)
    flat_e = idx.reshape(-1)
    onehot = (flat_e[:, None] == experts[None, :]).astype(jnp.int32)
    csum = jnp.cumsum(onehot, axis=0)
    rank = jnp.sum((csum - onehot) * onehot, axis=1)
    counts = csum[-1]
    tiles_e = (counts + tm - 1) // tm
    tile_end = jnp.cumsum(tiles_e)
    tile_start = tile_end - tiles_e
    pos = (jnp.sum(onehot * tile_start[None, :], axis=1) * tm + rank).astype(jnp.int32)
    n_used = tile_end[-1]
    m_ids = jnp.arange(n_tiles, dtype=jnp.int32)
    te = jnp.sum((jnp.minimum(m_ids, n_used - 1)[:, None] >= tile_end[None, :]).astype(jnp.int32), axis=1)
    first = ((m_ids == 0) | (te != jnp.roll(te, 1))) & (m_ids < n_used)
    te_onehot = (te[:, None] == experts[None, :]).astype(jnp.int32)
    rows_before = (m_ids - jnp.sum(te_onehot * tile_start[None, :], axis=1)) * tm
    valid = jnp.clip(jnp.sum(te_onehot * counts[None, :], axis=1) - rows_before, 0, tm)
    valid = jnp.where(m_ids < n_used, valid, 0).astype(jnp.int32)
    nonempty = jnp.where(tiles_e > 0, experts, E)
    after = jnp.min(jnp.where(experts[None, :] > experts[:, None], nonempty[None, :], E), axis=1)
    nxt_e = jnp.where(after < E, after, jnp.min(nonempty))
    nxt = jnp.sum(te_onehot * nxt_e[None, :], axis=1).astype(jnp.int32)
    wrap = jnp.sum(te_onehot * (after == E).astype(jnp.int32)[None, :], axis=1).astype(jnp.int32)
    csum_tile = jnp.sum(csum[:, :, None] * te_onehot.T[None, :, :], axis=1)
    pair_first = jnp.sum((csum_tile <= rows_before[None, :]).astype(jnp.int32), axis=0)
    pair_last = jnp.sum((csum_tile <= (rows_before + valid - 1)[None, :]).astype(jnp.int32), axis=0)
    blk_lo = jnp.where(valid > 0, pair_first // (TOP_K * GATHER_SRC_BLOCK), 0).astype(jnp.int32)
    blk_hi = jnp.where(valid > 0, pair_last // (TOP_K * GATHER_SRC_BLOCK), -1).astype(jnp.int32)
    return (pos[0::TOP_K], pos[1::TOP_K], blk_lo, blk_hi,
            (te.astype(jnp.int32), first.astype(jnp.int32), valid, nxt, wrap))


def moe(x, h, w_router, w_gu, w_down, g_next, *, layer, tm_tok):
    T = x.shape[0]
    E = N_EXPERTS
    w_r = jnp.pad(w_router[layer], ((0, 0), (0, LANES - E)))
    idx, gates = router(h, w_r, tm=tm_tok)
    pos0, pos1, blk_lo, blk_hi, (te, first, valid, nxt, wrap) = _route_plan(idx[:, :TOP_K], MOE_TM)
    plan = (te + layer * E, first, valid, nxt + layer * E, wrap)
    n_rows = te.shape[0] * MOE_TM
    xs = gather_rows(h, pos0, pos1, blk_lo, blk_hi, n_rows, tm=MOE_TM)
    hm = moe_up(xs, w_gu.reshape((-1,) + w_gu.shape[2:]), plan, tm=MOE_TM, tn=1024)
    ys = moe_down(hm, w_down.reshape((-1,) + w_down.shape[2:]), plan, tm=MOE_TM, tn=512)
    return moe_combine_norm(x, gates, ys, pos0, pos1, g_next, tm=tm_tok)


def kernel(x_prompt, x_sample, state_ret, cache_k, cache_v, p_prompt, p_sample, g_mix, g_ffn, g_ple, g_final,
           ret_w_in, ret_w_out, att_w_in, att_w_out, att_rel_bias, gmlp_w_in, gmlp_ln_g, gmlp_w_s, gmlp_b_s,
           gmlp_w_out, ffn_w_gu, ffn_w_down, moe_w_router, moe_w_gu, moe_w_down, ple_w_in, ple_w_gate):
    B, S, D = x_prompt.shape
    BS, L, _ = x_sample.shape
    depth = g_mix.shape[0]
    NP, NS = B * S, BS * L
    T = NP + NS
    assert B == 1 and S % RET_CHUNK == 0 and S % ATT_QBLOCK == 0 and L <= CHUNK and NP % NS == 0
    TM = T // 8
    TMS = T // 16

    x = jnp.concatenate([x_prompt.reshape(NP, D), x_sample.reshape(NS, D)], axis=0)
    p_all = jnp.concatenate([p_prompt.reshape(depth, NP, -1), p_sample.reshape(depth, NS, -1)], axis=1)

    pos = jnp.concatenate([jnp.arange(S, dtype=jnp.int32), jnp.tile(PAST_LEN + jnp.arange(L, dtype=jnp.int32), BS)])
    freqs = ROPE_BASE ** (-jnp.arange(0, RET_DK, 2, dtype=F32) / RET_DK)
    ang = pos.astype(F32)[:, None] * freqs[None, :]
    cos, sin = jnp.cos(ang), jnp.sin(ang)
    log_g = jnp.log1p(-jnp.exp2(-5.0 - jnp.arange(RET_HEADS, dtype=F32)))

    ret_p, ret_s = None, None
    k_p, v_p, k_s, v_s, gv = [], [], [], [], []
    (h,) = rmsnorm(x, g_mix[0], [BF16], TMS)
    for i in range(depth):
        kind, slot = i % 3, i // 3
        if kind == 0:
            qkvg = matmul(h, ret_w_in, layer=slot, tm=TM, tn=1024)
            y_p, ret_p = retention(qkvg, cos, sin, log_g, None, ret_p, batch=B, seq=S, chunk=RET_CHUNK, row0=0,
                                   heads_per_step=RET_HEADS)
            y_s, ret_s = retention(qkvg, cos, sin, log_g, state_ret, ret_s, batch=BS, seq=L, chunk=L, row0=NP,
                                   heads_per_step=RET_HEADS, state_layer=slot)
            x = matmul(y_p, ret_w_out, layer=slot, tm=NS, tn=512, res=x, a_tail=y_s)
        elif kind == 1:
            HD = ATT_HEADS * ATT_DH
            W = cache_k.shape[2]
            past = BAND_PAST_CHUNKS * CHUNK
            qkv = matmul(h, att_w_in, layer=slot, tm=TM, tn=1024)
            bias_p = _rel_bias(att_rel_bias[slot], ATT_QBLOCK, past + ATT_QBLOCK, past)
            bias_s = _rel_bias(att_rel_bias[slot], L, W + L, W)
            o_p = attention_prompt(qkv, bias_p, seq=S)
            o_s = attention_sample(qkv, cache_k, cache_v, bias_s[:, :, :W], bias_s[:, :, W:],
                                   batch=BS, seq=L, row0=NP, layer=slot)
            w = min(past, S)
            k_p.append(qkv[NP - w:NP, HD:2 * HD].reshape(B, w, ATT_HEADS, ATT_DH))
            v_p.append(qkv[NP - w:NP, 2 * HD:].reshape(B, w, ATT_HEADS, ATT_DH))
            k_s.append(qkv[NP:, HD:2 * HD].reshape(BS, L, ATT_HEADS, ATT_DH))
            v_s.append(qkv[NP:, 2 * HD:].reshape(BS, L, ATT_HEADS, ATT_DH))
            x = matmul(o_p, att_w_out, layer=slot, tm=NS, tn=1024, res=x, a_tail=o_s)
        else:
            tile = GMLP_CHUNK
            uv = matmul(h, gmlp_w_in, layer=slot, tm=TM, tn=1024)
            r = jnp.arange(tile)
            ws_p = jnp.where((r[:, None] >= r[None, :])[None], gmlp_w_s[slot][:, :tile, :tile], 0.0)
            same = (r[:, None] // L == r[None, :] // L) & (r[:, None] >= r[None, :])
            ws_s = jnp.where(same[None], jnp.tile(gmlp_w_s[slot][:, :L, :L], (1, tile // L, tile // L)), 0.0)
            bs_p = gmlp_b_s[slot][:, :tile]
            bs_s = jnp.tile(gmlp_b_s[slot][:, :L], (1, tile // L))
            ws = jnp.stack([ws_p, ws_s]).astype(BF16)
            bs = jnp.stack([bs_p, bs_s])[..., None]
            yg, vn = gmlp_gate(uv, gmlp_ln_g[slot], ws, bs, n_prompt_tiles=NP // tile, tile=tile)
            gv.append((vn[:tile].reshape(B, tile, -1), vn[tile:].reshape(BS, L, -1)))
            x = matmul(yg, gmlp_w_out, layer=slot, tm=TMS, tn=512, res=x)

        if i % 2 == 0:
            (h,) = rmsnorm(x, g_ffn[i], [BF16], TMS)
            x = dense_ffn(x, h, ffn_w_gu, ffn_w_down, layer=i // 2, tm_up=TM, tm_down=TMS)
            (hn,) = rmsnorm(x, g_ple[i], [BF16], TMS)
        else:
            (h,) = rmsnorm(x, g_ffn[i], [BF16], TMS)
            x, hn = moe(x, h, moe_w_router, moe_w_gu, moe_w_down, g_ple[i], layer=i // 2, tm_tok=TMS)

        g_next, next_dtype = (g_mix[i + 1], BF16) if i + 1 < depth else (g_final, F32)
        x, h = ple_norm(x, hn, p_all, ple_w_gate, ple_w_in, g_next, next_dtype, layer=i, tm=T // 32)

    y = h
    y_prompt = y[:NP].reshape(B, S, D)
    y_sample = y[NP:].reshape(BS, L, D)
    return (y_prompt, y_sample, ret_p, ret_s, jnp.stack(k_p), jnp.stack(v_p),
            jnp.stack(k_s), jnp.stack(v_s), jnp.stack([a for a, _ in gv]), jnp.stack([b for _, b in gv]))
```
